```python
import math
import jax, jax.numpy as jnp
from jax import lax
import numpy as np

D_MODEL = 2048
BATCH = 4
SEQ = 8192
DEPTH = 1
DEC_BATCH = 2
DEC_SEQ = 8192
PAST_LEN = 128

MIX_WIDTH = D_MODEL
ATTN_WIDTH = MIX_WIDTH // 2
LRU_WIDTH = MIX_WIDTH - ATTN_WIDTH
HEAD_DIM = 128
N_HEADS = ATTN_WIDTH // HEAD_DIM
DILATED_BRANCHES = ((128, 1), (512, 4), (2048, 16))
ROPE_THETA = 10000.0
LRU_BLOCK = 128
LRU_N_BLOCKS = LRU_WIDTH // LRU_BLOCK
LRU_C = 8.0
CONV_WIDTH = 4
CONV_LEFT = 2
PEER_HEADS = 8
PEER_N_KEYS = 128
PEER_N_EXPERTS = PEER_N_KEYS * PEER_N_KEYS
PEER_TOPK = 16
PEER_KEY_DIM = 256
PEER_HALF = PEER_KEY_DIM // 2
TOKEN_CHUNK = 128
N_MOD = 6
IN_WIDTH = 3 * ATTN_WIDTH + 2 * LRU_WIDTH
EPS = 1e-6
MASK_VALUE = -1e30

kernel_name = "hybrid_rglru_dilated_peer_encoder"


def rmsnorm(x, g):
    xf = x.astype(jnp.float32)
    y = xf * lax.rsqrt(jnp.mean(xf * xf, axis=-1, keepdims=True) + EPS)
    return (y * g.astype(jnp.float32)).astype(x.dtype)


def rope(x):
    S = x.shape[1]
    half = HEAD_DIM // 2
    inv = ROPE_THETA ** (-jnp.arange(half, dtype=jnp.float32) / half)
    ang = jnp.arange(S, dtype=jnp.float32)[:, None] * inv[None, :]
    cos = jnp.cos(ang)[None, :, None, :]
    sin = jnp.sin(ang)[None, :, None, :]
    x1, x2 = x[..., :half], x[..., half:]
    return jnp.concatenate([x1 * cos - x2 * sin, x2 * cos + x1 * sin], axis=-1)


def dilated_branch(q, k, v, window, dilation):
    B, S, H, E = q.shape
    R = window // (2 * dilation)
    Q = R
    L = S // dilation
    nb = -(-L // Q)
    Lp = nb * Q

    def strided(t):
        return t.reshape(B, L, dilation, H, E)

    qs = jnp.pad(strided(q), ((0, 0), (0, Lp - L), (0, 0), (0, 0), (0, 0)))
    qs = qs.reshape(B, nb, Q, dilation, H, E)
    pad_kv = ((0, 0), (Q, Lp - L + Q), (0, 0), (0, 0), (0, 0))
    kp = jnp.pad(strided(k), pad_kv).reshape(B, nb + 2, Q, dilation, H, E)
    vp = jnp.pad(strided(v), pad_kv).reshape(B, nb + 2, Q, dilation, H, E)
    kw = jnp.concatenate([kp[:, :-2], kp[:, 1:-1], kp[:, 2:]], axis=2)
    vw = jnp.concatenate([vp[:, :-2], vp[:, 1:-1], vp[:, 2:]], axis=2)

    qi = jnp.arange(Q)[:, None]
    kj = jnp.arange(3 * Q)[None, :]
    band = jnp.abs(Q + qi - kj) <= R
    kpos = (jnp.arange(nb)[:, None] - 1) * Q + jnp.arange(3 * Q)[None, :]
    inside = (kpos >= 0) & (kpos < L)
    mask = band[None, :, :] & inside[:, None, :]

    s = jnp.einsum('bnqrhe,bnkrhe->bnrhqk', qs, kw) * (E ** -0.5)
    s = jnp.where(mask[None, :, None, None], s, MASK_VALUE)
    m = jnp.max(s, axis=-1, keepdims=True)
    p = jnp.exp(s - m)
    den = jnp.sum(p, axis=-1, keepdims=True)
    o = jnp.einsum('bnrhqk,bnkrhe->bnqrhe', p / den, vw)
    lse = (m + jnp.log(den))[..., 0]
    o = o.reshape(B, Lp, dilation, H, E)[:, :L].reshape(B, S, H, E)
    lse = jnp.moveaxis(lse, -1, 2).reshape(B, Lp, dilation, H)[:, :L].reshape(B, S, H)
    return o, lse


def dilated_attention(q, k, v):
    B, S, _ = q.shape

    def heads(t):
        return t.reshape(B, S, N_HEADS, HEAD_DIM).astype(jnp.float32)

    qh = rope(heads(q))
    kh = rope(heads(k))
    vh = heads(v)
    outs, lses = [], []
    for window, dilation in DILATED_BRANCHES:
        o, l = dilated_branch(qh, kh, vh, window, dilation)
        outs.append(o)
        lses.append(l)
    wts = jax.nn.softmax(jnp.stack(lses, axis=0), axis=0)
    y = jnp.sum(wts[..., None] * jnp.stack(outs, axis=0), axis=0)
    return y.reshape(B, S, ATTN_WIDTH).astype(q.dtype)


def lru_combine(left, right):
    a1, b1 = left
    a2, b2 = right
    return (a1 * a2, a2 * b1 + b2)


def rglru_mixer(xb, gate, conv_w, conv_b, w_r, b_r, w_i, b_i, lam):
    B, S, W = xb.shape
    xp = jnp.pad(xb, ((0, 0), (CONV_LEFT, CONV_WIDTH - 1 - CONV_LEFT), (0, 0)))
    xc = conv_b
    for j in range(CONV_WIDTH):
        xc = xc + xp[:, j:j + S] * conv_w[j]
    xf = xc.astype(jnp.float32)
    xblk = xf.reshape(B, S, LRU_N_BLOCKS, LRU_BLOCK)

    def direction(dirn, reverse):
        r = jax.nn.sigmoid(jnp.einsum('bsnc,ncd->bsnd', xblk, w_r[dirn].astype(jnp.float32)).reshape(B, S, W)
                           + b_r[dirn].astype(jnp.float32))
        i = jax.nn.sigmoid(jnp.einsum('bsnc,ncd->bsnd', xblk, w_i[dirn].astype(jnp.float32)).reshape(B, S, W)
                           + b_i[dirn].astype(jnp.float32))
        log_a = -LRU_C * r * jax.nn.softplus(-lam[dirn].astype(jnp.float32))
        a = jnp.exp(log_a)
        bterm = jnp.sqrt(jnp.maximum(-jnp.expm1(2.0 * log_a), 0.0)) * (i * xf)
        _, h = lax.associative_scan(lru_combine, (a, bterm), axis=1, reverse=reverse)
        return h

    h = direction(0, False) + direction(1, True)
    y = h * jax.nn.gelu(gate.astype(jnp.float32))
    return y.astype(xb.dtype)


def peer(h, w_q, sub_keys, expert_u, expert_v):
    B, S, D = h.shape
    T = B * S
    hc = h.reshape(T // TOKEN_CHUNK, TOKEN_CHUNK, D)
    sk = sub_keys.astype(jnp.float32)

    def chunk(xc):
        C = xc.shape[0]
        q = (xc @ w_q).reshape(C, PEER_HEADS, 2, PEER_HALF).astype(jnp.float32)
        s = jnp.einsum('thpe,hpne->thpn', q, sk)
        sv, si = lax.top_k(s, PEER_TOPK)
        cand = (sv[:, :, 0, :, None] + sv[:, :, 1, None, :]).reshape(C, PEER_HEADS, PEER_TOPK * PEER_TOPK)
        cid = (si[:, :, 0, :, None] * PEER_N_KEYS + si[:, :, 1, None, :]).reshape(C, PEER_HEADS, PEER_TOPK * PEER_TOPK)
        top_s, pos = lax.top_k(cand, PEER_TOPK)
        ids = jnp.take_along_axis(cid, pos, axis=-1)
        g = jax.nn.softmax(top_s, axis=-1)
        u = jnp.take(expert_u, ids, axis=0)
        act = jax.nn.gelu(jnp.einsum('thkd,td->thk', u, xc).astype(jnp.float32))
        v = jnp.take(expert_v, ids, axis=0)
        return jnp.einsum('thk,thkd->td', (g * act).astype(xc.dtype), v)

    return lax.map(chunk, hc).reshape(B, S, D)


def trunk(x, c, w_mod, b_mod, norm1_g, w_in, conv_w, conv_b, lru_w_r, lru_b_r, lru_w_i, lru_b_i,
          lru_lambda, attn_out_g, lru_out_g, w_out, norm2_g, peer_w_q, peer_sub_keys, peer_u, peer_v,
          norm_final_g):
    A = ATTN_WIDTH
    for l in range(DEPTH):
        mod = jax.nn.silu(c.astype(jnp.float32)) @ w_mod[l].astype(jnp.float32) + b_mod[l].astype(jnp.float32)
        sh1, sc1, g1, sh2, sc2, g2 = jnp.split(mod.astype(x.dtype)[:, None, :], N_MOD, axis=-1)
        h = rmsnorm(x, norm1_g[l]) * (1 + sc1) + sh1
        z = h @ w_in[l]
        q, k, v, xb, gate = jnp.split(z, [A, 2 * A, 3 * A, 3 * A + LRU_WIDTH], axis=-1)
        attn = dilated_attention(q, k, v)
        lru = rglru_mixer(xb, gate, conv_w[l], conv_b[l], lru_w_r[l], lru_b_r[l], lru_w_i[l], lru_b_i[l],
                          lru_lambda[l])
        mix = jnp.concatenate([rmsnorm(attn, attn_out_g[l]), rmsnorm(lru, lru_out_g[l])], axis=-1) @ w_out[l]
        x = x + g1 * mix
        h2 = rmsnorm(x, norm2_g[l]) * (1 + sc2) + sh2
        x = x + g2 * peer(h2, peer_w_q[l], peer_sub_keys[l], peer_u[l], peer_v[l])
    return rmsnorm(x, norm_final_g)


def setup_inputs(seed: int = 0) -> dict:
    key = jax.random.key(seed)
    ks = jax.random.split(key, 24)
    f32 = jnp.float32
    D = D_MODEL

    def nrm(k, shape, scale):
        return jax.random.normal(k, shape, f32) * scale

    a0 = jax.random.uniform(ks[14], (DEPTH, 2, LRU_WIDTH), f32, minval=0.9, maxval=0.999)
    return {
        "x_prompt": nrm(ks[0], (BATCH, SEQ, D), 1.0),
        "x_sample": nrm(ks[1], (DEC_BATCH, DEC_SEQ, D), 1.0),
        "c_prompt": nrm(ks[2], (BATCH, D), 1.0),
        "c_sample": nrm(ks[3], (DEC_BATCH, D), 1.0),
        "w_mod": nrm(ks[4], (DEPTH, D, N_MOD * D), 0.5 * D ** -0.5),
        "b_mod": nrm(ks[5], (DEPTH, N_MOD * D), 0.02),
        "norm1_g": 1.0 + nrm(ks[6], (DEPTH, D), 0.02),
        "w_in": nrm(ks[7], (DEPTH, D, IN_WIDTH), D ** -0.5),
        "conv_w": nrm(ks[8], (DEPTH, CONV_WIDTH, LRU_WIDTH), CONV_WIDTH ** -0.5),
        "conv_b": nrm(ks[9], (DEPTH, LRU_WIDTH), 0.02),
        "lru_w_r": nrm(ks[10], (DEPTH, 2, LRU_N_BLOCKS, LRU_BLOCK, LRU_BLOCK), LRU_BLOCK ** -0.5),
        "lru_b_r": nrm(ks[11], (DEPTH, 2, LRU_WIDTH), 0.02),
        "lru_w_i": nrm(ks[12], (DEPTH, 2, LRU_N_BLOCKS, LRU_BLOCK, LRU_BLOCK), LRU_BLOCK ** -0.5),
        "lru_b_i": nrm(ks[13], (DEPTH, 2, LRU_WIDTH), 0.02),
        "lru_lambda": jnp.log(a0) - jnp.log1p(-a0),
        "attn_out_g": 1.0 + nrm(ks[15], (DEPTH, ATTN_WIDTH), 0.02),
        "lru_out_g": 1.0 + nrm(ks[16], (DEPTH, LRU_WIDTH), 0.02),
        "w_out": nrm(ks[17], (DEPTH, MIX_WIDTH, D), MIX_WIDTH ** -0.5),
        "norm2_g": 1.0 + nrm(ks[18], (DEPTH, D), 0.02),
        "peer_w_q": nrm(ks[19], (DEPTH, D, PEER_HEADS * PEER_KEY_DIM), D ** -0.5),
        "peer_sub_keys": nrm(ks[20], (DEPTH, PEER_HEADS, 2, PEER_N_KEYS, PEER_HALF), PEER_HALF ** -0.5),
        "peer_u": nrm(ks[21], (DEPTH, PEER_N_EXPERTS, D), D ** -0.5),
        "peer_v": nrm(ks[22], (DEPTH, PEER_N_EXPERTS, D), 0.5),
        "norm_final_g": 1.0 + nrm(ks[23], (D,), 0.02),
    }


def reference(x_prompt, x_sample, c_prompt, c_sample, w_mod, b_mod, norm1_g, w_in, conv_w, conv_b,
              lru_w_r, lru_b_r, lru_w_i, lru_b_i, lru_lambda, attn_out_g, lru_out_g, w_out, norm2_g,
              peer_w_q, peer_sub_keys, peer_u, peer_v, norm_final_g):
    y_prompt = trunk(x_prompt, c_prompt, w_mod, b_mod, norm1_g, w_in, conv_w, conv_b, lru_w_r, lru_b_r,
                     lru_w_i, lru_b_i, lru_lambda, attn_out_g, lru_out_g, w_out, norm2_g, peer_w_q,
                     peer_sub_keys, peer_u, peer_v, norm_final_g)
    y_sample = trunk(x_sample, c_sample, w_mod, b_mod, norm1_g, w_in, conv_w, conv_b, lru_w_r, lru_b_r,
                     lru_w_i, lru_b_i, lru_lambda, attn_out_g, lru_out_g, w_out, norm2_g, peer_w_q,
                     peer_sub_keys, peer_u, peer_v, norm_final_g)
    return (y_prompt, y_sample)
```

```python
import functools
import math

import jax
import jax.numpy as jnp
from jax import lax
from jax.experimental import pallas as pl
from jax.experimental.pallas import tpu as pltpu

F32 = jnp.float32
BF16 = jnp.bfloat16

HEAD_DIM = 128
N_HEADS = 8
ATTN_WIDTH = N_HEADS * HEAD_DIM
LRU_BLOCK = 128
DILATIONS = (16, 4, 1)
BAND_RADIUS = 64
ROPE_THETA = 10000.0
LRU_C = 8.0
PEER_HEADS = 8
PEER_KEYS = 128
PEER_TOPK = 16
N_MOD = 6
EPS = 1e-6
MASK_VALUE = -1e30
NEG_INF = float("-inf")

V7X_VMEM_BYTES = 64 * 1024 * 1024
VMEM_LIMIT = V7X_VMEM_BYTES - 8 * 1024 * 1024
LANES = 128
SUBLANES = 8


def _cparams(*sem):
    return pltpu.CompilerParams(dimension_semantics=sem, vmem_limit_bytes=VMEM_LIMIT)


def _const_spec(shape):
    nd = len(shape)
    return pl.BlockSpec(shape, lambda *_: (0,) * nd, pipeline_mode=pl.Buffered(1))


def _rms(x, g):
    return x * lax.rsqrt(jnp.mean(x * x, axis=-1, keepdims=True) + EPS) * g


def _gelu(x):
    return 0.5 * x * (1.0 + jnp.tanh(math.sqrt(2.0 / math.pi) * (x + 0.044715 * (x * x * x))))


def _mod_kernel(c_ref, w_ref, b_ref, o_ref):
    c = c_ref[...]
    sc = c * jax.nn.sigmoid(c)
    o_ref[...] = jnp.dot(sc, w_ref[...], precision=lax.Precision.HIGHEST,
                         preferred_element_type=F32) + b_ref[...]


def _mod_call(c8, w_mod, b_mod):
    rows, d = c8.shape
    n = w_mod.shape[1]
    tn = 1024
    return pl.pallas_call(
        _mod_kernel,
        grid=(n // tn,),
        in_specs=[pl.BlockSpec((rows, d), lambda j: (0, 0)),
                  pl.BlockSpec((d, tn), lambda j: (0, j)),
                  pl.BlockSpec((1, tn), lambda j: (0, j))],
        out_specs=pl.BlockSpec((rows, tn), lambda j: (0, j)),
        out_shape=jax.ShapeDtypeStruct((rows, n), F32),
        compiler_params=_cparams("arbitrary"),
    )(c8, w_mod, b_mod)


def _rope_kernel(inv_ref, cos_ref, sin_ref):
    ts = cos_ref.shape[0]
    pos = (pl.program_id(0) * ts + lax.broadcasted_iota(jnp.int32, (ts, HEAD_DIM), 0)).astype(F32)
    lane = lax.broadcasted_iota(jnp.int32, (ts, HEAD_DIM), 1)
    ang = pos * inv_ref[...]
    cos_ref[...] = jnp.cos(ang)
    sn = jnp.sin(ang)
    sin_ref[...] = jnp.where(lane < HEAD_DIM // 2, -sn, sn)


def _rope_call(seq):
    half = HEAD_DIM // 2
    inv = ROPE_THETA ** (-jnp.arange(half, dtype=F32) / half)
    inv2 = jnp.concatenate([inv, inv])[None, :]
    ts = min(seq, 1024)
    return pl.pallas_call(
        _rope_kernel,
        grid=(seq // ts,),
        in_specs=[pl.BlockSpec((1, HEAD_DIM), lambda i: (0, 0))],
        out_specs=[pl.BlockSpec((ts, HEAD_DIM), lambda i: (i, 0))] * 2,
        out_shape=[jax.ShapeDtypeStruct((seq, HEAD_DIM), F32)] * 2,
        compiler_params=_cparams("arbitrary"),
    )(inv2)


def _inproj_kernel(x_ref, mod_ref, g_ref, w_ref, cos_ref, sin_ref,
                   q_ref, k_ref, v_ref, xb_ref, gate_ref):
    x = x_ref[...]
    modv = mod_ref[...]
    sh1, sc1 = modv[0:1], modv[1:2]
    h = _rms(x, g_ref[...]) * (1.0 + sc1) + sh1
    hb = h.astype(BF16)
    cosv = cos_ref[...]
    sinv = sin_ref[...]
    aw = ATTN_WIDTH

    def rope_store(z, out_ref, scale):
        for hd in range(N_HEADS):
            zs = z[:, hd * HEAD_DIM:(hd + 1) * HEAD_DIM]
            rot = pltpu.roll(zs, HEAD_DIM // 2, axis=1)
            r = zs * cosv + rot * sinv
            if scale != 1.0:
                r = r * scale
            out_ref[:, hd * HEAD_DIM:(hd + 1) * HEAD_DIM] = r.astype(out_ref.dtype)

    zq = jnp.dot(hb, w_ref[:, 0:aw], preferred_element_type=F32)
    rope_store(zq, q_ref, HEAD_DIM ** -0.5)
    zk = jnp.dot(hb, w_ref[:, aw:2 * aw], preferred_element_type=F32)
    rope_store(zk, k_ref, 1.0)
    v_ref[...] = jnp.dot(hb, w_ref[:, 2 * aw:3 * aw], preferred_element_type=F32).astype(v_ref.dtype)
    lw = xb_ref.shape[-1]
    xb_ref[...] = jnp.dot(hb, w_ref[:, 3 * aw:3 * aw + lw], preferred_element_type=F32)
    gate_ref[...] = jnp.dot(hb, w_ref[:, 3 * aw + lw:3 * aw + 2 * lw], preferred_element_type=F32)


def _inproj_call(x, mod3, g1, w_in_bf, cosf, sinf):
    b, s, d = x.shape
    lw = (w_in_bf.shape[1] - 3 * ATTN_WIDTH) // 2
    tm = 256
    tok = lambda i, j: (i, j, 0)
    return pl.pallas_call(
        _inproj_kernel,
        grid=(b, s // tm),
        in_specs=[pl.BlockSpec((None, tm, d), tok),
                  pl.BlockSpec((None, N_MOD, d), lambda i, j: (i, 0, 0)),
                  _const_spec((1, d)),
                  _const_spec(w_in_bf.shape),
                  pl.BlockSpec((tm, HEAD_DIM), lambda i, j: (j, 0)),
                  pl.BlockSpec((tm, HEAD_DIM), lambda i, j: (j, 0))],
        out_specs=[pl.BlockSpec((None, tm, ATTN_WIDTH), tok)] * 3 + [pl.BlockSpec((None, tm, lw), tok)] * 2,
        out_shape=[jax.ShapeDtypeStruct((b, s, ATTN_WIDTH), BF16)] * 3
        + [jax.ShapeDtypeStruct((b, s, lw), F32)] * 2,
        compiler_params=_cparams("arbitrary", "arbitrary"),
    )(x, mod3, g1, w_in_bf, cosf, sinf)


ATT_QB = 128
ATT_KW = ATT_QB + 2 * BAND_RADIUS


def _attn_kernel(*refs, seq_len, has_prev, final):
    if has_prev:
        q_ref, k_ref, v_ref, po_ref, pl_ref = refs[:5]
        outs = refs[5:]
    else:
        q_ref, k_ref, v_ref = refs[:3]
        outs = refs[3:]
    o_ref = outs[0]
    lq = q_ref.shape[0]
    base = pl.program_id(3) * lq
    rel0 = (lax.broadcasted_iota(jnp.int32, (ATT_QB, ATT_KW), 1)
            - lax.broadcasted_iota(jnp.int32, (ATT_QB, ATT_KW), 0))
    for j in range(lq // ATT_QB):
        rows = slice(j * ATT_QB, (j + 1) * ATT_QB)
        qs = base + j * ATT_QB
        ks = jnp.clip(qs - BAND_RADIUS, 0, seq_len - ATT_KW)
        ks = pl.multiple_of(ks, BAND_RADIUS)
        q = q_ref[rows, :]
        kw = k_ref[pl.ds(ks, ATT_KW), :]
        vw = v_ref[pl.ds(ks, ATT_KW), :]
        s = lax.dot_general(q, kw, (((1,), (1,)), ((), ())), preferred_element_type=F32)
        rel = rel0 + (ks - qs)
        s = jnp.where(jnp.abs(rel) <= BAND_RADIUS, s, MASK_VALUE)
        m = jnp.max(s, axis=-1, keepdims=True)
        p = jnp.exp(s - m)
        den = jnp.sum(p, axis=-1, keepdims=True)
        o = jnp.dot(p.astype(BF16), vw, preferred_element_type=F32) * (1.0 / den)
        lse = jnp.broadcast_to(m + jnp.log(den), (ATT_QB, HEAD_DIM))
        if has_prev:
            po = po_ref[rows, :].astype(F32)
            pls = pl_ref[rows, :]
            mx = jnp.maximum(pls, lse)
            w1 = jnp.exp(pls - mx)
            w2 = jnp.exp(lse - mx)
            tot = w1 + w2
            o = (po * w1 + o * w2) * (1.0 / tot)
            lse = mx + jnp.log(tot)
        o_ref[rows, :] = o.astype(o_ref.dtype)
        if not final:
            outs[1][rows, :] = lse


def _attn_branch(q, k, v, dil, prev, final):
    b, s, aw = q.shape
    sl = s // dil
    lq = min(sl, 512)
    assert sl >= ATT_KW and sl % lq == 0
    qv, kv, vv = (t.reshape(b, sl, dil * aw) for t in (q, k, v))
    col = lambda i, r, h, j: (i, j, r * N_HEADS + h)
    full = lambda i, r, h, j: (i, 0, r * N_HEADS + h)
    in_specs = [pl.BlockSpec((None, lq, HEAD_DIM), col),
                pl.BlockSpec((None, sl, HEAD_DIM), full),
                pl.BlockSpec((None, sl, HEAD_DIM), full)]
    args = [qv, kv, vv]
    if prev is not None:
        in_specs += [pl.BlockSpec((None, lq, HEAD_DIM), col)] * 2
        args += [prev[0].reshape(b, sl, dil * aw), prev[1].reshape(b, sl, dil * aw)]
    out_specs = [pl.BlockSpec((None, lq, HEAD_DIM), col)]
    out_shape = [jax.ShapeDtypeStruct((b, sl, dil * aw), BF16)]
    if not final:
        out_specs.append(pl.BlockSpec((None, lq, HEAD_DIM), col))
        out_shape.append(jax.ShapeDtypeStruct((b, sl, dil * aw), F32))
    res = pl.pallas_call(
        functools.partial(_attn_kernel, seq_len=sl, has_prev=prev is not None, final=final),
        grid=(b, dil, N_HEADS, sl // lq),
        in_specs=in_specs, out_specs=out_specs, out_shape=out_shape,
        compiler_params=_cparams("arbitrary", "arbitrary", "arbitrary", "arbitrary"),
    )(*args)
    return [r.reshape(b, s, aw) for r in res]


def _attention(q, k, v):
    prev = None
    for idx, dil in enumerate(DILATIONS):
        final = idx == len(DILATIONS) - 1
        prev = _attn_branch(q, k, v, dil, prev, final)
    return prev[0]


def _lru_kernel(xf_ref, xfb_ref, xfa_ref, xr_ref, xrb_ref, xra_ref, cw_ref, cb_ref, wr_ref, wi_ref,
                br_ref, bi_ref, lam_ref, hf_ref, hr_ref, a_scr, b_scr, carry_scr):
    si = pl.program_id(1)
    ns = pl.num_programs(1)
    ts, w = xf_ref.shape
    ngroups = ts // SUBLANES

    @pl.when(si == 0)
    def _():
        carry_scr[...] = jnp.zeros_like(carry_scr)

    row = lax.broadcasted_iota(jnp.int32, (ts, LRU_BLOCK), 0)
    sub = row % SUBLANES
    cw = cw_ref[...]
    cb = cb_ref[...]

    def prepare(dirn, main_ref, before_ref, after_ref, tile):
        main = main_ref[...]
        before = jnp.where(tile == 0, 0.0, before_ref[...])
        after = jnp.where(tile == ns - 1, 0.0, after_ref[...])
        rowf = lax.broadcasted_iota(jnp.int32, (ts, w), 0)
        xm1 = jnp.where(rowf == 0, before[7:8], pltpu.roll(main, 1, axis=0))
        xm2 = jnp.where(rowf == 0, before[6:7],
                        jnp.where(rowf == 1, before[7:8], pltpu.roll(main, 2, axis=0)))
        xp1 = jnp.where(rowf == ts - 1, after[0:1], pltpu.roll(main, ts - 1, axis=0))
        xc = cb + xm2 * cw[0:1] + xm1 * cw[1:2] + main * cw[2:3] + xp1 * cw[3:4]
        xcb = xc.astype(BF16)
        lam = lam_ref[dirn:dirn + 1, :]
        sp = jnp.maximum(-lam, 0.0) + jnp.log1p(jnp.exp(-jnp.abs(lam)))
        for n in range(w // LRU_BLOCK):
            cols = slice(n * LRU_BLOCK, (n + 1) * LRU_BLOCK)
            blk = xcb[:, cols]
            r = jax.nn.sigmoid(jnp.dot(blk, wr_ref[dirn, n], preferred_element_type=F32)
                               + br_ref[dirn:dirn + 1, cols])
            gi = jax.nn.sigmoid(jnp.dot(blk, wi_ref[dirn, n], preferred_element_type=F32)
                                + bi_ref[dirn:dirn + 1, cols])
            a = jnp.exp(-LRU_C * r * sp[:, cols])
            bt = jnp.sqrt(jnp.maximum(1.0 - a * a, 0.0)) * (gi * xc[:, cols])
            for shift in (1, 2, 4):
                if dirn == 0:
                    a_sh = pltpu.roll(a, shift, axis=0)
                    b_sh = pltpu.roll(bt, shift, axis=0)
                    ok = sub >= shift
                else:
                    a_sh = pltpu.roll(a, ts - shift, axis=0)
                    b_sh = pltpu.roll(bt, ts - shift, axis=0)
                    ok = sub < SUBLANES - shift
                bt = jnp.where(ok, a * b_sh + bt, bt)
                a = jnp.where(ok, a * a_sh, a)
            a_scr[dirn, :, cols] = a
            b_scr[dirn, :, cols] = bt

    prepare(0, xf_ref, xfb_ref, xfa_ref, si)
    prepare(1, xr_ref, xrb_ref, xra_ref, ns - 1 - si)

    def body(g, carry):
        cf, cr = carry
        rf = pl.ds(pl.multiple_of(g * SUBLANES, SUBLANES), SUBLANES)
        hf = b_scr[0, rf, :] + a_scr[0, rf, :] * cf
        hf_ref[rf, :] = hf
        cf = jnp.broadcast_to(hf[SUBLANES - 1:SUBLANES, :], (SUBLANES, w))
        rr = pl.ds(pl.multiple_of((ngroups - 1 - g) * SUBLANES, SUBLANES), SUBLANES)
        hr = b_scr[1, rr, :] + a_scr[1, rr, :] * cr
        hr_ref[rr, :] = hr
        cr = jnp.broadcast_to(hr[0:1, :], (SUBLANES, w))
        return cf, cr

    cf, cr = lax.fori_loop(0, ngroups, body, (carry_scr[0], carry_scr[1]))
    carry_scr[0] = cf
    carry_scr[1] = cr


def _lru_call(xb, conv_w, conv_b, wr_bf, wi_bf, b_r, b_i, lam):
    b, s, w = xb.shape
    ts = 256
    ns = s // ts
    hb = ts // SUBLANES
    nh = s // SUBLANES
    fwd = lambda i, j: (i, j, 0)
    rev = lambda i, j: (i, ns - 1 - j, 0)
    fwd_before = lambda i, j: (i, jnp.maximum(j * hb - 1, 0), 0)
    fwd_after = lambda i, j: (i, jnp.minimum((j + 1) * hb, nh - 1), 0)
    rev_before = lambda i, j: (i, jnp.maximum((ns - 1 - j) * hb - 1, 0), 0)
    rev_after = lambda i, j: (i, jnp.minimum((ns - j) * hb, nh - 1), 0)
    main = lambda im: pl.BlockSpec((None, ts, w), im)
    halo = lambda im: pl.BlockSpec((None, SUBLANES, w), im)
    return pl.pallas_call(
        _lru_kernel,
        grid=(b, ns),
        in_specs=[main(fwd), halo(fwd_before), halo(fwd_after),
                  main(rev), halo(rev_before), halo(rev_after),
                  _const_spec(conv_w.shape), _const_spec(conv_b.shape),
                  _const_spec(wr_bf.shape), _const_spec(wi_bf.shape),
                  _const_spec(b_r.shape), _const_spec(b_i.shape), _const_spec(lam.shape)],
        out_specs=[main(fwd), main(rev)],
        out_shape=[jax.ShapeDtypeStruct((b, s, w), F32)] * 2,
        scratch_shapes=[pltpu.VMEM((2, ts, w), F32), pltpu.VMEM((2, ts, w), F32),
                        pltpu.VMEM((2, SUBLANES, w), F32)],
        compiler_params=_cparams("arbitrary", "arbitrary"),
    )(xb, xb, xb, xb, xb, xb, conv_w, conv_b, wr_bf, wi_bf, b_r, b_i, lam)


def _outproj_kernel(att_ref, hf_ref, hr_ref, gate_ref, x_ref, mod_ref, ga_ref, gl_ref, w_ref, g2_ref,
                    x1_ref, h2t_ref):
    modv = mod_ref[...]
    g1, sh2, sc2 = modv[2:3], modv[3:4], modv[4:5]
    an = _rms(att_ref[...].astype(F32), ga_ref[...])
    lru = (hf_ref[...] + hr_ref[...]) * _gelu(gate_ref[...])
    ln = _rms(lru, gl_ref[...])
    aw = an.shape[-1]
    mix = (jnp.dot(an.astype(BF16), w_ref[0:aw, :], preferred_element_type=F32)
           + jnp.dot(ln.astype(BF16), w_ref[aw:, :], preferred_element_type=F32))
    x1 = x_ref[...] + g1 * mix
    x1_ref[...] = x1
    h2 = _rms(x1, g2_ref[...]) * (1.0 + sc2) + sh2
    h2t_ref[...] = h2.T.astype(BF16)


def _outproj_call(att, hf, hr, gate, x, mod3, ga, gl, w_out_bf, g2):
    b, s, d = x.shape
    aw, lw = att.shape[-1], hf.shape[-1]
    tm = 256
    nt = s // tm
    tok = lambda i, j: (i, j, 0)
    return pl.pallas_call(
        _outproj_kernel,
        grid=(b, nt),
        in_specs=[pl.BlockSpec((None, tm, aw), tok), pl.BlockSpec((None, tm, lw), tok),
                  pl.BlockSpec((None, tm, lw), tok), pl.BlockSpec((None, tm, lw), tok),
                  pl.BlockSpec((None, tm, d), tok),
                  pl.BlockSpec((None, N_MOD, d), lambda i, j: (i, 0, 0)),
                  _const_spec((1, aw)), _const_spec((1, lw)), _const_spec(w_out_bf.shape),
                  _const_spec((1, d))],
        out_specs=[pl.BlockSpec((None, tm, d), tok),
                   pl.BlockSpec((d, tm), lambda i, j: (0, i * nt + j))],
        out_shape=[jax.ShapeDtypeStruct((b, s, d), F32), jax.ShapeDtypeStruct((d, b * s), BF16)],
        compiler_params=_cparams("arbitrary", "arbitrary"),
    )(att, hf, hr, gate, x, mod3, ga, gl, w_out_bf, g2)


def _top16_rows(s):
    t = s.shape[1]
    rowi = lax.broadcasted_iota(jnp.int32, (PEER_TOPK, t), 0)
    out = jnp.full((PEER_TOPK, t), NEG_INF, F32)
    cur = s
    for kk in range(PEER_TOPK):
        m = jnp.max(cur, axis=0, keepdims=True)
        out = jnp.where(rowi == kk, m, out)
        cur = jnp.where(cur == m, NEG_INF, cur)
    return out


def _peerq_kernel(h2t_ref, wq_ref, sk_ref, s1_ref, s2_ref, a_ref, b_ref, tau_ref):
    tq = h2t_ref.shape[1]
    qt = jnp.dot(wq_ref[...], h2t_ref[...], preferred_element_type=F32).astype(BF16)
    rowi = lax.broadcasted_iota(jnp.int32, (PEER_TOPK, tq), 0)
    for h in range(PEER_HEADS):
        s1 = jnp.dot(sk_ref[2 * h], qt[(2 * h) * PEER_KEYS:(2 * h + 1) * PEER_KEYS, :],
                     preferred_element_type=F32)
        s2 = jnp.dot(sk_ref[2 * h + 1], qt[(2 * h + 1) * PEER_KEYS:(2 * h + 2) * PEER_KEYS, :],
                     preferred_element_type=F32)
        a16 = _top16_rows(s1)
        b16 = _top16_rows(s2)
        pieces = []
        for i in range(4):
            cnt = PEER_TOPK // (i + 1)
            pieces.append(jnp.where(rowi < cnt, a16[i:i + 1, :] + b16, NEG_INF))
        for j in range(3):
            cnt = PEER_TOPK // (j + 1)
            pieces.append(jnp.where((rowi >= 4) & (rowi < cnt), a16 + b16[j:j + 1, :], NEG_INF))
        top = a16[0:1, :] + b16[0:1, :]
        z = jnp.zeros((1, tq), F32)
        tau = top
        for kk in range(PEER_TOPK):
            m = pieces[0]
            for pc in pieces[1:]:
                m = jnp.maximum(m, pc)
            m = jnp.max(m, axis=0, keepdims=True)
            z = z + jnp.exp(m - top)
            tau = m
            pieces = [jnp.where(pc == m, NEG_INF, pc) for pc in pieces]
        s1_ref[h] = s1
        s2_ref[h] = s2
        a_ref[h] = jnp.exp(s1 - a16[0:1, :]) * (1.0 / z)
        b_ref[h] = jnp.exp(s2 - b16[0:1, :])
        tau_ref[h:h + 1, :] = tau


def _peerq_call(h2t, wq_t_bf, sk_bf):
    d, t = h2t.shape
    tq = 256
    big = lambda: pl.BlockSpec((PEER_HEADS, PEER_KEYS, tq), lambda i: (0, 0, i))
    return pl.pallas_call(
        _peerq_kernel,
        grid=(t // tq,),
        in_specs=[pl.BlockSpec((d, tq), lambda i: (0, i)),
                  _const_spec(wq_t_bf.shape), _const_spec(sk_bf.shape)],
        out_specs=[big(), big(), big(), big(), pl.BlockSpec((PEER_HEADS, tq), lambda i: (0, i))],
        out_shape=[jax.ShapeDtypeStruct((PEER_HEADS, PEER_KEYS, t), F32)] * 4
        + [jax.ShapeDtypeStruct((PEER_HEADS, t), F32)],
        compiler_params=_cparams("arbitrary"),
    )(h2t, wq_t_bf, sk_bf)


PEER_STRIP = 256


def _peer_kernel(h2t_ref, u_ref, vt_ref, s1_ref, a_ref, s2_ref, b_ref, tau_ref, acc_ref, s_scr, c_scr):
    c = pl.program_id(1)
    ce, tm = s_scr.shape

    @pl.when(c == 0)
    def _():
        acc_ref[...] = jnp.zeros_like(acc_ref)

    s_scr[...] = jnp.dot(u_ref[...], h2t_ref[...], preferred_element_type=F32)

    def sub_block(il, carry):
        rows = pl.ds(pl.multiple_of(il * PEER_KEYS, PEER_KEYS), PEER_KEYS)
        for si in range(tm // PEER_STRIP):
            cols = slice(si * PEER_STRIP, (si + 1) * PEER_STRIP)
            w = jnp.zeros((PEER_KEYS, PEER_STRIP), F32)
            for h in range(PEER_HEADS):
                s1row = s1_ref[h, pl.ds(il, 1), cols]
                arow = a_ref[h, pl.ds(il, 1), cols]
                taurow = tau_ref[h:h + 1, cols]
                sel = (s2_ref[h, :, cols] + s1row) >= taurow
                w = w + jnp.where(sel, b_ref[h, :, cols], 0.0) * arow
            cmat = w * _gelu(s_scr[rows, cols])
            c_scr[rows, cols] = cmat.astype(BF16)
        return carry

    lax.fori_loop(0, ce // PEER_KEYS, sub_block, 0)

    acc_ref[...] += jnp.dot(vt_ref[...], c_scr[...], preferred_element_type=F32)


def _peer_call(h2t, u_bf, vt_bf, s1t, at, s2t, bt, taut):
    d, t = h2t.shape
    ne = u_bf.shape[0]
    tm = 512
    ce = 1024
    nsub = ce // PEER_KEYS
    sub = lambda: pl.BlockSpec((PEER_HEADS, nsub, tm), lambda i, c: (0, c, i))
    allk = lambda: pl.BlockSpec((PEER_HEADS, PEER_KEYS, tm), lambda i, c: (0, 0, i))
    return pl.pallas_call(
        _peer_kernel,
        grid=(t // tm, ne // ce),
        in_specs=[pl.BlockSpec((d, tm), lambda i, c: (0, i)),
                  pl.BlockSpec((ce, d), lambda i, c: (c, 0)),
                  pl.BlockSpec((d, ce), lambda i, c: (0, c)),
                  sub(), sub(), allk(), allk(),
                  pl.BlockSpec((PEER_HEADS, tm), lambda i, c: (0, i))],
        out_specs=pl.BlockSpec((d, tm), lambda i, c: (0, i)),
        out_shape=jax.ShapeDtypeStruct((d, t), F32),
        scratch_shapes=[pltpu.VMEM((ce, tm), F32), pltpu.VMEM((ce, tm), BF16)],
        compiler_params=_cparams("arbitrary", "arbitrary"),
    )(h2t, u_bf, vt_bf, s1t, at, s2t, bt, taut)


def _final_kernel(pt_ref, x1_ref, mod_ref, gf_ref, y_ref):
    g2 = mod_ref[...][5:6]
    x2 = x1_ref[...] + g2 * pt_ref[...].T
    y_ref[...] = _rms(x2, gf_ref[...])


def _final_call(peer_t, x1, mod3, gf):
    b, s, d = x1.shape
    tm = 256
    nt = s // tm
    tok = lambda i, j: (i, j, 0)
    return pl.pallas_call(
        _final_kernel,
        grid=(b, nt),
        in_specs=[pl.BlockSpec((d, tm), lambda i, j: (0, i * nt + j)),
                  pl.BlockSpec((None, tm, d), tok),
                  pl.BlockSpec((None, N_MOD, d), lambda i, j: (i, 0, 0)),
                  _const_spec((1, d))],
        out_specs=pl.BlockSpec((None, tm, d), tok),
        out_shape=jax.ShapeDtypeStruct((b, s, d), F32),
        compiler_params=_cparams("arbitrary", "arbitrary"),
    )(peer_t, x1, mod3, gf)


def _layer(x, c, w_mod, b_mod, norm1_g, w_in, conv_w, conv_b, lru_w_r, lru_b_r, lru_w_i, lru_b_i,
           lru_lambda, attn_out_g, lru_out_g, w_out, norm2_g, peer_w_q, peer_sub_keys, peer_u, peer_v,
           norm_final_g):
    b, s, d = x.shape
    c8 = jnp.pad(c, ((0, (-b) % SUBLANES), (0, 0)))
    mod = _mod_call(c8, w_mod, b_mod[None, :])
    mod3 = mod.reshape(c8.shape[0], N_MOD, d)[:b]
    cosf, sinf = _rope_call(s)
    q, k, v, xb, gate = _inproj_call(x, mod3, norm1_g[None, :], w_in.astype(BF16), cosf, sinf)
    att = _attention(q, k, v)
    hf, hr = _lru_call(xb, conv_w, conv_b[None, :], lru_w_r.astype(BF16), lru_w_i.astype(BF16),
                       lru_b_r, lru_b_i, lru_lambda)
    x1, h2t = _outproj_call(att, hf, hr, gate, x, mod3, attn_out_g[None, :], lru_out_g[None, :],
                            w_out.astype(BF16), norm2_g[None, :])
    sk = peer_sub_keys.reshape(PEER_HEADS * 2, PEER_KEYS, -1).astype(BF16)
    s1t, s2t, at, bt, taut = _peerq_call(h2t, peer_w_q.T.astype(BF16), sk)
    peer_t = _peer_call(h2t, peer_u.astype(BF16), peer_v.T.astype(BF16), s1t, at, s2t, bt, taut)
    return _final_call(peer_t, x1, mod3, norm_final_g[None, :])


def kernel(x_prompt, x_sample, c_prompt, c_sample, w_mod, b_mod, norm1_g, w_in, conv_w, conv_b, lru_w_r,
           lru_b_r, lru_w_i, lru_b_i, lru_lambda, attn_out_g, lru_out_g, w_out, norm2_g, peer_w_q,
           peer_sub_keys, peer_u, peer_v, norm_final_g):
    assert w_mod.shape[0] == 1, "single-layer stack expected"
    assert x_prompt.shape[1:] == x_sample.shape[1:]
    nb = x_prompt.shape[0]
    x = jnp.concatenate([x_prompt, x_sample], axis=0)
    c = jnp.concatenate([c_prompt, c_sample], axis=0)
    y = _layer(x, c, w_mod[0], b_mod[0], norm1_g[0], w_in[0], conv_w[0], conv_b[0], lru_w_r[0], lru_b_r[0],
               lru_w_i[0], lru_b_i[0], lru_lambda[0], attn_out_g[0], lru_out_g[0], w_out[0], norm2_g[0],
               peer_w_q[0], peer_sub_keys[0], peer_u[0], peer_v[0], norm_final_g)
    return (y[:nb], y[nb:])
```

```python
import functools
import math

import jax
import jax.numpy as jnp
from jax import lax
from jax.experimental import pallas as pl
from jax.experimental.pallas import tpu as pltpu

F32 = jnp.float32
BF16 = jnp.bfloat16

HEAD_DIM = 128
N_HEADS = 8
ATTN_WIDTH = N_HEADS * HEAD_DIM
LRU_BLOCK = 128
DILATIONS = (16, 4, 1)
BAND_RADIUS = 64
ROPE_THETA = 10000.0
LRU_C = 8.0
PEER_HEADS = 8
PEER_KEYS = 128
PEER_TOPK = 16
N_MOD = 6
EPS = 1e-6
MASK_VALUE = -1e30
NEG_INF = float("-inf")

V7X_VMEM_BYTES = 64 * 1024 * 1024
VMEM_LIMIT = V7X_VMEM_BYTES - 8 * 1024 * 1024
LANES = 128
SUBLANES = 8


def _cparams(*sem):
    return pltpu.CompilerParams(dimension_semantics=sem, vmem_limit_bytes=VMEM_LIMIT)


def _const_spec(shape):
    nd = len(shape)
    return pl.BlockSpec(shape, lambda *_: (0,) * nd, pipeline_mode=pl.Buffered(1))


def _rms(x, g):
    return x * lax.rsqrt(jnp.mean(x * x, axis=-1, keepdims=True) + EPS) * g


def _gelu(x):
    return 0.5 * x * (1.0 + jnp.tanh(math.sqrt(2.0 / math.pi) * (x + 0.044715 * (x * x * x))))


def _mod_kernel(c_ref, w_ref, b_ref, o_ref):
    c = c_ref[...]
    sc = c * jax.nn.sigmoid(c)
    o_ref[...] = jnp.dot(sc, w_ref[...], precision=lax.Precision.HIGHEST,
                         preferred_element_type=F32) + b_ref[...]


def _mod_call(c8, w_mod, b_mod):
    rows, d = c8.shape
    n = w_mod.shape[1]
    tn = 1024
    return pl.pallas_call(
        _mod_kernel,
        grid=(n // tn,),
        in_specs=[pl.BlockSpec((rows, d), lambda j: (0, 0)),
                  pl.BlockSpec((d, tn), lambda j: (0, j)),
                  pl.BlockSpec((1, tn), lambda j: (0, j))],
        out_specs=pl.BlockSpec((rows, tn), lambda j: (0, j)),
        out_shape=jax.ShapeDtypeStruct((rows, n), F32),
        compiler_params=_cparams("arbitrary"),
    )(c8, w_mod, b_mod)


def _rope_kernel(inv_ref, cos_ref, sin_ref):
    ts = cos_ref.shape[0]
    pos = (pl.program_id(0) * ts + lax.broadcasted_iota(jnp.int32, (ts, HEAD_DIM), 0)).astype(F32)
    lane = lax.broadcasted_iota(jnp.int32, (ts, HEAD_DIM), 1)
    ang = pos * inv_ref[...]
    cos_ref[...] = jnp.cos(ang)
    sn = jnp.sin(ang)
    sin_ref[...] = jnp.where(lane < HEAD_DIM // 2, -sn, sn)


def _rope_call(seq):
    half = HEAD_DIM // 2
    inv = ROPE_THETA ** (-jnp.arange(half, dtype=F32) / half)
    inv2 = jnp.concatenate([inv, inv])[None, :]
    ts = min(seq, 1024)
    return pl.pallas_call(
        _rope_kernel,
        grid=(seq // ts,),
        in_specs=[pl.BlockSpec((1, HEAD_DIM), lambda i: (0, 0))],
        out_specs=[pl.BlockSpec((ts, HEAD_DIM), lambda i: (i, 0))] * 2,
        out_shape=[jax.ShapeDtypeStruct((seq, HEAD_DIM), F32)] * 2,
        compiler_params=_cparams("arbitrary"),
    )(inv2)


def _inproj_kernel(x_ref, mod_ref, g_ref, w_ref, cos_ref, sin_ref,
                   q_ref, k_ref, v_ref, xb_ref, gate_ref):
    x = x_ref[...]
    modv = mod_ref[...]
    sh1, sc1 = modv[0:1], modv[1:2]
    h = _rms(x, g_ref[...]) * (1.0 + sc1) + sh1
    hb = h.astype(BF16)
    cosv = cos_ref[...]
    sinv = sin_ref[...]
    aw = ATTN_WIDTH

    def rope_store(z, out_ref, scale):
        for hd in range(N_HEADS):
            zs = z[:, hd * HEAD_DIM:(hd + 1) * HEAD_DIM]
            rot = pltpu.roll(zs, HEAD_DIM // 2, axis=1)
            r = zs * cosv + rot * sinv
            if scale != 1.0:
                r = r * scale
            out_ref[:, hd * HEAD_DIM:(hd + 1) * HEAD_DIM] = r.astype(out_ref.dtype)

    zq = jnp.dot(hb, w_ref[:, 0:aw], preferred_element_type=F32)
    rope_store(zq, q_ref, HEAD_DIM ** -0.5)
    zk = jnp.dot(hb, w_ref[:, aw:2 * aw], preferred_element_type=F32)
    rope_store(zk, k_ref, 1.0)
    v_ref[...] = jnp.dot(hb, w_ref[:, 2 * aw:3 * aw], preferred_element_type=F32).astype(v_ref.dtype)
    lw = xb_ref.shape[-1]
    xb_ref[...] = jnp.dot(hb, w_ref[:, 3 * aw:3 * aw + lw], preferred_element_type=F32)
    gate_ref[...] = jnp.dot(hb, w_ref[:, 3 * aw + lw:3 * aw + 2 * lw], preferred_element_type=F32)


def _inproj_call(x, mod3, g1, w_in_bf, cosf, sinf):
    b, s, d = x.shape
    lw = (w_in_bf.shape[1] - 3 * ATTN_WIDTH) // 2
    tm = 256
    tok = lambda i, j: (i, j, 0)
    return pl.pallas_call(
        _inproj_kernel,
        grid=(b, s // tm),
        in_specs=[pl.BlockSpec((None, tm, d), tok),
                  pl.BlockSpec((None, N_MOD, d), lambda i, j: (i, 0, 0)),
                  _const_spec((1, d)),
                  _const_spec(w_in_bf.shape),
                  pl.BlockSpec((tm, HEAD_DIM), lambda i, j: (j, 0)),
                  pl.BlockSpec((tm, HEAD_DIM), lambda i, j: (j, 0))],
        out_specs=[pl.BlockSpec((None, tm, ATTN_WIDTH), tok)] * 3 + [pl.BlockSpec((None, tm, lw), tok)] * 2,
        out_shape=[jax.ShapeDtypeStruct((b, s, ATTN_WIDTH), BF16)] * 3
        + [jax.ShapeDtypeStruct((b, s, lw), F32)] * 2,
        compiler_params=_cparams("arbitrary", "arbitrary"),
    )(x, mod3, g1, w_in_bf, cosf, sinf)


ATT_QB = 128
ATT_KW = ATT_QB + 2 * BAND_RADIUS


def _attn_kernel(*refs, seq_len, has_prev, final):
    if has_prev:
        q_ref, k_ref, v_ref, po_ref, pl_ref = refs[:5]
        outs = refs[5:]
    else:
        q_ref, k_ref, v_ref = refs[:3]
        outs = refs[3:]
    o_ref = outs[0]
    lq = q_ref.shape[0]
    base = pl.program_id(3) * lq
    rel0 = (lax.broadcasted_iota(jnp.int32, (ATT_QB, ATT_KW), 1)
            - lax.broadcasted_iota(jnp.int32, (ATT_QB, ATT_KW), 0))
    for j in range(lq // ATT_QB):
        rows = slice(j * ATT_QB, (j + 1) * ATT_QB)
        qs = base + j * ATT_QB
        ks = jnp.clip(qs - BAND_RADIUS, 0, seq_len - ATT_KW)
        ks = pl.multiple_of(ks, BAND_RADIUS)
        q = q_ref[rows, :]
        kw = k_ref[pl.ds(ks, ATT_KW), :]
        vw = v_ref[pl.ds(ks, ATT_KW), :]
        s = lax.dot_general(q, kw, (((1,), (1,)), ((), ())), preferred_element_type=F32)
        rel = rel0 + (ks - qs)
        s = jnp.where(jnp.abs(rel) <= BAND_RADIUS, s, MASK_VALUE)
        m = jnp.max(s, axis=-1, keepdims=True)
        p = jnp.exp(s - m)
        den = jnp.sum(p, axis=-1, keepdims=True)
        o = jnp.dot(p.astype(BF16), vw, preferred_element_type=F32) * (1.0 / den)
        lse = jnp.broadcast_to(m + jnp.log(den), (ATT_QB, HEAD_DIM))
        if has_prev:
            po = po_ref[rows, :].astype(F32)
            pls = pl_ref[rows, :]
            mx = jnp.maximum(pls, lse)
            w1 = jnp.exp(pls - mx)
            w2 = jnp.exp(lse - mx)
            tot = w1 + w2
            o = (po * w1 + o * w2) * (1.0 / tot)
            lse = mx + jnp.log(tot)
        o_ref[rows, :] = o.astype(o_ref.dtype)
        if not final:
            outs[1][rows, :] = lse


def _attn_branch(q, k, v, dil, prev, final):
    b, s, aw = q.shape
    sl = s // dil
    lq = min(sl, 512)
    assert sl >= ATT_KW and sl % lq == 0
    qv, kv, vv = (t.reshape(b, sl, dil * aw) for t in (q, k, v))
    col = lambda i, r, h, j: (i, j, r * N_HEADS + h)
    full = lambda i, r, h, j: (i, 0, r * N_HEADS + h)
    in_specs = [pl.BlockSpec((None, lq, HEAD_DIM), col),
                pl.BlockSpec((None, sl, HEAD_DIM), full),
                pl.BlockSpec((None, sl, HEAD_DIM), full)]
    args = [qv, kv, vv]
    if prev is not None:
        in_specs += [pl.BlockSpec((None, lq, HEAD_DIM), col)] * 2
        args += [prev[0].reshape(b, sl, dil * aw), prev[1].reshape(b, sl, dil * aw)]
    out_specs = [pl.BlockSpec((None, lq, HEAD_DIM), col)]
    out_shape = [jax.ShapeDtypeStruct((b, sl, dil * aw), BF16)]
    if not final:
        out_specs.append(pl.BlockSpec((None, lq, HEAD_DIM), col))
        out_shape.append(jax.ShapeDtypeStruct((b, sl, dil * aw), F32))
    res = pl.pallas_call(
        functools.partial(_attn_kernel, seq_len=sl, has_prev=prev is not None, final=final),
        grid=(b, dil, N_HEADS, sl // lq),
        in_specs=in_specs, out_specs=out_specs, out_shape=out_shape,
        compiler_params=_cparams("arbitrary", "arbitrary", "arbitrary", "arbitrary"),
    )(*args)
    return [r.reshape(b, s, aw) for r in res]


def _attention(q, k, v):
    prev = None
    for idx, dil in enumerate(DILATIONS):
        final = idx == len(DILATIONS) - 1
        prev = _attn_branch(q, k, v, dil, prev, final)
    return prev[0]


def _lru_kernel(xf_ref, xfb_ref, xfa_ref, xr_ref, xrb_ref, xra_ref, cw_ref, cb_ref, wr_ref, wi_ref,
                br_ref, bi_ref, lam_ref, hf_ref, hr_ref, a_scr, b_scr, carry_scr):
    si = pl.program_id(1)
    ns = pl.num_programs(1)
    ts, w = xf_ref.shape
    ngroups = ts // SUBLANES

    @pl.when(si == 0)
    def _():
        carry_scr[...] = jnp.zeros_like(carry_scr)

    row = lax.broadcasted_iota(jnp.int32, (ts, LRU_BLOCK), 0)
    sub = row % SUBLANES
    cw = cw_ref[...]
    cb = cb_ref[...]

    def prepare(dirn, main_ref, before_ref, after_ref, tile):
        main = main_ref[...]
        before = jnp.where(tile == 0, 0.0, before_ref[...])
        after = jnp.where(tile == ns - 1, 0.0, after_ref[...])
        rowf = lax.broadcasted_iota(jnp.int32, (ts, w), 0)
        xm1 = jnp.where(rowf == 0, before[7:8], pltpu.roll(main, 1, axis=0))
        xm2 = jnp.where(rowf == 0, before[6:7],
                        jnp.where(rowf == 1, before[7:8], pltpu.roll(main, 2, axis=0)))
        xp1 = jnp.where(rowf == ts - 1, after[0:1], pltpu.roll(main, ts - 1, axis=0))
        xc = cb + xm2 * cw[0:1] + xm1 * cw[1:2] + main * cw[2:3] + xp1 * cw[3:4]
        xcb = xc.astype(BF16)
        lam = lam_ref[dirn:dirn + 1, :]
        sp = jnp.maximum(-lam, 0.0) + jnp.log1p(jnp.exp(-jnp.abs(lam)))
        for n in range(w // LRU_BLOCK):
            cols = slice(n * LRU_BLOCK, (n + 1) * LRU_BLOCK)
            blk = xcb[:, cols]
            r = jax.nn.sigmoid(jnp.dot(blk, wr_ref[dirn, n], preferred_element_type=F32)
                               + br_ref[dirn:dirn + 1, cols])
            gi = jax.nn.sigmoid(jnp.dot(blk, wi_ref[dirn, n], preferred_element_type=F32)
                                + bi_ref[dirn:dirn + 1, cols])
            a = jnp.exp(-LRU_C * r * sp[:, cols])
            bt = jnp.sqrt(jnp.maximum(1.0 - a * a, 0.0)) * (gi * xc[:, cols])
            for shift in (1, 2, 4):
                if dirn == 0:
                    a_sh = pltpu.roll(a, shift, axis=0)
                    b_sh = pltpu.roll(bt, shift, axis=0)
                    ok = sub >= shift
                else:
                    a_sh = pltpu.roll(a, ts - shift, axis=0)
                    b_sh = pltpu.roll(bt, ts - shift, axis=0)
                    ok = sub < SUBLANES - shift
                bt = jnp.where(ok, a * b_sh + bt, bt)
                a = jnp.where(ok, a * a_sh, a)
            a_scr[dirn, :, cols] = a
            b_scr[dirn, :, cols] = bt

    prepare(0, xf_ref, xfb_ref, xfa_ref, si)
    prepare(1, xr_ref, xrb_ref, xra_ref, ns - 1 - si)

    def body(g, carry):
        cf, cr = carry
        rf = pl.ds(pl.multiple_of(g * SUBLANES, SUBLANES), SUBLANES)
        hf = b_scr[0, rf, :] + a_scr[0, rf, :] * cf
        hf_ref[rf, :] = hf
        cf = jnp.broadcast_to(hf[SUBLANES - 1:SUBLANES, :], (SUBLANES, w))
        rr = pl.ds(pl.multiple_of((ngroups - 1 - g) * SUBLANES, SUBLANES), SUBLANES)
        hr = b_scr[1, rr, :] + a_scr[1, rr, :] * cr
        hr_ref[rr, :] = hr
        cr = jnp.broadcast_to(hr[0:1, :], (SUBLANES, w))
        return cf, cr

    cf, cr = lax.fori_loop(0, ngroups, body, (carry_scr[0], carry_scr[1]))
    carry_scr[0] = cf
    carry_scr[1] = cr


def _lru_call(xb, conv_w, conv_b, wr_bf, wi_bf, b_r, b_i, lam):
    b, s, w = xb.shape
    ts = 256
    ns = s // ts
    hb = ts // SUBLANES
    nh = s // SUBLANES
    fwd = lambda i, j: (i, j, 0)
    rev = lambda i, j: (i, ns - 1 - j, 0)
    fwd_before = lambda i, j: (i, jnp.maximum(j * hb - 1, 0), 0)
    fwd_after = lambda i, j: (i, jnp.minimum((j + 1) * hb, nh - 1), 0)
    rev_before = lambda i, j: (i, jnp.maximum((ns - 1 - j) * hb - 1, 0), 0)
    rev_after = lambda i, j: (i, jnp.minimum((ns - j) * hb, nh - 1), 0)
    main = lambda im: pl.BlockSpec((None, ts, w), im)
    halo = lambda im: pl.BlockSpec((None, SUBLANES, w), im)
    return pl.pallas_call(
        _lru_kernel,
        grid=(b, ns),
        in_specs=[main(fwd), halo(fwd_before), halo(fwd_after),
                  main(rev), halo(rev_before), halo(rev_after),
                  _const_spec(conv_w.shape), _const_spec(conv_b.shape),
                  _const_spec(wr_bf.shape), _const_spec(wi_bf.shape),
                  _const_spec(b_r.shape), _const_spec(b_i.shape), _const_spec(lam.shape)],
        out_specs=[main(fwd), main(rev)],
        out_shape=[jax.ShapeDtypeStruct((b, s, w), F32)] * 2,
        scratch_shapes=[pltpu.VMEM((2, ts, w), F32), pltpu.VMEM((2, ts, w), F32),
                        pltpu.VMEM((2, SUBLANES, w), F32)],
        compiler_params=_cparams("arbitrary", "arbitrary"),
    )(xb, xb, xb, xb, xb, xb, conv_w, conv_b, wr_bf, wi_bf, b_r, b_i, lam)


def _outproj_kernel(att_ref, hf_ref, hr_ref, gate_ref, x_ref, mod_ref, ga_ref, gl_ref, w_ref, g2_ref,
                    x1_ref, h2t_ref):
    modv = mod_ref[...]
    g1, sh2, sc2 = modv[2:3], modv[3:4], modv[4:5]
    an = _rms(att_ref[...].astype(F32), ga_ref[...])
    lru = (hf_ref[...] + hr_ref[...]) * _gelu(gate_ref[...])
    ln = _rms(lru, gl_ref[...])
    aw = an.shape[-1]
    mix = (jnp.dot(an.astype(BF16), w_ref[0:aw, :], preferred_element_type=F32)
           + jnp.dot(ln.astype(BF16), w_ref[aw:, :], preferred_element_type=F32))
    x1 = x_ref[...] + g1 * mix
    x1_ref[...] = x1
    h2 = _rms(x1, g2_ref[...]) * (1.0 + sc2) + sh2
    h2t_ref[...] = h2.T.astype(BF16)


def _outproj_call(att, hf, hr, gate, x, mod3, ga, gl, w_out_bf, g2):
    b, s, d = x.shape
    aw, lw = att.shape[-1], hf.shape[-1]
    tm = 256
    nt = s // tm
    tok = lambda i, j: (i, j, 0)
    return pl.pallas_call(
        _outproj_kernel,
        grid=(b, nt),
        in_specs=[pl.BlockSpec((None, tm, aw), tok), pl.BlockSpec((None, tm, lw), tok),
                  pl.BlockSpec((None, tm, lw), tok), pl.BlockSpec((None, tm, lw), tok),
                  pl.BlockSpec((None, tm, d), tok),
                  pl.BlockSpec((None, N_MOD, d), lambda i, j: (i, 0, 0)),
                  _const_spec((1, aw)), _const_spec((1, lw)), _const_spec(w_out_bf.shape),
                  _const_spec((1, d))],
        out_specs=[pl.BlockSpec((None, tm, d), tok),
                   pl.BlockSpec((d, tm), lambda i, j: (0, i * nt + j))],
        out_shape=[jax.ShapeDtypeStruct((b, s, d), F32), jax.ShapeDtypeStruct((d, b * s), BF16)],
        compiler_params=_cparams("arbitrary", "arbitrary"),
    )(att, hf, hr, gate, x, mod3, ga, gl, w_out_bf, g2)


def _top16_rows(s):
    t = s.shape[1]
    rowi = lax.broadcasted_iota(jnp.int32, (PEER_TOPK, t), 0)
    out = jnp.full((PEER_TOPK, t), NEG_INF, F32)
    rank = jnp.full(s.shape, float(PEER_TOPK), F32)
    cur = s
    for kk in range(PEER_TOPK):
        m = jnp.max(cur, axis=0, keepdims=True)
        out = jnp.where(rowi == kk, m, out)
        hit = cur == m
        rank = jnp.where(hit, float(kk), rank)
        cur = jnp.where(hit, NEG_INF, cur)
    return out, rank


def _product_key_select(s1, s2):
    t = s1.shape[1]
    rowi = lax.broadcasted_iota(jnp.int32, (PEER_TOPK, t), 0)
    a16, rank1 = _top16_rows(s1)
    b16, rank2 = _top16_rows(s2)
    pieces = []
    for i in range(4):
        cnt = PEER_TOPK // (i + 1)
        pieces.append(jnp.where(rowi < cnt, a16[i:i + 1, :] + b16, NEG_INF))
    for j in range(3):
        cnt = PEER_TOPK // (j + 1)
        pieces.append(jnp.where((rowi >= 4) & (rowi < cnt), a16 + b16[j:j + 1, :], NEG_INF))
    top = a16[0:1, :] + b16[0:1, :]
    z = jnp.zeros((1, t), F32)
    tau = top
    for kk in range(PEER_TOPK):
        m = pieces[0]
        for pc in pieces[1:]:
            m = jnp.maximum(m, pc)
        m = jnp.max(m, axis=0, keepdims=True)
        z = z + jnp.exp(m - top)
        tau = m
        pieces = [jnp.where(pc == m, NEG_INF, pc) for pc in pieces]
    cnt = jnp.zeros((PEER_TOPK, t), F32)
    for l in range(PEER_TOPK):
        cnt = cnt + jnp.where(a16 + b16[l:l + 1, :] >= tau, 1.0, 0.0)
    n1 = jnp.zeros(s1.shape, F32)
    for r in range(PEER_TOPK):
        n1 = jnp.where(rank1 == float(r), cnt[r:r + 1, :], n1)
    return n1, jnp.exp(s1 - a16[0:1, :]) * (1.0 / z), rank2, jnp.exp(s2 - b16[0:1, :])


def _peerq_kernel(h2t_ref, wq_ref, sk_ref, n_ref, a_ref, cj_ref, bj_ref, qt_scr):
    tq = h2t_ref.shape[1]
    qt_scr[...] = jnp.dot(wq_ref[...], h2t_ref[...], preferred_element_type=F32).astype(BF16)

    def head(h, carry):
        r1 = pl.ds(pl.multiple_of(2 * h * PEER_KEYS, PEER_KEYS), PEER_KEYS)
        r2 = pl.ds(pl.multiple_of((2 * h + 1) * PEER_KEYS, PEER_KEYS), PEER_KEYS)
        s1 = jnp.dot(sk_ref[2 * h], qt_scr[r1, :], preferred_element_type=F32)
        s2 = jnp.dot(sk_ref[2 * h + 1], qt_scr[r2, :], preferred_element_type=F32)
        for st in range(tq // LANES):
            cols = slice(st * LANES, (st + 1) * LANES)
            n1, a, rank2, b = _product_key_select(s1[:, cols], s2[:, cols])
            n_ref[h, :, cols] = n1
            a_ref[h, :, cols] = a
            cj_ref[h, :, cols] = rank2.astype(BF16)
            bj_ref[h, :, cols] = b.astype(BF16)
        return carry

    lax.fori_loop(0, PEER_HEADS, head, 0)


def _peerq_call(h2t, wq_t_bf, sk_bf):
    d, t = h2t.shape
    tq = 256
    big = lambda: pl.BlockSpec((PEER_HEADS, PEER_KEYS, tq), lambda i: (0, 0, i))
    shp = lambda dt: jax.ShapeDtypeStruct((PEER_HEADS, PEER_KEYS, t), dt)
    return pl.pallas_call(
        _peerq_kernel,
        grid=(t // tq,),
        in_specs=[pl.BlockSpec((d, tq), lambda i: (0, i)),
                  _const_spec(wq_t_bf.shape), _const_spec(sk_bf.shape)],
        out_specs=[big(), big(), big(), big()],
        out_shape=[shp(F32), shp(F32), shp(BF16), shp(BF16)],
        scratch_shapes=[pltpu.VMEM((wq_t_bf.shape[0], tq), BF16)],
        compiler_params=_cparams("arbitrary"),
    )(h2t, wq_t_bf, sk_bf)


PEER_STRIP = 256


def _peer_kernel(h2t_ref, u_ref, vt_ref, n_ref, a_ref, cj_ref, bj_ref, acc_ref, s_scr, c_even, c_odd):
    c = pl.program_id(1)
    ce, tm = s_scr.shape

    @pl.when(c == 0)
    def _():
        acc_ref[...] = jnp.zeros_like(acc_ref)
        c_odd[...] = jnp.zeros_like(c_odd)

    def step(c_write, c_read):
        s_scr[...] = jnp.dot(u_ref[...], h2t_ref[...], preferred_element_type=F32)
        for il in range(ce // PEER_KEYS):
            rows = slice(il * PEER_KEYS, (il + 1) * PEER_KEYS)
            for si in range(tm // PEER_STRIP):
                cols = slice(si * PEER_STRIP, (si + 1) * PEER_STRIP)
                w = jnp.zeros((PEER_KEYS, PEER_STRIP), BF16)
                for h in range(PEER_HEADS):
                    nrow = n_ref[h, il:il + 1, cols].astype(BF16)
                    arow = a_ref[h, il:il + 1, cols].astype(BF16)
                    sel = cj_ref[h, :, cols] < nrow
                    w = w + jnp.where(sel, bj_ref[h, :, cols], jnp.zeros((), BF16)) * arow
                act = _gelu(s_scr[rows, cols]).astype(BF16)
                c_write[rows, cols] = w * act
        acc_ref[...] += jnp.dot(vt_ref[...], c_read[...], preferred_element_type=F32)

    @pl.when(c % 2 == 0)
    def _():
        step(c_even, c_odd)

    @pl.when(c % 2 == 1)
    def _():
        step(c_odd, c_even)


def _peer_call(h2t, u_bf, vt_bf, nt, at, cjt, bjt):
    d, t = h2t.shape
    ne = u_bf.shape[0]
    tm = 512
    ce = 1024
    nc = ne // ce
    nsub = ce // PEER_KEYS
    sub = lambda: pl.BlockSpec((PEER_HEADS, nsub, tm), lambda i, c: (0, jnp.minimum(c, nc - 1), i))
    allk = lambda: pl.BlockSpec((PEER_HEADS, PEER_KEYS, tm), lambda i, c: (0, 0, i))
    return pl.pallas_call(
        _peer_kernel,
        grid=(t // tm, nc + 1),
        in_specs=[pl.BlockSpec((d, tm), lambda i, c: (0, i)),
                  pl.BlockSpec((ce, d), lambda i, c: (jnp.minimum(c, nc - 1), 0)),
                  pl.BlockSpec((d, ce), lambda i, c: (0, jnp.maximum(c - 1, 0))),
                  sub(), sub(), allk(), allk()],
        out_specs=pl.BlockSpec((d, tm), lambda i, c: (0, i)),
        out_shape=jax.ShapeDtypeStruct((d, t), F32),
        scratch_shapes=[pltpu.VMEM((ce, tm), F32), pltpu.VMEM((ce, tm), BF16), pltpu.VMEM((ce, tm), BF16)],
        compiler_params=_cparams("arbitrary", "arbitrary"),
    )(h2t, u_bf, vt_bf, nt, at, cjt, bjt)


def _final_kernel(pt_ref, x1_ref, mod_ref, gf_ref, y_ref):
    g2 = mod_ref[...][5:6]
    x2 = x1_ref[...] + g2 * pt_ref[...].T
    y_ref[...] = _rms(x2, gf_ref[...])


def _final_call(peer_t, x1, mod3, gf):
    b, s, d = x1.shape
    tm = 256
    nt = s // tm
    tok = lambda i, j: (i, j, 0)
    return pl.pallas_call(
        _final_kernel,
        grid=(b, nt),
        in_specs=[pl.BlockSpec((d, tm), lambda i, j: (0, i * nt + j)),
                  pl.BlockSpec((None, tm, d), tok),
                  pl.BlockSpec((None, N_MOD, d), lambda i, j: (i, 0, 0)),
                  _const_spec((1, d))],
        out_specs=pl.BlockSpec((None, tm, d), tok),
        out_shape=jax.ShapeDtypeStruct((b, s, d), F32),
        compiler_params=_cparams("arbitrary", "arbitrary"),
    )(peer_t, x1, mod3, gf)


def _layer(x, c, w_mod, b_mod, norm1_g, w_in, conv_w, conv_b, lru_w_r, lru_b_r, lru_w_i, lru_b_i,
           lru_lambda, attn_out_g, lru_out_g, w_out, norm2_g, peer_w_q, peer_sub_keys, peer_u, peer_v,
           norm_final_g):
    b, s, d = x.shape
    c8 = jnp.pad(c, ((0, (-b) % SUBLANES), (0, 0)))
    mod = _mod_call(c8, w_mod, b_mod[None, :])
    mod3 = mod.reshape(c8.shape[0], N_MOD, d)[:b]
    cosf, sinf = _rope_call(s)
    q, k, v, xb, gate = _inproj_call(x, mod3, norm1_g[None, :], w_in.astype(BF16), cosf, sinf)
    att = _attention(q, k, v)
    hf, hr = _lru_call(xb, conv_w, conv_b[None, :], lru_w_r.astype(BF16), lru_w_i.astype(BF16),
                       lru_b_r, lru_b_i, lru_lambda)
    x1, h2t = _outproj_call(att, hf, hr, gate, x, mod3, attn_out_g[None, :], lru_out_g[None, :],
                            w_out.astype(BF16), norm2_g[None, :])
    sk = peer_sub_keys.reshape(PEER_HEADS * 2, PEER_KEYS, -1).astype(BF16)
    nt, at, cjt, bjt = _peerq_call(h2t, peer_w_q.T.astype(BF16), sk)
    peer_t = _peer_call(h2t, peer_u.astype(BF16), peer_v.T.astype(BF16), nt, at, cjt, bjt)
    return _final_call(peer_t, x1, mod3, norm_final_g[None, :])


def kernel(x_prompt, x_sample, c_prompt, c_sample, w_mod, b_mod, norm1_g, w_in, conv_w, conv_b, lru_w_r,
           lru_b_r, lru_w_i, lru_b_i, lru_lambda, attn_out_g, lru_out_g, w_out, norm2_g, peer_w_q,
           peer_sub_keys, peer_u, peer_v, norm_final_g):
    assert w_mod.shape[0] == 1, "single-layer stack expected"
    assert x_prompt.shape[1:] == x_sample.shape[1:]
    nb = x_prompt.shape[0]
    x = jnp.concatenate([x_prompt, x_sample], axis=0)
    c = jnp.concatenate([c_prompt, c_sample], axis=0)
    y = _layer(x, c, w_mod[0], b_mod[0], norm1_g[0], w_in[0], conv_w[0], conv_b[0], lru_w_r[0], lru_b_r[0],
               lru_w_i[0], lru_b_i[0], lru_lambda[0], attn_out_g[0], lru_out_g[0], w_out[0], norm2_g[0],
               peer_w_q[0], peer_sub_keys[0], peer_u[0], peer_v[0], norm_final_g)
    return (y[:nb], y[nb:])
```

```python
import functools
import math

import jax
import jax.numpy as jnp
from jax import lax
from jax.experimental import pallas as pl
from jax.experimental.pallas import tpu as pltpu

F32 = jnp.float32
BF16 = jnp.bfloat16

HEAD_DIM = 128
N_HEADS = 8
ATTN_WIDTH = N_HEADS * HEAD_DIM
LRU_BLOCK = 128
DILATIONS = (16, 4, 1)
BAND_RADIUS = 64
ROPE_THETA = 10000.0
LRU_C = 8.0
PEER_HEADS = 8
PEER_KEYS = 128
PEER_TOPK = 16
N_MOD = 6
EPS = 1e-6
MASK_VALUE = -1e30
NEG_INF = float("-inf")

V7X_VMEM_BYTES = 64 * 1024 * 1024
VMEM_LIMIT = V7X_VMEM_BYTES - 8 * 1024 * 1024
LANES = 128
SUBLANES = 8


def _cparams(*sem):
    return pltpu.CompilerParams(dimension_semantics=sem, vmem_limit_bytes=VMEM_LIMIT)


def _const_spec(shape):
    nd = len(shape)
    return pl.BlockSpec(shape, lambda *_: (0,) * nd, pipeline_mode=pl.Buffered(1))


def _rms(x, g):
    return x * lax.rsqrt(jnp.mean(x * x, axis=-1, keepdims=True) + EPS) * g


def _gelu(x):
    return 0.5 * x * (1.0 + jnp.tanh(math.sqrt(2.0 / math.pi) * (x + 0.044715 * (x * x * x))))


def _mod_kernel(c_ref, w_ref, b_ref, o_ref):
    c = c_ref[...]
    sc = c * jax.nn.sigmoid(c)
    o_ref[...] = jnp.dot(sc, w_ref[...], precision=lax.Precision.HIGHEST,
                         preferred_element_type=F32) + b_ref[...]


def _mod_call(c8, w_mod, b_mod):
    rows, d = c8.shape
    n = w_mod.shape[1]
    tn = 1024
    return pl.pallas_call(
        _mod_kernel,
        grid=(n // tn,),
        in_specs=[pl.BlockSpec((rows, d), lambda j: (0, 0)),
                  pl.BlockSpec((d, tn), lambda j: (0, j)),
                  pl.BlockSpec((1, tn), lambda j: (0, j))],
        out_specs=pl.BlockSpec((rows, tn), lambda j: (0, j)),
        out_shape=jax.ShapeDtypeStruct((rows, n), F32),
        compiler_params=_cparams("arbitrary"),
    )(c8, w_mod, b_mod)


def _rope_kernel(inv_ref, cos_ref, sin_ref):
    ts = cos_ref.shape[0]
    pos = (pl.program_id(0) * ts + lax.broadcasted_iota(jnp.int32, (ts, HEAD_DIM), 0)).astype(F32)
    lane = lax.broadcasted_iota(jnp.int32, (ts, HEAD_DIM), 1)
    ang = pos * inv_ref[...]
    cos_ref[...] = jnp.cos(ang)
    sn = jnp.sin(ang)
    sin_ref[...] = jnp.where(lane < HEAD_DIM // 2, -sn, sn)


def _rope_call(seq):
    half = HEAD_DIM // 2
    inv = ROPE_THETA ** (-jnp.arange(half, dtype=F32) / half)
    inv2 = jnp.concatenate([inv, inv])[None, :]
    ts = min(seq, 1024)
    return pl.pallas_call(
        _rope_kernel,
        grid=(seq // ts,),
        in_specs=[pl.BlockSpec((1, HEAD_DIM), lambda i: (0, 0))],
        out_specs=[pl.BlockSpec((ts, HEAD_DIM), lambda i: (i, 0))] * 2,
        out_shape=[jax.ShapeDtypeStruct((seq, HEAD_DIM), F32)] * 2,
        compiler_params=_cparams("arbitrary"),
    )(inv2)


def _inproj_kernel(x_ref, mod_ref, g_ref, w_ref, cos_ref, sin_ref, p4_ref, p16_ref,
                   q1_ref, k1_ref, v1_ref, q4_ref, k4_ref, v4_ref, q16_ref, k16_ref, v16_ref,
                   xb_ref, gate_ref):
    x = x_ref[...]
    modv = mod_ref[...]
    sh1, sc1 = modv[0:1], modv[1:2]
    h = _rms(x, g_ref[...]) * (1.0 + sc1) + sh1
    hb = h.astype(BF16)
    cosv = cos_ref[...]
    sinv = sin_ref[...]
    aw = ATTN_WIDTH

    def rope(z, scale):
        parts = []
        for hd in range(N_HEADS):
            zs = z[:, hd * HEAD_DIM:(hd + 1) * HEAD_DIM]
            rot = pltpu.roll(zs, HEAD_DIM // 2, axis=1)
            r = zs * cosv + rot * sinv
            if scale != 1.0:
                r = r * scale
            parts.append(r.astype(BF16))
        return jnp.concatenate(parts, axis=1)

    def store_layouts(zb, o1_ref, o4_ref, o16_ref):
        o1_ref[...] = zb
        for p_ref, o_ref in ((p4_ref, o4_ref), (p16_ref, o16_ref)):
            perm = jnp.dot(p_ref[...], zb, preferred_element_type=F32).astype(BF16)
            o_ref[...] = perm.reshape(o_ref.shape)

    store_layouts(rope(jnp.dot(hb, w_ref[:, 0:aw], preferred_element_type=F32), HEAD_DIM ** -0.5),
                  q1_ref, q4_ref, q16_ref)
    store_layouts(rope(jnp.dot(hb, w_ref[:, aw:2 * aw], preferred_element_type=F32), 1.0),
                  k1_ref, k4_ref, k16_ref)
    store_layouts(jnp.dot(hb, w_ref[:, 2 * aw:3 * aw], preferred_element_type=F32).astype(BF16),
                  v1_ref, v4_ref, v16_ref)
    lw = xb_ref.shape[-1]
    xb_ref[...] = jnp.dot(hb, w_ref[:, 3 * aw:3 * aw + lw], preferred_element_type=F32)
    gate_ref[...] = jnp.dot(hb, w_ref[:, 3 * aw + lw:3 * aw + 2 * lw], preferred_element_type=F32)


def _residue_perm(tm, dil):
    src = jnp.arange(tm)
    dst = (src % dil) * (tm // dil) + src // dil
    return jnp.zeros((tm, tm), BF16).at[dst, src].set(1.0)


def _inproj_call(x, mod3, g1, w_in_bf, cosf, sinf):
    b, s, d = x.shape
    aw = ATTN_WIDTH
    lw = (w_in_bf.shape[1] - 3 * aw) // 2
    tm = 256
    tok = lambda i, j: (i, j, 0)
    grouped = lambda i, j: (i, 0, j, 0)
    lay1 = pl.BlockSpec((None, tm, aw), tok)
    lay4 = pl.BlockSpec((None, 4, tm // 4, aw), grouped)
    lay16 = pl.BlockSpec((None, 16, tm // 16, aw), grouped)
    shp1 = jax.ShapeDtypeStruct((b, s, aw), BF16)
    shp4 = jax.ShapeDtypeStruct((b, 4, s // 4, aw), BF16)
    shp16 = jax.ShapeDtypeStruct((b, 16, s // 16, aw), BF16)
    return pl.pallas_call(
        _inproj_kernel,
        grid=(b, s // tm),
        in_specs=[pl.BlockSpec((None, tm, d), tok),
                  pl.BlockSpec((None, N_MOD, d), lambda i, j: (i, 0, 0)),
                  _const_spec((1, d)),
                  _const_spec(w_in_bf.shape),
                  pl.BlockSpec((tm, HEAD_DIM), lambda i, j: (j, 0)),
                  pl.BlockSpec((tm, HEAD_DIM), lambda i, j: (j, 0)),
                  _const_spec((tm, tm)), _const_spec((tm, tm))],
        out_specs=[lay1] * 3 + [lay4] * 3 + [lay16] * 3 + [pl.BlockSpec((None, tm, lw), tok)] * 2,
        out_shape=[shp1] * 3 + [shp4] * 3 + [shp16] * 3 + [jax.ShapeDtypeStruct((b, s, lw), F32)] * 2,
        compiler_params=_cparams("arbitrary", "arbitrary"),
    )(x, mod3, g1, w_in_bf, cosf, sinf, _residue_perm(tm, 4), _residue_perm(tm, 16))


ATT_QB = 128
ATT_KW = ATT_QB + 2 * BAND_RADIUS


ATT_NB = 2048


def _attn_kernel(q1_ref, k1_ref, v1_ref, q4_ref, k4_ref, v4_ref, q16_ref, k16_ref, v16_ref, y_ref,
                 o_scr, l_scr, *, seq):
    blk = pl.program_id(2)
    nb = y_ref.shape[0]
    rel0 = (lax.broadcasted_iota(jnp.int32, (ATT_QB, ATT_KW), 1)
            - lax.broadcasted_iota(jnp.int32, (ATT_QB, ATT_KW), 0))
    branches = ((q16_ref, k16_ref, v16_ref, 16), (q4_ref, k4_ref, v4_ref, 4), (q1_ref, k1_ref, v1_ref, 1))
    for bi, (q_ref, k_ref, v_ref, dil) in enumerate(branches):
        per = nb // dil
        nj = per // ATT_QB
        sl = seq // dil

        def sub(idx, carry, bi=bi, q_ref=q_ref, k_ref=k_ref, v_ref=v_ref, dil=dil, per=per, nj=nj, sl=sl):
            r = idx // nj
            j = idx % nj
            l0 = blk * per + j * ATT_QB
            ks = pl.multiple_of(jnp.clip(l0 - BAND_RADIUS, 0, sl - ATT_KW), BAND_RADIUS)
            qrows = pl.ds(pl.multiple_of(j * ATT_QB, ATT_QB), ATT_QB)
            if dil == 1:
                q, kw, vw = q_ref[qrows, :], k_ref[pl.ds(ks, ATT_KW), :], v_ref[pl.ds(ks, ATT_KW), :]
            else:
                q, kw, vw = q_ref[r, qrows, :], k_ref[r, pl.ds(ks, ATT_KW), :], v_ref[r, pl.ds(ks, ATT_KW), :]
            s = lax.dot_general(q, kw, (((1,), (1,)), ((), ())), preferred_element_type=F32)
            s = jnp.where(jnp.abs(rel0 + (ks - l0)) <= BAND_RADIUS, s, MASK_VALUE)
            m = jnp.max(s, axis=-1, keepdims=True)
            p = jnp.exp(s - m)
            den = jnp.sum(p, axis=-1, keepdims=True)
            o = jnp.dot(p.astype(BF16), vw, preferred_element_type=F32) * (1.0 / den)
            lse = jnp.broadcast_to(m + jnp.log(den), (ATT_QB, HEAD_DIM))
            if dil == 1:
                dst = qrows
            else:
                dst = pl.ds(j * (ATT_QB * dil) + r, ATT_QB, stride=dil)
            o_scr[bi, dst, :] = o
            l_scr[bi, dst, :] = lse
            return carry

        lax.fori_loop(0, dil * nj, sub, 0, unroll=8)

    def merge(j, carry):
        rows = pl.ds(pl.multiple_of(j * ATT_QB, ATT_QB), ATT_QB)
        l0, l1, l2 = l_scr[0, rows, :], l_scr[1, rows, :], l_scr[2, rows, :]
        mx = jnp.maximum(jnp.maximum(l0, l1), l2)
        w0, w1, w2 = jnp.exp(l0 - mx), jnp.exp(l1 - mx), jnp.exp(l2 - mx)
        y = (o_scr[0, rows, :] * w0 + o_scr[1, rows, :] * w1 + o_scr[2, rows, :] * w2) * (1.0 / (w0 + w1 + w2))
        y_ref[rows, :] = y.astype(y_ref.dtype)
        return carry

    lax.fori_loop(0, nb // ATT_QB, merge, 0)


def _attention(q1, k1, v1, q4, k4, v4, q16, k16, v16):
    b, s, aw = q1.shape
    nb = min(s, ATT_NB)
    assert s % nb == 0 and s // 16 >= ATT_KW and nb % (16 * ATT_QB) == 0
    nat_q = pl.BlockSpec((None, nb, HEAD_DIM), lambda i, h, j: (i, j, h))
    nat_kv = pl.BlockSpec((None, s, HEAD_DIM), lambda i, h, j: (i, 0, h))
    grp_q = lambda dil: pl.BlockSpec((None, dil, nb // dil, HEAD_DIM), lambda i, h, j: (i, 0, j, h))
    grp_kv = lambda dil: pl.BlockSpec((None, dil, s // dil, HEAD_DIM), lambda i, h, j: (i, 0, 0, h))
    return pl.pallas_call(
        functools.partial(_attn_kernel, seq=s),
        grid=(b, N_HEADS, s // nb),
        in_specs=[nat_q, nat_kv, nat_kv, grp_q(4), grp_kv(4), grp_kv(4), grp_q(16), grp_kv(16), grp_kv(16)],
        out_specs=pl.BlockSpec((None, nb, HEAD_DIM), lambda i, h, j: (i, j, h)),
        out_shape=jax.ShapeDtypeStruct((b, s, aw), BF16),
        scratch_shapes=[pltpu.VMEM((len(DILATIONS), nb, HEAD_DIM), F32),
                        pltpu.VMEM((len(DILATIONS), nb, HEAD_DIM), F32)],
        compiler_params=_cparams("arbitrary", "arbitrary", "arbitrary"),
    )(q1, k1, v1, q4, k4, v4, q16, k16, v16)


def _lru_kernel(xf_ref, xfb_ref, xfa_ref, xr_ref, xrb_ref, xra_ref, cw_ref, cb_ref, wr_ref, wi_ref,
                br_ref, bi_ref, lam_ref, hf_ref, hr_ref, a_scr, b_scr, carry_scr):
    si = pl.program_id(1)
    ns = pl.num_programs(1)
    ts, w = xf_ref.shape
    ngroups = ts // SUBLANES

    @pl.when(si == 0)
    def _():
        carry_scr[...] = jnp.zeros_like(carry_scr)

    row = lax.broadcasted_iota(jnp.int32, (ts, LRU_BLOCK), 0)
    sub = row % SUBLANES
    cw = cw_ref[...]
    cb = cb_ref[...]

    def prepare(dirn, main_ref, before_ref, after_ref, tile):
        main = main_ref[...]
        before = jnp.where(tile == 0, 0.0, before_ref[...])
        after = jnp.where(tile == ns - 1, 0.0, after_ref[...])
        rowf = lax.broadcasted_iota(jnp.int32, (ts, w), 0)
        xm1 = jnp.where(rowf == 0, before[7:8], pltpu.roll(main, 1, axis=0))
        xm2 = jnp.where(rowf == 0, before[6:7],
                        jnp.where(rowf == 1, before[7:8], pltpu.roll(main, 2, axis=0)))
        xp1 = jnp.where(rowf == ts - 1, after[0:1], pltpu.roll(main, ts - 1, axis=0))
        xc = cb + xm2 * cw[0:1] + xm1 * cw[1:2] + main * cw[2:3] + xp1 * cw[3:4]
        xcb = xc.astype(BF16)
        lam = lam_ref[dirn:dirn + 1, :]
        sp = jnp.maximum(-lam, 0.0) + jnp.log1p(jnp.exp(-jnp.abs(lam)))
        for n in range(w // LRU_BLOCK):
            cols = slice(n * LRU_BLOCK, (n + 1) * LRU_BLOCK)
            blk = xcb[:, cols]
            r = jax.nn.sigmoid(jnp.dot(blk, wr_ref[dirn, n], preferred_element_type=F32)
                               + br_ref[dirn:dirn + 1, cols])
            gi = jax.nn.sigmoid(jnp.dot(blk, wi_ref[dirn, n], preferred_element_type=F32)
                                + bi_ref[dirn:dirn + 1, cols])
            a = jnp.exp(-LRU_C * r * sp[:, cols])
            bt = jnp.sqrt(jnp.maximum(1.0 - a * a, 0.0)) * (gi * xc[:, cols])
            for shift in (1, 2, 4):
                if dirn == 0:
                    a_sh = pltpu.roll(a, shift, axis=0)
                    b_sh = pltpu.roll(bt, shift, axis=0)
                    ok = sub >= shift
                else:
                    a_sh = pltpu.roll(a, ts - shift, axis=0)
                    b_sh = pltpu.roll(bt, ts - shift, axis=0)
                    ok = sub < SUBLANES - shift
                bt = jnp.where(ok, a * b_sh + bt, bt)
                a = jnp.where(ok, a * a_sh, a)
            a_scr[dirn, :, cols] = a
            b_scr[dirn, :, cols] = bt

    prepare(0, xf_ref, xfb_ref, xfa_ref, si)
    prepare(1, xr_ref, xrb_ref, xra_ref, ns - 1 - si)

    def body(g, carry):
        cf, cr = carry
        rf = pl.ds(pl.multiple_of(g * SUBLANES, SUBLANES), SUBLANES)
        hf = b_scr[0, rf, :] + a_scr[0, rf, :] * cf
        hf_ref[rf, :] = hf
        cf = jnp.broadcast_to(hf[SUBLANES - 1:SUBLANES, :], (SUBLANES, w))
        rr = pl.ds(pl.multiple_of((ngroups - 1 - g) * SUBLANES, SUBLANES), SUBLANES)
        hr = b_scr[1, rr, :] + a_scr[1, rr, :] * cr
        hr_ref[rr, :] = hr
        cr = jnp.broadcast_to(hr[0:1, :], (SUBLANES, w))
        return cf, cr

    cf, cr = lax.fori_loop(0, ngroups, body, (carry_scr[0], carry_scr[1]))
    carry_scr[0] = cf
    carry_scr[1] = cr


def _lru_call(xb, conv_w, conv_b, wr_bf, wi_bf, b_r, b_i, lam):
    b, s, w = xb.shape
    ts = 256
    ns = s // ts
    hb = ts // SUBLANES
    nh = s // SUBLANES
    fwd = lambda i, j: (i, j, 0)
    rev = lambda i, j: (i, ns - 1 - j, 0)
    fwd_before = lambda i, j: (i, jnp.maximum(j * hb - 1, 0), 0)
    fwd_after = lambda i, j: (i, jnp.minimum((j + 1) * hb, nh - 1), 0)
    rev_before = lambda i, j: (i, jnp.maximum((ns - 1 - j) * hb - 1, 0), 0)
    rev_after = lambda i, j: (i, jnp.minimum((ns - j) * hb, nh - 1), 0)
    main = lambda im: pl.BlockSpec((None, ts, w), im)
    halo = lambda im: pl.BlockSpec((None, SUBLANES, w), im)
    return pl.pallas_call(
        _lru_kernel,
        grid=(b, ns),
        in_specs=[main(fwd), halo(fwd_before), halo(fwd_after),
                  main(rev), halo(rev_before), halo(rev_after),
                  _const_spec(conv_w.shape), _const_spec(conv_b.shape),
                  _const_spec(wr_bf.shape), _const_spec(wi_bf.shape),
                  _const_spec(b_r.shape), _const_spec(b_i.shape), _const_spec(lam.shape)],
        out_specs=[main(fwd), main(rev)],
        out_shape=[jax.ShapeDtypeStruct((b, s, w), F32)] * 2,
        scratch_shapes=[pltpu.VMEM((2, ts, w), F32), pltpu.VMEM((2, ts, w), F32),
                        pltpu.VMEM((2, SUBLANES, w), F32)],
        compiler_params=_cparams("arbitrary", "arbitrary"),
    )(xb, xb, xb, xb, xb, xb, conv_w, conv_b, wr_bf, wi_bf, b_r, b_i, lam)


def _outproj_kernel(att_ref, hf_ref, hr_ref, gate_ref, x_ref, mod_ref, ga_ref, gl_ref, w_ref, g2_ref,
                    x1_ref, h2t_ref):
    modv = mod_ref[...]
    g1, sh2, sc2 = modv[2:3], modv[3:4], modv[4:5]
    an = _rms(att_ref[...].astype(F32), ga_ref[...])
    lru = (hf_ref[...] + hr_ref[...]) * _gelu(gate_ref[...])
    ln = _rms(lru, gl_ref[...])
    aw = an.shape[-1]
    mix = (jnp.dot(an.astype(BF16), w_ref[0:aw, :], preferred_element_type=F32)
           + jnp.dot(ln.astype(BF16), w_ref[aw:, :], preferred_element_type=F32))
    x1 = x_ref[...] + g1 * mix
    x1_ref[...] = x1
    h2 = _rms(x1, g2_ref[...]) * (1.0 + sc2) + sh2
    h2t_ref[...] = h2.T.astype(BF16)


def _outproj_call(att, hf, hr, gate, x, mod3, ga, gl, w_out_bf, g2):
    b, s, d = x.shape
    aw, lw = att.shape[-1], hf.shape[-1]
    tm = 256
    nt = s // tm
    tok = lambda i, j: (i, j, 0)
    return pl.pallas_call(
        _outproj_kernel,
        grid=(b, nt),
        in_specs=[pl.BlockSpec((None, tm, aw), tok), pl.BlockSpec((None, tm, lw), tok),
                  pl.BlockSpec((None, tm, lw), tok), pl.BlockSpec((None, tm, lw), tok),
                  pl.BlockSpec((None, tm, d), tok),
                  pl.BlockSpec((None, N_MOD, d), lambda i, j: (i, 0, 0)),
                  _const_spec((1, aw)), _const_spec((1, lw)), _const_spec(w_out_bf.shape),
                  _const_spec((1, d))],
        out_specs=[pl.BlockSpec((None, tm, d), tok),
                   pl.BlockSpec((d, tm), lambda i, j: (0, i * nt + j))],
        out_shape=[jax.ShapeDtypeStruct((b, s, d), F32), jax.ShapeDtypeStruct((d, b * s), BF16)],
        compiler_params=_cparams("arbitrary", "arbitrary"),
    )(att, hf, hr, gate, x, mod3, ga, gl, w_out_bf, g2)


def _top16_rows(s):
    t = s.shape[1]
    rowi = lax.broadcasted_iota(jnp.int32, (PEER_TOPK, t), 0)
    out = jnp.full((PEER_TOPK, t), NEG_INF, F32)
    rank = jnp.full(s.shape, float(PEER_TOPK), F32)
    cur = s
    for kk in range(PEER_TOPK):
        m = jnp.max(cur, axis=0, keepdims=True)
        out = jnp.where(rowi == kk, m, out)
        hit = cur == m
        rank = jnp.where(hit, float(kk), rank)
        cur = jnp.where(hit, NEG_INF, cur)
    return out, rank


def _product_key_select(s1, s2):
    t = s1.shape[1]
    rowi = lax.broadcasted_iota(jnp.int32, (PEER_TOPK, t), 0)
    a16, rank1 = _top16_rows(s1)
    b16, rank2 = _top16_rows(s2)
    pieces = []
    for i in range(4):
        cnt = PEER_TOPK // (i + 1)
        pieces.append(jnp.where(rowi < cnt, a16[i:i + 1, :] + b16, NEG_INF))
    for j in range(3):
        cnt = PEER_TOPK // (j + 1)
        pieces.append(jnp.where((rowi >= 4) & (rowi < cnt), a16 + b16[j:j + 1, :], NEG_INF))
    top = a16[0:1, :] + b16[0:1, :]
    z = jnp.zeros((1, t), F32)
    tau = top
    for kk in range(PEER_TOPK):
        m = pieces[0]
        for pc in pieces[1:]:
            m = jnp.maximum(m, pc)
        m = jnp.max(m, axis=0, keepdims=True)
        z = z + jnp.exp(m - top)
        tau = m
        pieces = [jnp.where(pc == m, NEG_INF, pc) for pc in pieces]
    cnt = jnp.zeros((PEER_TOPK, t), F32)
    for l in range(PEER_TOPK):
        cnt = cnt + jnp.where(a16 + b16[l:l + 1, :] >= tau, 1.0, 0.0)
    n1 = jnp.zeros(s1.shape, F32)
    for r in range(PEER_TOPK):
        n1 = jnp.where(rank1 == float(r), cnt[r:r + 1, :], n1)
    return n1, jnp.exp(s1 - a16[0:1, :]) * (1.0 / z), rank2, jnp.exp(s2 - b16[0:1, :])


def _peerq_kernel(h2t_ref, wq_ref, sk_ref, n_ref, a_ref, cj_ref, bj_ref, qt_scr):
    tq = h2t_ref.shape[1]
    qt_scr[...] = jnp.dot(wq_ref[...], h2t_ref[...], preferred_element_type=F32).astype(BF16)

    def head(h, carry):
        r1 = pl.ds(pl.multiple_of(2 * h * PEER_KEYS, PEER_KEYS), PEER_KEYS)
        r2 = pl.ds(pl.multiple_of((2 * h + 1) * PEER_KEYS, PEER_KEYS), PEER_KEYS)
        s1 = jnp.dot(sk_ref[2 * h], qt_scr[r1, :], preferred_element_type=F32)
        s2 = jnp.dot(sk_ref[2 * h + 1], qt_scr[r2, :], preferred_element_type=F32)
        for st in range(tq // LANES):
            cols = slice(st * LANES, (st + 1) * LANES)
            n1, a, rank2, b = _product_key_select(s1[:, cols], s2[:, cols])
            n_ref[h, :, cols] = n1
            a_ref[h, :, cols] = a
            cj_ref[h, :, cols] = rank2.astype(BF16)
            bj_ref[h, :, cols] = b.astype(BF16)
        return carry

    lax.fori_loop(0, PEER_HEADS, head, 0)


def _peerq_call(h2t, wq_t_bf, sk_bf):
    d, t = h2t.shape
    tq = 256
    big = lambda: pl.BlockSpec((PEER_HEADS, PEER_KEYS, tq), lambda i: (0, 0, i))
    shp = lambda dt: jax.ShapeDtypeStruct((PEER_HEADS, PEER_KEYS, t), dt)
    return pl.pallas_call(
        _peerq_kernel,
        grid=(t // tq,),
        in_specs=[pl.BlockSpec((d, tq), lambda i: (0, i)),
                  _const_spec(wq_t_bf.shape), _const_spec(sk_bf.shape)],
        out_specs=[big(), big(), big(), big()],
        out_shape=[shp(F32), shp(F32), shp(BF16), shp(BF16)],
        scratch_shapes=[pltpu.VMEM((wq_t_bf.shape[0], tq), BF16)],
        compiler_params=_cparams("arbitrary"),
    )(h2t, wq_t_bf, sk_bf)


PEER_STRIP = 256


def _peer_kernel(h2t_ref, u_ref, vt_ref, n_ref, a_ref, cj_ref, bj_ref, acc_ref, s_scr, c_even, c_odd):
    c = pl.program_id(1)
    ce, tm = s_scr.shape

    @pl.when(c == 0)
    def _():
        acc_ref[...] = jnp.zeros_like(acc_ref)
        c_odd[...] = jnp.zeros_like(c_odd)

    def step(c_write, c_read):
        s_scr[...] = jnp.dot(u_ref[...], h2t_ref[...], preferred_element_type=F32)
        for il in range(ce // PEER_KEYS):
            rows = slice(il * PEER_KEYS, (il + 1) * PEER_KEYS)
            for si in range(tm // PEER_STRIP):
                cols = slice(si * PEER_STRIP, (si + 1) * PEER_STRIP)
                w = jnp.zeros((PEER_KEYS, PEER_STRIP), BF16)
                for h in range(PEER_HEADS):
                    nrow = n_ref[h, il:il + 1, cols].astype(BF16)
                    arow = a_ref[h, il:il + 1, cols].astype(BF16)
                    sel = cj_ref[h, :, cols] < nrow
                    w = w + jnp.where(sel, bj_ref[h, :, cols], jnp.zeros((), BF16)) * arow
                act = _gelu(s_scr[rows, cols]).astype(BF16)
                c_write[rows, cols] = w * act
        acc_ref[...] += jnp.dot(vt_ref[...], c_read[...], preferred_element_type=F32)

    @pl.when(c % 2 == 0)
    def _():
        step(c_even, c_odd)

    @pl.when(c % 2 == 1)
    def _():
        step(c_odd, c_even)


def _peer_call(h2t, u_bf, vt_bf, nt, at, cjt, bjt):
    d, t = h2t.shape
    ne = u_bf.shape[0]
    tm = 512
    ce = 1024
    nc = ne // ce
    nsub = ce // PEER_KEYS
    sub = lambda: pl.BlockSpec((PEER_HEADS, nsub, tm), lambda i, c: (0, jnp.minimum(c, nc - 1), i))
    allk = lambda: pl.BlockSpec((PEER_HEADS, PEER_KEYS, tm), lambda i, c: (0, 0, i))
    return pl.pallas_call(
        _peer_kernel,
        grid=(t // tm, nc + 1),
        in_specs=[pl.BlockSpec((d, tm), lambda i, c: (0, i)),
                  pl.BlockSpec((ce, d), lambda i, c: (jnp.minimum(c, nc - 1), 0)),
                  pl.BlockSpec((d, ce), lambda i, c: (0, jnp.maximum(c - 1, 0))),
                  sub(), sub(), allk(), allk()],
        out_specs=pl.BlockSpec((d, tm), lambda i, c: (0, i)),
        out_shape=jax.ShapeDtypeStruct((d, t), F32),
        scratch_shapes=[pltpu.VMEM((ce, tm), F32), pltpu.VMEM((ce, tm), BF16), pltpu.VMEM((ce, tm), BF16)],
        compiler_params=_cparams("arbitrary", "arbitrary"),
    )(h2t, u_bf, vt_bf, nt, at, cjt, bjt)


def _final_kernel(pt_ref, x1_ref, mod_ref, gf_ref, y_ref):
    g2 = mod_ref[...][5:6]
    x2 = x1_ref[...] + g2 * pt_ref[...].T
    y_ref[...] = _rms(x2, gf_ref[...])


def _final_call(peer_t, x1, mod3, gf):
    b, s, d = x1.shape
    tm = 256
    nt = s // tm
    tok = lambda i, j: (i, j, 0)
    return pl.pallas_call(
        _final_kernel,
        grid=(b, nt),
        in_specs=[pl.BlockSpec((d, tm), lambda i, j: (0, i * nt + j)),
                  pl.BlockSpec((None, tm, d), tok),
                  pl.BlockSpec((None, N_MOD, d), lambda i, j: (i, 0, 0)),
                  _const_spec((1, d))],
        out_specs=pl.BlockSpec((None, tm, d), tok),
        out_shape=jax.ShapeDtypeStruct((b, s, d), F32),
        compiler_params=_cparams("arbitrary", "arbitrary"),
    )(peer_t, x1, mod3, gf)


def _layer(x, c, w_mod, b_mod, norm1_g, w_in, conv_w, conv_b, lru_w_r, lru_b_r, lru_w_i, lru_b_i,
           lru_lambda, attn_out_g, lru_out_g, w_out, norm2_g, peer_w_q, peer_sub_keys, peer_u, peer_v,
           norm_final_g):
    b, s, d = x.shape
    c8 = jnp.pad(c, ((0, (-b) % SUBLANES), (0, 0)))
    mod = _mod_call(c8, w_mod, b_mod[None, :])
    mod3 = mod.reshape(c8.shape[0], N_MOD, d)[:b]
    cosf, sinf = _rope_call(s)
    *qkv, xb, gate = _inproj_call(x, mod3, norm1_g[None, :], w_in.astype(BF16), cosf, sinf)
    att = _attention(*qkv)
    hf, hr = _lru_call(xb, conv_w, conv_b[None, :], lru_w_r.astype(BF16), lru_w_i.astype(BF16),
                       lru_b_r, lru_b_i, lru_lambda)
    x1, h2t = _outproj_call(att, hf, hr, gate, x, mod3, attn_out_g[None, :], lru_out_g[None, :],
                            w_out.astype(BF16), norm2_g[None, :])
    sk = peer_sub_keys.reshape(PEER_HEADS * 2, PEER_KEYS, -1).astype(BF16)
    nt, at, cjt, bjt = _peerq_call(h2t, peer_w_q.T.astype(BF16), sk)
    peer_t = _peer_call(h2t, peer_u.astype(BF16), peer_v.T.astype(BF16), nt, at, cjt, bjt)
    return _final_call(peer_t, x1, mod3, norm_final_g[None, :])


def kernel(x_prompt, x_sample, c_prompt, c_sample, w_mod, b_mod, norm1_g, w_in, conv_w, conv_b, lru_w_r,
           lru_b_r, lru_w_i, lru_b_i, lru_lambda, attn_out_g, lru_out_g, w_out, norm2_g, peer_w_q,
           peer_sub_keys, peer_u, peer_v, norm_final_g):
    assert w_mod.shape[0] == 1, "single-layer stack expected"
    assert x_prompt.shape[1:] == x_sample.shape[1:]
    nb = x_prompt.shape[0]
    x = jnp.concatenate([x_prompt, x_sample], axis=0)
    c = jnp.concatenate([c_prompt, c_sample], axis=0)
    y = _layer(x, c, w_mod[0], b_mod[0], norm1_g[0], w_in[0], conv_w[0], conv_b[0], lru_w_r[0], lru_b_r[0],
               lru_w_i[0], lru_b_i[0], lru_lambda[0], attn_out_g[0], lru_out_g[0], w_out[0], norm2_g[0],
               peer_w_q[0], peer_sub_keys[0], peer_u[0], peer_v[0], norm_final_g)
    return (y[:nb], y[nb:])
```

```python
import functools
import math

import jax
import jax.numpy as jnp
from jax import lax
from jax.experimental import pallas as pl
from jax.experimental.pallas import tpu as pltpu

F32 = jnp.float32
BF16 = jnp.bfloat16

HEAD_DIM = 128
N_HEADS = 8
ATTN_WIDTH = N_HEADS * HEAD_DIM
LRU_BLOCK = 128
DILATIONS = (16, 4, 1)
BAND_RADIUS = 64
ROPE_THETA = 10000.0
LRU_C = 8.0
PEER_HEADS = 8
PEER_KEYS = 128
PEER_TOPK = 16
N_MOD = 6
EPS = 1e-6
MASK_VALUE = -1e30
NEG_INF = float("-inf")

V7X_VMEM_BYTES = 64 * 1024 * 1024
VMEM_LIMIT = V7X_VMEM_BYTES - 8 * 1024 * 1024
LANES = 128
SUBLANES = 8


def _cparams(*sem):
    return pltpu.CompilerParams(dimension_semantics=sem, vmem_limit_bytes=VMEM_LIMIT)


def _const_spec(shape):
    nd = len(shape)
    return pl.BlockSpec(shape, lambda *_: (0,) * nd, pipeline_mode=pl.Buffered(1))


def _two_group_specs(block, n_first, n_tiles):
    first = pl.BlockSpec(block, lambda i, j: (jnp.minimum(i, n_first - 1), jnp.where(i < n_first, j, n_tiles - 1), 0))
    second = pl.BlockSpec(block, lambda i, j: (jnp.maximum(i - n_first, 0), jnp.where(i < n_first, 0, j), 0))
    return first, second


def _rms(x, g):
    return x * lax.rsqrt(jnp.mean(x * x, axis=-1, keepdims=True) + EPS) * g


def _gelu(x):
    return 0.5 * x * (1.0 + jnp.tanh(math.sqrt(2.0 / math.pi) * (x + 0.044715 * (x * x * x))))


def _mod_kernel(c_ref, w_ref, b_ref, o_ref):
    c = c_ref[...]
    sc = c * jax.nn.sigmoid(c)
    o_ref[...] = jnp.dot(sc, w_ref[...], precision=lax.Precision.HIGHEST,
                         preferred_element_type=F32) + b_ref[...]


def _mod_call(c8, w_mod, b_mod):
    rows, d = c8.shape
    n = w_mod.shape[1]
    tn = 1024
    return pl.pallas_call(
        _mod_kernel,
        grid=(n // tn,),
        in_specs=[pl.BlockSpec((rows, d), lambda j: (0, 0)),
                  pl.BlockSpec((d, tn), lambda j: (0, j)),
                  pl.BlockSpec((1, tn), lambda j: (0, j))],
        out_specs=pl.BlockSpec((rows, tn), lambda j: (0, j)),
        out_shape=jax.ShapeDtypeStruct((rows, n), F32),
        compiler_params=_cparams("arbitrary"),
    )(c8, w_mod, b_mod)


def _rope_kernel(inv_ref, cos_ref, sin_ref):
    ts = cos_ref.shape[0]
    pos = (pl.program_id(0) * ts + lax.broadcasted_iota(jnp.int32, (ts, HEAD_DIM), 0)).astype(F32)
    lane = lax.broadcasted_iota(jnp.int32, (ts, HEAD_DIM), 1)
    ang = pos * inv_ref[...]
    cos_ref[...] = jnp.cos(ang)
    sn = jnp.sin(ang)
    sin_ref[...] = jnp.where(lane < HEAD_DIM // 2, -sn, sn)


def _rope_call(seq):
    half = HEAD_DIM // 2
    inv = ROPE_THETA ** (-jnp.arange(half, dtype=F32) / half)
    inv2 = jnp.concatenate([inv, inv])[None, :]
    ts = min(seq, 1024)
    return pl.pallas_call(
        _rope_kernel,
        grid=(seq // ts,),
        in_specs=[pl.BlockSpec((1, HEAD_DIM), lambda i: (0, 0))],
        out_specs=[pl.BlockSpec((ts, HEAD_DIM), lambda i: (i, 0))] * 2,
        out_shape=[jax.ShapeDtypeStruct((seq, HEAD_DIM), F32)] * 2,
        compiler_params=_cparams("arbitrary"),
    )(inv2)


def _inproj_kernel(xa_ref, xb2_ref, mod_ref, g_ref, w_ref, cos_ref, sin_ref, p4_ref, p16_ref,
                   q1_ref, k1_ref, v1_ref, q4_ref, k4_ref, v4_ref, q16_ref, k16_ref, v16_ref,
                   xb_ref, gate_ref, *, n_first):
    x = jnp.where(pl.program_id(0) < n_first, xa_ref[...], xb2_ref[...])
    modv = mod_ref[...]
    sh1, sc1 = modv[0:1], modv[1:2]
    h = _rms(x, g_ref[...]) * (1.0 + sc1) + sh1
    hb = h.astype(BF16)
    cosv = cos_ref[...]
    sinv = sin_ref[...]
    aw = ATTN_WIDTH

    def rope(z, scale):
        parts = []
        for hd in range(N_HEADS):
            zs = z[:, hd * HEAD_DIM:(hd + 1) * HEAD_DIM]
            rot = pltpu.roll(zs, HEAD_DIM // 2, axis=1)
            r = zs * cosv + rot * sinv
            if scale != 1.0:
                r = r * scale
            parts.append(r.astype(BF16))
        return jnp.concatenate(parts, axis=1)

    def store_layouts(zb, o1_ref, o4_ref, o16_ref):
        o1_ref[...] = zb
        for p_ref, o_ref in ((p4_ref, o4_ref), (p16_ref, o16_ref)):
            perm = jnp.dot(p_ref[...], zb, preferred_element_type=F32).astype(BF16)
            o_ref[...] = perm.reshape(o_ref.shape)

    store_layouts(rope(jnp.dot(hb, w_ref[:, 0:aw], preferred_element_type=F32), HEAD_DIM ** -0.5),
                  q1_ref, q4_ref, q16_ref)
    store_layouts(rope(jnp.dot(hb, w_ref[:, aw:2 * aw], preferred_element_type=F32), 1.0),
                  k1_ref, k4_ref, k16_ref)
    store_layouts(jnp.dot(hb, w_ref[:, 2 * aw:3 * aw], preferred_element_type=F32).astype(BF16),
                  v1_ref, v4_ref, v16_ref)
    lw = xb_ref.shape[-1]
    xb_ref[...] = jnp.dot(hb, w_ref[:, 3 * aw:3 * aw + lw], preferred_element_type=F32)
    gate_ref[...] = jnp.dot(hb, w_ref[:, 3 * aw + lw:3 * aw + 2 * lw], preferred_element_type=F32)


def _residue_perm(tm, dil):
    src = jnp.arange(tm)
    dst = (src % dil) * (tm // dil) + src // dil
    return jnp.zeros((tm, tm), BF16).at[dst, src].set(1.0)


def _inproj_call(xa, xb2, mod3, g1, w_in_bf, cosf, sinf):
    n_first, s, d = xa.shape
    b = n_first + xb2.shape[0]
    aw = ATTN_WIDTH
    lw = (w_in_bf.shape[1] - 3 * aw) // 2
    tm = 256
    tok = lambda i, j: (i, j, 0)
    grouped = lambda i, j: (i, 0, j, 0)
    lay1 = pl.BlockSpec((None, tm, aw), tok)
    lay4 = pl.BlockSpec((None, 4, tm // 4, aw), grouped)
    lay16 = pl.BlockSpec((None, 16, tm // 16, aw), grouped)
    shp1 = jax.ShapeDtypeStruct((b, s, aw), BF16)
    shp4 = jax.ShapeDtypeStruct((b, 4, s // 4, aw), BF16)
    shp16 = jax.ShapeDtypeStruct((b, 16, s // 16, aw), BF16)
    return pl.pallas_call(
        functools.partial(_inproj_kernel, n_first=n_first),
        grid=(b, s // tm),
        in_specs=[*_two_group_specs((None, tm, d), n_first, s // tm),
                  pl.BlockSpec((None, N_MOD, d), lambda i, j: (i, 0, 0)),
                  _const_spec((1, d)),
                  _const_spec(w_in_bf.shape),
                  pl.BlockSpec((tm, HEAD_DIM), lambda i, j: (j, 0)),
                  pl.BlockSpec((tm, HEAD_DIM), lambda i, j: (j, 0)),
                  _const_spec((tm, tm)), _const_spec((tm, tm))],
        out_specs=[lay1] * 3 + [lay4] * 3 + [lay16] * 3 + [pl.BlockSpec((None, tm, lw), tok)] * 2,
        out_shape=[shp1] * 3 + [shp4] * 3 + [shp16] * 3 + [jax.ShapeDtypeStruct((b, s, lw), F32)] * 2,
        compiler_params=_cparams("arbitrary", "arbitrary"),
    )(xa, xb2, mod3, g1, w_in_bf, cosf, sinf, _residue_perm(tm, 4), _residue_perm(tm, 16))


ATT_QB = 128
ATT_KW = ATT_QB + 2 * BAND_RADIUS


ATT_NB = 2048


def _attn_kernel(q1_ref, k1_ref, v1_ref, q4_ref, k4_ref, v4_ref, q16_ref, k16_ref, v16_ref, y_ref,
                 o_scr, l_scr, *, seq):
    blk = pl.program_id(2)
    nb = y_ref.shape[0]
    rel0 = (lax.broadcasted_iota(jnp.int32, (ATT_QB, ATT_KW), 1)
            - lax.broadcasted_iota(jnp.int32, (ATT_QB, ATT_KW), 0))
    branches = ((q16_ref, k16_ref, v16_ref, 16), (q4_ref, k4_ref, v4_ref, 4), (q1_ref, k1_ref, v1_ref, 1))
    for bi, (q_ref, k_ref, v_ref, dil) in enumerate(branches):
        per = nb // dil
        nj = per // ATT_QB
        sl = seq // dil

        def sub(idx, carry, bi=bi, q_ref=q_ref, k_ref=k_ref, v_ref=v_ref, dil=dil, per=per, nj=nj, sl=sl):
            r = idx // nj
            j = idx % nj
            l0 = blk * per + j * ATT_QB
            ks = pl.multiple_of(jnp.clip(l0 - BAND_RADIUS, 0, sl - ATT_KW), BAND_RADIUS)
            qrows = pl.ds(pl.multiple_of(j * ATT_QB, ATT_QB), ATT_QB)
            if dil == 1:
                q, kw, vw = q_ref[qrows, :], k_ref[pl.ds(ks, ATT_KW), :], v_ref[pl.ds(ks, ATT_KW), :]
            else:
                q, kw, vw = q_ref[r, qrows, :], k_ref[r, pl.ds(ks, ATT_KW), :], v_ref[r, pl.ds(ks, ATT_KW), :]
            s = lax.dot_general(q, kw, (((1,), (1,)), ((), ())), preferred_element_type=F32)
            s = jnp.where(jnp.abs(rel0 + (ks - l0)) <= BAND_RADIUS, s, MASK_VALUE)
            m = jnp.max(s, axis=-1, keepdims=True)
            p = jnp.exp(s - m)
            den = jnp.sum(p, axis=-1, keepdims=True)
            o = jnp.dot(p.astype(BF16), vw, preferred_element_type=F32) * (1.0 / den)
            lse = jnp.broadcast_to(m + jnp.log(den), (ATT_QB, HEAD_DIM))
            if dil == 1:
                dst = qrows
            else:
                dst = pl.ds(j * (ATT_QB * dil) + r, ATT_QB, stride=dil)
            o_scr[bi, dst, :] = o
            l_scr[bi, dst, :] = lse
            return carry

        lax.fori_loop(0, dil * nj, sub, 0, unroll=8)

    def merge(j, carry):
        rows = pl.ds(pl.multiple_of(j * ATT_QB, ATT_QB), ATT_QB)
        l0, l1, l2 = l_scr[0, rows, :], l_scr[1, rows, :], l_scr[2, rows, :]
        mx = jnp.maximum(jnp.maximum(l0, l1), l2)
        w0, w1, w2 = jnp.exp(l0 - mx), jnp.exp(l1 - mx), jnp.exp(l2 - mx)
        y = (o_scr[0, rows, :] * w0 + o_scr[1, rows, :] * w1 + o_scr[2, rows, :] * w2) * (1.0 / (w0 + w1 + w2))
        y_ref[rows, :] = y.astype(y_ref.dtype)
        return carry

    lax.fori_loop(0, nb // ATT_QB, merge, 0)


def _attention(q1, k1, v1, q4, k4, v4, q16, k16, v16):
    b, s, aw = q1.shape
    nb = min(s, ATT_NB)
    assert s % nb == 0 and s // 16 >= ATT_KW and nb % (16 * ATT_QB) == 0
    nat_q = pl.BlockSpec((None, nb, HEAD_DIM), lambda i, h, j: (i, j, h))
    nat_kv = pl.BlockSpec((None, s, HEAD_DIM), lambda i, h, j: (i, 0, h))
    grp_q = lambda dil: pl.BlockSpec((None, dil, nb // dil, HEAD_DIM), lambda i, h, j: (i, 0, j, h))
    grp_kv = lambda dil: pl.BlockSpec((None, dil, s // dil, HEAD_DIM), lambda i, h, j: (i, 0, 0, h))
    return pl.pallas_call(
        functools.partial(_attn_kernel, seq=s),
        grid=(b, N_HEADS, s // nb),
        in_specs=[nat_q, nat_kv, nat_kv, grp_q(4), grp_kv(4), grp_kv(4), grp_q(16), grp_kv(16), grp_kv(16)],
        out_specs=pl.BlockSpec((None, nb, HEAD_DIM), lambda i, h, j: (i, j, h)),
        out_shape=jax.ShapeDtypeStruct((b, s, aw), BF16),
        scratch_shapes=[pltpu.VMEM((len(DILATIONS), nb, HEAD_DIM), F32),
                        pltpu.VMEM((len(DILATIONS), nb, HEAD_DIM), F32)],
        compiler_params=_cparams("arbitrary", "arbitrary", "arbitrary"),
    )(q1, k1, v1, q4, k4, v4, q16, k16, v16)


def _lru_kernel(xf_ref, xfb_ref, xfa_ref, xr_ref, xrb_ref, xra_ref, cw_ref, cb_ref, wr_ref, wi_ref,
                br_ref, bi_ref, lam_ref, hf_ref, hr_ref, a_scr, b_scr, carry_scr):
    si = pl.program_id(1)
    ns = pl.num_programs(1)
    ts, w = xf_ref.shape
    ngroups = ts // SUBLANES

    @pl.when(si == 0)
    def _():
        carry_scr[...] = jnp.zeros_like(carry_scr)

    row = lax.broadcasted_iota(jnp.int32, (ts, LRU_BLOCK), 0)
    sub = row % SUBLANES
    cw = cw_ref[...]
    cb = cb_ref[...]

    def prepare(dirn, main_ref, before_ref, after_ref, tile):
        main = main_ref[...]
        before = jnp.where(tile == 0, 0.0, before_ref[...])
        after = jnp.where(tile == ns - 1, 0.0, after_ref[...])
        rowf = lax.broadcasted_iota(jnp.int32, (ts, w), 0)
        xm1 = jnp.where(rowf == 0, before[7:8], pltpu.roll(main, 1, axis=0))
        xm2 = jnp.where(rowf == 0, before[6:7],
                        jnp.where(rowf == 1, before[7:8], pltpu.roll(main, 2, axis=0)))
        xp1 = jnp.where(rowf == ts - 1, after[0:1], pltpu.roll(main, ts - 1, axis=0))
        xc = cb + xm2 * cw[0:1] + xm1 * cw[1:2] + main * cw[2:3] + xp1 * cw[3:4]
        xcb = xc.astype(BF16)
        lam = lam_ref[dirn:dirn + 1, :]
        sp = jnp.maximum(-lam, 0.0) + jnp.log1p(jnp.exp(-jnp.abs(lam)))
        for n in range(w // LRU_BLOCK):
            cols = slice(n * LRU_BLOCK, (n + 1) * LRU_BLOCK)
            blk = xcb[:, cols]
            r = jax.nn.sigmoid(jnp.dot(blk, wr_ref[dirn, n], preferred_element_type=F32)
                               + br_ref[dirn:dirn + 1, cols])
            gi = jax.nn.sigmoid(jnp.dot(blk, wi_ref[dirn, n], preferred_element_type=F32)
                                + bi_ref[dirn:dirn + 1, cols])
            a = jnp.exp(-LRU_C * r * sp[:, cols])
            bt = jnp.sqrt(jnp.maximum(1.0 - a * a, 0.0)) * (gi * xc[:, cols])
            for shift in (1, 2, 4):
                if dirn == 0:
                    a_sh = pltpu.roll(a, shift, axis=0)
                    b_sh = pltpu.roll(bt, shift, axis=0)
                    ok = sub >= shift
                else:
                    a_sh = pltpu.roll(a, ts - shift, axis=0)
                    b_sh = pltpu.roll(bt, ts - shift, axis=0)
                    ok = sub < SUBLANES - shift
                bt = jnp.where(ok, a * b_sh + bt, bt)
                a = jnp.where(ok, a * a_sh, a)
            a_scr[dirn, :, cols] = a
            b_scr[dirn, :, cols] = bt

    prepare(0, xf_ref, xfb_ref, xfa_ref, si)
    prepare(1, xr_ref, xrb_ref, xra_ref, ns - 1 - si)

    def body(g, carry):
        cf, cr = carry
        rf = pl.ds(pl.multiple_of(g * SUBLANES, SUBLANES), SUBLANES)
        hf = b_scr[0, rf, :] + a_scr[0, rf, :] * cf
        hf_ref[rf, :] = hf
        cf = jnp.broadcast_to(hf[SUBLANES - 1:SUBLANES, :], (SUBLANES, w))
        rr = pl.ds(pl.multiple_of((ngroups - 1 - g) * SUBLANES, SUBLANES), SUBLANES)
        hr = b_scr[1, rr, :] + a_scr[1, rr, :] * cr
        hr_ref[rr, :] = hr
        cr = jnp.broadcast_to(hr[0:1, :], (SUBLANES, w))
        return cf, cr

    cf, cr = lax.fori_loop(0, ngroups, body, (carry_scr[0], carry_scr[1]))
    carry_scr[0] = cf
    carry_scr[1] = cr


def _lru_call(xb, conv_w, conv_b, wr_bf, wi_bf, b_r, b_i, lam):
    b, s, w = xb.shape
    ts = 256
    ns = s // ts
    hb = ts // SUBLANES
    nh = s // SUBLANES
    fwd = lambda i, j: (i, j, 0)
    rev = lambda i, j: (i, ns - 1 - j, 0)
    fwd_before = lambda i, j: (i, jnp.maximum(j * hb - 1, 0), 0)
    fwd_after = lambda i, j: (i, jnp.minimum((j + 1) * hb, nh - 1), 0)
    rev_before = lambda i, j: (i, jnp.maximum((ns - 1 - j) * hb - 1, 0), 0)
    rev_after = lambda i, j: (i, jnp.minimum((ns - j) * hb, nh - 1), 0)
    main = lambda im: pl.BlockSpec((None, ts, w), im)
    halo = lambda im: pl.BlockSpec((None, SUBLANES, w), im)
    return pl.pallas_call(
        _lru_kernel,
        grid=(b, ns),
        in_specs=[main(fwd), halo(fwd_before), halo(fwd_after),
                  main(rev), halo(rev_before), halo(rev_after),
                  _const_spec(conv_w.shape), _const_spec(conv_b.shape),
                  _const_spec(wr_bf.shape), _const_spec(wi_bf.shape),
                  _const_spec(b_r.shape), _const_spec(b_i.shape), _const_spec(lam.shape)],
        out_specs=[main(fwd), main(rev)],
        out_shape=[jax.ShapeDtypeStruct((b, s, w), F32)] * 2,
        scratch_shapes=[pltpu.VMEM((2, ts, w), F32), pltpu.VMEM((2, ts, w), F32),
                        pltpu.VMEM((2, SUBLANES, w), F32)],
        compiler_params=_cparams("arbitrary", "arbitrary"),
    )(xb, xb, xb, xb, xb, xb, conv_w, conv_b, wr_bf, wi_bf, b_r, b_i, lam)


def _outproj_kernel(att_ref, hf_ref, hr_ref, gate_ref, xa_ref, xb2_ref, mod_ref, ga_ref, gl_ref, w_ref, g2_ref,
                    x1_ref, h2t_ref, *, n_first):
    modv = mod_ref[...]
    g1, sh2, sc2 = modv[2:3], modv[3:4], modv[4:5]
    an = _rms(att_ref[...].astype(F32), ga_ref[...])
    lru = (hf_ref[...] + hr_ref[...]) * _gelu(gate_ref[...])
    ln = _rms(lru, gl_ref[...])
    aw = an.shape[-1]
    mix = (jnp.dot(an.astype(BF16), w_ref[0:aw, :], preferred_element_type=F32)
           + jnp.dot(ln.astype(BF16), w_ref[aw:, :], preferred_element_type=F32))
    x1 = jnp.where(pl.program_id(0) < n_first, xa_ref[...], xb2_ref[...]) + g1 * mix
    x1_ref[...] = x1
    h2 = _rms(x1, g2_ref[...]) * (1.0 + sc2) + sh2
    h2t_ref[...] = h2.T.astype(BF16)


def _outproj_call(att, hf, hr, gate, xa, xb2, mod3, ga, gl, w_out_bf, g2):
    n_first, s, d = xa.shape
    b = n_first + xb2.shape[0]
    aw, lw = att.shape[-1], hf.shape[-1]
    tm = 256
    nt = s // tm
    tok = lambda i, j: (i, j, 0)
    return pl.pallas_call(
        functools.partial(_outproj_kernel, n_first=n_first),
        grid=(b, nt),
        in_specs=[pl.BlockSpec((None, tm, aw), tok), pl.BlockSpec((None, tm, lw), tok),
                  pl.BlockSpec((None, tm, lw), tok), pl.BlockSpec((None, tm, lw), tok),
                  *_two_group_specs((None, tm, d), n_first, nt),
                  pl.BlockSpec((None, N_MOD, d), lambda i, j: (i, 0, 0)),
                  _const_spec((1, aw)), _const_spec((1, lw)), _const_spec(w_out_bf.shape),
                  _const_spec((1, d))],
        out_specs=[pl.BlockSpec((None, tm, d), tok),
                   pl.BlockSpec((d, tm), lambda i, j: (0, i * nt + j))],
        out_shape=[jax.ShapeDtypeStruct((b, s, d), F32), jax.ShapeDtypeStruct((d, b * s), BF16)],
        compiler_params=_cparams("arbitrary", "arbitrary"),
    )(att, hf, hr, gate, xa, xb2, mod3, ga, gl, w_out_bf, g2)


def _top16_rows(s):
    t = s.shape[1]
    rowi = lax.broadcasted_iota(jnp.int32, (PEER_TOPK, t), 0)
    out = jnp.full((PEER_TOPK, t), NEG_INF, F32)
    rank = jnp.full(s.shape, float(PEER_TOPK), F32)
    cur = s
    for kk in range(PEER_TOPK):
        m = jnp.max(cur, axis=0, keepdims=True)
        out = jnp.where(rowi == kk, m, out)
        hit = cur == m
        rank = jnp.where(hit, float(kk), rank)
        cur = jnp.where(hit, NEG_INF, cur)
    return out, rank


def _product_key_select(s1, s2):
    t = s1.shape[1]
    rowi = lax.broadcasted_iota(jnp.int32, (PEER_TOPK, t), 0)
    a16, rank1 = _top16_rows(s1)
    b16, rank2 = _top16_rows(s2)
    pieces = []
    for i in range(4):
        cnt = PEER_TOPK // (i + 1)
        pieces.append(jnp.where(rowi < cnt, a16[i:i + 1, :] + b16, NEG_INF))
    for j in range(3):
        cnt = PEER_TOPK // (j + 1)
        pieces.append(jnp.where((rowi >= 4) & (rowi < cnt), a16 + b16[j:j + 1, :], NEG_INF))
    top = a16[0:1, :] + b16[0:1, :]
    z = jnp.zeros((1, t), F32)
    tau = top
    for kk in range(PEER_TOPK):
        m = pieces[0]
        for pc in pieces[1:]:
            m = jnp.maximum(m, pc)
        m = jnp.max(m, axis=0, keepdims=True)
        z = z + jnp.exp(m - top)
        tau = m
        pieces = [jnp.where(pc == m, NEG_INF, pc) for pc in pieces]
    cnt = jnp.zeros((PEER_TOPK, t), F32)
    for l in range(PEER_TOPK):
        cnt = cnt + jnp.where(a16 + b16[l:l + 1, :] >= tau, 1.0, 0.0)
    n1 = jnp.zeros(s1.shape, F32)
    for r in range(PEER_TOPK):
        n1 = jnp.where(rank1 == float(r), cnt[r:r + 1, :], n1)
    return n1, jnp.exp(s1 - a16[0:1, :]) * (1.0 / z), rank2, jnp.exp(s2 - b16[0:1, :])


def _peerq_kernel(h2t_ref, wq_ref, sk_ref, n_ref, a_ref, cj_ref, bj_ref, qt_scr):
    tq = h2t_ref.shape[1]
    qt_scr[...] = jnp.dot(wq_ref[...], h2t_ref[...], preferred_element_type=F32).astype(BF16)

    def head(h, carry):
        r1 = pl.ds(pl.multiple_of(2 * h * PEER_KEYS, PEER_KEYS), PEER_KEYS)
        r2 = pl.ds(pl.multiple_of((2 * h + 1) * PEER_KEYS, PEER_KEYS), PEER_KEYS)
        s1 = jnp.dot(sk_ref[2 * h], qt_scr[r1, :], preferred_element_type=F32)
        s2 = jnp.dot(sk_ref[2 * h + 1], qt_scr[r2, :], preferred_element_type=F32)
        for st in range(tq // LANES):
            cols = slice(st * LANES, (st + 1) * LANES)
            n1, a, rank2, b = _product_key_select(s1[:, cols], s2[:, cols])
            n_ref[h, :, cols] = n1
            a_ref[h, :, cols] = a
            cj_ref[h, :, cols] = rank2.astype(BF16)
            bj_ref[h, :, cols] = b.astype(BF16)
        return carry

    lax.fori_loop(0, PEER_HEADS, head, 0)


def _peerq_call(h2t, wq_t_bf, sk_bf):
    d, t = h2t.shape
    tq = 256
    big = lambda: pl.BlockSpec((PEER_HEADS, PEER_KEYS, tq), lambda i: (0, 0, i))
    shp = lambda dt: jax.ShapeDtypeStruct((PEER_HEADS, PEER_KEYS, t), dt)
    return pl.pallas_call(
        _peerq_kernel,
        grid=(t // tq,),
        in_specs=[pl.BlockSpec((d, tq), lambda i: (0, i)),
                  _const_spec(wq_t_bf.shape), _const_spec(sk_bf.shape)],
        out_specs=[big(), big(), big(), big()],
        out_shape=[shp(F32), shp(F32), shp(BF16), shp(BF16)],
        scratch_shapes=[pltpu.VMEM((wq_t_bf.shape[0], tq), BF16)],
        compiler_params=_cparams("arbitrary"),
    )(h2t, wq_t_bf, sk_bf)


PEER_STRIP = 256


PEER_VALUE_K = 256


def _peer_kernel(h2t_ref, u_ref, vt_ref, n_ref, a_ref, cj_ref, bj_ref, acc_ref, s_even, s_odd, c_scr):
    c = pl.program_id(1)
    ce, tm = s_even.shape

    @pl.when(c == 0)
    def _():
        acc_ref[...] = jnp.zeros_like(acc_ref)
        s_odd[...] = jnp.zeros_like(s_odd)

    def step(s_write, s_read):
        per_slice = PEER_VALUE_K // PEER_KEYS
        half = tm // 2

        def weigh(ks):
            for il in range(ks * per_slice, (ks + 1) * per_slice):
                rows = slice(il * PEER_KEYS, (il + 1) * PEER_KEYS)
                for si in range(tm // PEER_STRIP):
                    cols = slice(si * PEER_STRIP, (si + 1) * PEER_STRIP)
                    w = jnp.zeros((PEER_KEYS, PEER_STRIP), BF16)
                    for h in range(PEER_HEADS):
                        nrow = n_ref[h, il:il + 1, cols].astype(BF16)
                        arow = a_ref[h, il:il + 1, cols].astype(BF16)
                        sel = cj_ref[h, :, cols] < nrow
                        w = w + jnp.where(sel, bj_ref[h, :, cols], jnp.zeros((), BF16)) * arow
                    act = _gelu(s_read[rows, cols]).astype(BF16)
                    c_scr[rows, cols] = w * act

        def value(ks):
            kk = slice(ks * PEER_VALUE_K, (ks + 1) * PEER_VALUE_K)
            acc_ref[...] += jnp.dot(vt_ref[:, kk], c_scr[kk, :], preferred_element_type=F32)

        def score(piece):
            cols = slice(piece * half, (piece + 1) * half)
            s_write[:, cols] = jnp.dot(u_ref[...], h2t_ref[:, cols], preferred_element_type=F32)

        weigh(0)
        score(0)
        weigh(1)
        value(0)
        value(1)
        weigh(2)
        score(1)
        weigh(3)
        value(2)
        value(3)

    @pl.when(c % 2 == 0)
    def _():
        step(s_even, s_odd)

    @pl.when(c % 2 == 1)
    def _():
        step(s_odd, s_even)


def _peer_call(h2t, u_bf, vt_bf, nt, at, cjt, bjt):
    d, t = h2t.shape
    ne = u_bf.shape[0]
    tm = 512
    ce = 1024
    nc = ne // ce
    nsub = ce // PEER_KEYS
    prev = lambda c: jnp.maximum(c - 1, 0)
    sub = lambda: pl.BlockSpec((PEER_HEADS, nsub, tm), lambda i, c: (0, prev(c), i))
    allk = lambda: pl.BlockSpec((PEER_HEADS, PEER_KEYS, tm), lambda i, c: (0, 0, i))
    return pl.pallas_call(
        _peer_kernel,
        grid=(t // tm, nc + 1),
        in_specs=[pl.BlockSpec((d, tm), lambda i, c: (0, i)),
                  pl.BlockSpec((ce, d), lambda i, c: (jnp.minimum(c, nc - 1), 0)),
                  pl.BlockSpec((d, ce), lambda i, c: (0, prev(c))),
                  sub(), sub(), allk(), allk()],
        out_specs=pl.BlockSpec((d, tm), lambda i, c: (0, i)),
        out_shape=jax.ShapeDtypeStruct((d, t), F32),
        scratch_shapes=[pltpu.VMEM((ce, tm), F32), pltpu.VMEM((ce, tm), F32), pltpu.VMEM((ce, tm), BF16)],
        compiler_params=_cparams("arbitrary", "arbitrary"),
    )(h2t, u_bf, vt_bf, nt, at, cjt, bjt)


def _final_kernel(pt_ref, x1_ref, mod_ref, gf_ref, ya_ref, yb_ref, *, n_first):
    g2 = mod_ref[...][5:6]
    x2 = x1_ref[...] + g2 * pt_ref[...].T
    y = _rms(x2, gf_ref[...])

    @pl.when(pl.program_id(0) < n_first)
    def _():
        ya_ref[...] = y

    @pl.when(pl.program_id(0) >= n_first)
    def _():
        yb_ref[...] = y


def _final_call(peer_t, x1, mod3, gf, n_first):
    b, s, d = x1.shape
    tm = 256
    nt = s // tm
    tok = lambda i, j: (i, j, 0)
    return pl.pallas_call(
        functools.partial(_final_kernel, n_first=n_first),
        grid=(b, nt),
        in_specs=[pl.BlockSpec((d, tm), lambda i, j: (0, i * nt + j)),
                  pl.BlockSpec((None, tm, d), tok),
                  pl.BlockSpec((None, N_MOD, d), lambda i, j: (i, 0, 0)),
                  _const_spec((1, d))],
        out_specs=list(_two_group_specs((None, tm, d), n_first, nt)),
        out_shape=[jax.ShapeDtypeStruct((n_first, s, d), F32), jax.ShapeDtypeStruct((b - n_first, s, d), F32)],
        compiler_params=_cparams("arbitrary", "arbitrary"),
    )(peer_t, x1, mod3, gf)


def _layer(xa, xb2, c, w_mod, b_mod, norm1_g, w_in, conv_w, conv_b, lru_w_r, lru_b_r, lru_w_i, lru_b_i,
           lru_lambda, attn_out_g, lru_out_g, w_out, norm2_g, peer_w_q, peer_sub_keys, peer_u, peer_v,
           norm_final_g):
    n_first, s, d = xa.shape
    b = n_first + xb2.shape[0]
    c8 = jnp.pad(c, ((0, (-b) % SUBLANES), (0, 0)))
    mod = _mod_call(c8, w_mod, b_mod[None, :])
    mod3 = mod.reshape(c8.shape[0], N_MOD, d)[:b]
    cosf, sinf = _rope_call(s)
    *qkv, xb, gate = _inproj_call(xa, xb2, mod3, norm1_g[None, :], w_in.astype(BF16), cosf, sinf)
    att = _attention(*qkv)
    hf, hr = _lru_call(xb, conv_w, conv_b[None, :], lru_w_r.astype(BF16), lru_w_i.astype(BF16),
                       lru_b_r, lru_b_i, lru_lambda)
    x1, h2t = _outproj_call(att, hf, hr, gate, xa, xb2, mod3, attn_out_g[None, :], lru_out_g[None, :],
                            w_out.astype(BF16), norm2_g[None, :])
    sk = peer_sub_keys.reshape(PEER_HEADS * 2, PEER_KEYS, -1).astype(BF16)
    nt, at, cjt, bjt = _peerq_call(h2t, peer_w_q.T.astype(BF16), sk)
    peer_t = _peer_call(h2t, peer_u.astype(BF16), peer_v.T.astype(BF16), nt, at, cjt, bjt)
    return _final_call(peer_t, x1, mod3, norm_final_g[None, :], n_first)


def kernel(x_prompt, x_sample, c_prompt, c_sample, w_mod, b_mod, norm1_g, w_in, conv_w, conv_b, lru_w_r,
           lru_b_r, lru_w_i, lru_b_i, lru_lambda, attn_out_g, lru_out_g, w_out, norm2_g, peer_w_q,
           peer_sub_keys, peer_u, peer_v, norm_final_g):
    assert w_mod.shape[0] == 1, "single-layer stack expected"
    assert x_prompt.shape[1:] == x_sample.shape[1:]
    c = jnp.concatenate([c_prompt, c_sample], axis=0)
    y_prompt, y_sample = _layer(
        x_prompt, x_sample, c, w_mod[0], b_mod[0], norm1_g[0], w_in[0], conv_w[0], conv_b[0], lru_w_r[0],
        lru_b_r[0], lru_w_i[0], lru_b_i[0], lru_lambda[0], attn_out_g[0], lru_out_g[0], w_out[0], norm2_g[0],
        peer_w_q[0], peer_sub_keys[0], peer_u[0], peer_v[0], norm_final_g)
    return (y_prompt, y_sample)
```

```python
import functools
import math

import jax
import jax.numpy as jnp
from jax import lax
from jax.experimental import pallas as pl
from jax.experimental.pallas import tpu as pltpu

F32 = jnp.float32
BF16 = jnp.bfloat16

HEAD_DIM = 128
N_HEADS = 8
ATTN_WIDTH = N_HEADS * HEAD_DIM
LRU_BLOCK = 128
DILATIONS = (16, 4, 1)
BAND_RADIUS = 64
ROPE_THETA = 10000.0
LRU_C = 8.0
PEER_HEADS = 8
PEER_KEYS = 128
PEER_TOPK = 16
N_MOD = 6
EPS = 1e-6
MASK_VALUE = -1e30
NEG_INF = float("-inf")

V7X_VMEM_BYTES = 64 * 1024 * 1024
VMEM_LIMIT = V7X_VMEM_BYTES - 8 * 1024 * 1024
LANES = 128
SUBLANES = 8


def _cparams(*sem):
    return pltpu.CompilerParams(dimension_semantics=sem, vmem_limit_bytes=VMEM_LIMIT)


def _const_spec(shape):
    nd = len(shape)
    return pl.BlockSpec(shape, lambda *_: (0,) * nd, pipeline_mode=pl.Buffered(1))


def _two_group_specs(block, n_first, n_tiles):
    first = pl.BlockSpec(block, lambda i, j: (jnp.minimum(i, n_first - 1), jnp.where(i < n_first, j, n_tiles - 1), 0))
    second = pl.BlockSpec(block, lambda i, j: (jnp.maximum(i - n_first, 0), jnp.where(i < n_first, 0, j), 0))
    return first, second


def _rms(x, g):
    return x * lax.rsqrt(jnp.mean(x * x, axis=-1, keepdims=True) + EPS) * g


def _gelu(x):
    return 0.5 * x * (1.0 + jnp.tanh(math.sqrt(2.0 / math.pi) * (x + 0.044715 * (x * x * x))))


def _mod_kernel(c_ref, w_ref, b_ref, o_ref):
    c = c_ref[...]
    sc = c * jax.nn.sigmoid(c)
    o_ref[...] = jnp.dot(sc, w_ref[...], precision=lax.Precision.HIGHEST,
                         preferred_element_type=F32) + b_ref[...]


def _mod_call(c8, w_mod, b_mod):
    rows, d = c8.shape
    n = w_mod.shape[1]
    tn = 1024
    return pl.pallas_call(
        _mod_kernel,
        grid=(n // tn,),
        in_specs=[pl.BlockSpec((rows, d), lambda j: (0, 0)),
                  pl.BlockSpec((d, tn), lambda j: (0, j)),
                  pl.BlockSpec((1, tn), lambda j: (0, j))],
        out_specs=pl.BlockSpec((rows, tn), lambda j: (0, j)),
        out_shape=jax.ShapeDtypeStruct((rows, n), F32),
        compiler_params=_cparams("arbitrary"),
    )(c8, w_mod, b_mod)


def _rope_kernel(inv_ref, cos_ref, sin_ref):
    ts = cos_ref.shape[0]
    pos = (pl.program_id(0) * ts + lax.broadcasted_iota(jnp.int32, (ts, HEAD_DIM), 0)).astype(F32)
    lane = lax.broadcasted_iota(jnp.int32, (ts, HEAD_DIM), 1)
    ang = pos * inv_ref[...]
    cos_ref[...] = jnp.cos(ang)
    sn = jnp.sin(ang)
    sin_ref[...] = jnp.where(lane < HEAD_DIM // 2, -sn, sn)


def _rope_call(seq):
    half = HEAD_DIM // 2
    inv = ROPE_THETA ** (-jnp.arange(half, dtype=F32) / half)
    inv2 = jnp.concatenate([inv, inv])[None, :]
    ts = min(seq, 1024)
    return pl.pallas_call(
        _rope_kernel,
        grid=(seq // ts,),
        in_specs=[pl.BlockSpec((1, HEAD_DIM), lambda i: (0, 0))],
        out_specs=[pl.BlockSpec((ts, HEAD_DIM), lambda i: (i, 0))] * 2,
        out_shape=[jax.ShapeDtypeStruct((seq, HEAD_DIM), F32)] * 2,
        compiler_params=_cparams("arbitrary"),
    )(inv2)


def _inproj_kernel(xa_ref, xb2_ref, mod_ref, g_ref, w_ref, cos_ref, sin_ref, p4_ref, p16_ref,
                   q1_ref, k1_ref, v1_ref, q4_ref, k4_ref, v4_ref, q16_ref, k16_ref, v16_ref,
                   xb_ref, gate_ref, *, n_first):
    x = jnp.where(pl.program_id(0) < n_first, xa_ref[...], xb2_ref[...])
    modv = mod_ref[...]
    sh1, sc1 = modv[0:1], modv[1:2]
    h = _rms(x, g_ref[...]) * (1.0 + sc1) + sh1
    hb = h.astype(BF16)
    cosv = cos_ref[...]
    sinv = sin_ref[...]
    aw = ATTN_WIDTH

    def rope(z, scale):
        parts = []
        for hd in range(N_HEADS):
            zs = z[:, hd * HEAD_DIM:(hd + 1) * HEAD_DIM]
            rot = pltpu.roll(zs, HEAD_DIM // 2, axis=1)
            r = zs * cosv + rot * sinv
            if scale != 1.0:
                r = r * scale
            parts.append(r.astype(BF16))
        return jnp.concatenate(parts, axis=1)

    def store_layouts(zb, o1_ref, o4_ref, o16_ref):
        o1_ref[...] = zb
        for p_ref, o_ref in ((p4_ref, o4_ref), (p16_ref, o16_ref)):
            perm = jnp.dot(p_ref[...], zb, preferred_element_type=F32).astype(BF16)
            o_ref[...] = perm.reshape(o_ref.shape)

    store_layouts(rope(jnp.dot(hb, w_ref[:, 0:aw], preferred_element_type=F32), HEAD_DIM ** -0.5),
                  q1_ref, q4_ref, q16_ref)
    store_layouts(rope(jnp.dot(hb, w_ref[:, aw:2 * aw], preferred_element_type=F32), 1.0),
                  k1_ref, k4_ref, k16_ref)
    store_layouts(jnp.dot(hb, w_ref[:, 2 * aw:3 * aw], preferred_element_type=F32).astype(BF16),
                  v1_ref, v4_ref, v16_ref)
    lw = xb_ref.shape[-1]
    xb_ref[...] = jnp.dot(hb, w_ref[:, 3 * aw:3 * aw + lw], preferred_element_type=F32)
    gate_ref[...] = jnp.dot(hb, w_ref[:, 3 * aw + lw:3 * aw + 2 * lw], preferred_element_type=F32)


def _residue_perm(tm, dil):
    src = jnp.arange(tm)
    dst = (src % dil) * (tm // dil) + src // dil
    return jnp.zeros((tm, tm), BF16).at[dst, src].set(1.0)


def _inproj_call(xa, xb2, mod3, g1, w_in_bf, cosf, sinf):
    n_first, s, d = xa.shape
    b = n_first + xb2.shape[0]
    aw = ATTN_WIDTH
    lw = (w_in_bf.shape[1] - 3 * aw) // 2
    tm = 256
    tok = lambda i, j: (i, j, 0)
    grouped = lambda i, j: (i, 0, j, 0)
    lay1 = pl.BlockSpec((None, tm, aw), tok)
    lay4 = pl.BlockSpec((None, 4, tm // 4, aw), grouped)
    lay16 = pl.BlockSpec((None, 16, tm // 16, aw), grouped)
    shp1 = jax.ShapeDtypeStruct((b, s, aw), BF16)
    shp4 = jax.ShapeDtypeStruct((b, 4, s // 4, aw), BF16)
    shp16 = jax.ShapeDtypeStruct((b, 16, s // 16, aw), BF16)
    return pl.pallas_call(
        functools.partial(_inproj_kernel, n_first=n_first),
        grid=(b, s // tm),
        in_specs=[*_two_group_specs((None, tm, d), n_first, s // tm),
                  pl.BlockSpec((None, N_MOD, d), lambda i, j: (i, 0, 0)),
                  _const_spec((1, d)),
                  _const_spec(w_in_bf.shape),
                  pl.BlockSpec((tm, HEAD_DIM), lambda i, j: (j, 0)),
                  pl.BlockSpec((tm, HEAD_DIM), lambda i, j: (j, 0)),
                  _const_spec((tm, tm)), _const_spec((tm, tm))],
        out_specs=[lay1] * 3 + [lay4] * 3 + [lay16] * 3 + [pl.BlockSpec((None, tm, lw), tok)] * 2,
        out_shape=[shp1] * 3 + [shp4] * 3 + [shp16] * 3 + [jax.ShapeDtypeStruct((b, s, lw), F32)] * 2,
        compiler_params=_cparams("arbitrary", "arbitrary"),
    )(xa, xb2, mod3, g1, w_in_bf, cosf, sinf, _residue_perm(tm, 4), _residue_perm(tm, 16))


ATT_QB = 128
ATT_KW = ATT_QB + 2 * BAND_RADIUS


ATT_NB = 2048


def _attn_kernel(q1_ref, k1_ref, v1_ref, q4_ref, k4_ref, v4_ref, q16_ref, k16_ref, v16_ref, y_ref,
                 o_scr, l_scr, *, seq):
    blk = pl.program_id(2)
    nb = y_ref.shape[0]
    rel0 = (lax.broadcasted_iota(jnp.int32, (ATT_QB, ATT_KW), 1)
            - lax.broadcasted_iota(jnp.int32, (ATT_QB, ATT_KW), 0))
    branches = ((q16_ref, k16_ref, v16_ref, 16), (q4_ref, k4_ref, v4_ref, 4), (q1_ref, k1_ref, v1_ref, 1))
    for bi, (q_ref, k_ref, v_ref, dil) in enumerate(branches):
        per = nb // dil
        nj = per // ATT_QB
        sl = seq // dil

        def sub(idx, carry, bi=bi, q_ref=q_ref, k_ref=k_ref, v_ref=v_ref, dil=dil, per=per, nj=nj, sl=sl):
            r = idx // nj
            j = idx % nj
            l0 = blk * per + j * ATT_QB
            ks = pl.multiple_of(jnp.clip(l0 - BAND_RADIUS, 0, sl - ATT_KW), BAND_RADIUS)
            qrows = pl.ds(pl.multiple_of(j * ATT_QB, ATT_QB), ATT_QB)
            if dil == 1:
                q, kw, vw = q_ref[qrows, :], k_ref[pl.ds(ks, ATT_KW), :], v_ref[pl.ds(ks, ATT_KW), :]
            else:
                q, kw, vw = q_ref[r, qrows, :], k_ref[r, pl.ds(ks, ATT_KW), :], v_ref[r, pl.ds(ks, ATT_KW), :]
            s = lax.dot_general(q, kw, (((1,), (1,)), ((), ())), preferred_element_type=F32)
            s = jnp.where(jnp.abs(rel0 + (ks - l0)) <= BAND_RADIUS, s, MASK_VALUE)
            m = jnp.max(s, axis=-1, keepdims=True)
            p = jnp.exp(s - m)
            den = jnp.sum(p, axis=-1, keepdims=True)
            o = jnp.dot(p.astype(BF16), vw, preferred_element_type=F32) * (1.0 / den)
            lse = jnp.broadcast_to(m + jnp.log(den), (ATT_QB, HEAD_DIM))
            if dil == 1:
                dst = qrows
            else:
                dst = pl.ds(j * (ATT_QB * dil) + r, ATT_QB, stride=dil)
            o_scr[bi, dst, :] = o
            l_scr[bi, dst, :] = lse
            return carry

        lax.fori_loop(0, dil * nj, sub, 0, unroll=8)

    def merge(j, carry):
        rows = pl.ds(pl.multiple_of(j * ATT_QB, ATT_QB), ATT_QB)
        l0, l1, l2 = l_scr[0, rows, :], l_scr[1, rows, :], l_scr[2, rows, :]
        mx = jnp.maximum(jnp.maximum(l0, l1), l2)
        w0, w1, w2 = jnp.exp(l0 - mx), jnp.exp(l1 - mx), jnp.exp(l2 - mx)
        y = (o_scr[0, rows, :] * w0 + o_scr[1, rows, :] * w1 + o_scr[2, rows, :] * w2) * (1.0 / (w0 + w1 + w2))
        y_ref[rows, :] = y.astype(y_ref.dtype)
        return carry

    lax.fori_loop(0, nb // ATT_QB, merge, 0)


def _attention(q1, k1, v1, q4, k4, v4, q16, k16, v16):
    b, s, aw = q1.shape
    nb = min(s, ATT_NB)
    assert s % nb == 0 and s // 16 >= ATT_KW and nb % (16 * ATT_QB) == 0
    nat_q = pl.BlockSpec((None, nb, HEAD_DIM), lambda i, h, j: (i, j, h))
    nat_kv = pl.BlockSpec((None, s, HEAD_DIM), lambda i, h, j: (i, 0, h))
    grp_q = lambda dil: pl.BlockSpec((None, dil, nb // dil, HEAD_DIM), lambda i, h, j: (i, 0, j, h))
    grp_kv = lambda dil: pl.BlockSpec((None, dil, s // dil, HEAD_DIM), lambda i, h, j: (i, 0, 0, h))
    return pl.pallas_call(
        functools.partial(_attn_kernel, seq=s),
        grid=(b, N_HEADS, s // nb),
        in_specs=[nat_q, nat_kv, nat_kv, grp_q(4), grp_kv(4), grp_kv(4), grp_q(16), grp_kv(16), grp_kv(16)],
        out_specs=pl.BlockSpec((None, nb, HEAD_DIM), lambda i, h, j: (i, j, h)),
        out_shape=jax.ShapeDtypeStruct((b, s, aw), BF16),
        scratch_shapes=[pltpu.VMEM((len(DILATIONS), nb, HEAD_DIM), F32),
                        pltpu.VMEM((len(DILATIONS), nb, HEAD_DIM), F32)],
        compiler_params=_cparams("arbitrary", "arbitrary", "arbitrary"),
    )(q1, k1, v1, q4, k4, v4, q16, k16, v16)


def _lru_kernel(xf_ref, xfb_ref, xfa_ref, xr_ref, xrb_ref, xra_ref, cw_ref, cb_ref, wr_ref, wi_ref,
                br_ref, bi_ref, lam_ref, hf_ref, hr_ref, a_scr, b_scr, carry_scr):
    si = pl.program_id(1)
    ns = pl.num_programs(1)
    ts, w = xf_ref.shape
    ngroups = ts // SUBLANES

    @pl.when(si == 0)
    def _():
        carry_scr[...] = jnp.zeros_like(carry_scr)

    sub = lax.broadcasted_iota(jnp.int32, (ngroups, SUBLANES, LRU_BLOCK), 1)
    cw = cw_ref[...]
    cb = cb_ref[...]

    def prepare(dirn, main_ref, before_ref, after_ref, tile):
        main = main_ref[...]
        before = jnp.where(tile == 0, 0.0, before_ref[...])
        after = jnp.where(tile == ns - 1, 0.0, after_ref[...])
        rowf = lax.broadcasted_iota(jnp.int32, (ts, w), 0)
        xm1 = jnp.where(rowf == 0, before[7:8], pltpu.roll(main, 1, axis=0))
        xm2 = jnp.where(rowf == 0, before[6:7],
                        jnp.where(rowf == 1, before[7:8], pltpu.roll(main, 2, axis=0)))
        xp1 = jnp.where(rowf == ts - 1, after[0:1], pltpu.roll(main, ts - 1, axis=0))
        xc = cb + xm2 * cw[0:1] + xm1 * cw[1:2] + main * cw[2:3] + xp1 * cw[3:4]
        xcb = xc.astype(BF16)
        lam = lam_ref[dirn:dirn + 1, :]
        sp = jnp.maximum(-lam, 0.0) + jnp.log1p(jnp.exp(-jnp.abs(lam)))
        for n in range(w // LRU_BLOCK):
            cols = slice(n * LRU_BLOCK, (n + 1) * LRU_BLOCK)
            blk = xcb[:, cols]
            r = jax.nn.sigmoid(jnp.dot(blk, wr_ref[dirn, n], preferred_element_type=F32)
                               + br_ref[dirn:dirn + 1, cols])
            gi = jax.nn.sigmoid(jnp.dot(blk, wi_ref[dirn, n], preferred_element_type=F32)
                                + bi_ref[dirn:dirn + 1, cols])
            a = jnp.exp(-LRU_C * r * sp[:, cols])
            bt = jnp.sqrt(jnp.maximum(1.0 - a * a, 0.0)) * (gi * xc[:, cols])
            a = a.reshape(ngroups, SUBLANES, LRU_BLOCK)
            bt = bt.reshape(ngroups, SUBLANES, LRU_BLOCK)
            for shift in (1, 2, 4):
                if dirn == 0:
                    a_sh = pltpu.roll(a, shift, axis=1)
                    b_sh = pltpu.roll(bt, shift, axis=1)
                    ok = sub >= shift
                else:
                    a_sh = pltpu.roll(a, SUBLANES - shift, axis=1)
                    b_sh = pltpu.roll(bt, SUBLANES - shift, axis=1)
                    ok = sub < SUBLANES - shift
                bt = jnp.where(ok, a * b_sh + bt, bt)
                a = jnp.where(ok, a * a_sh, a)
            a_scr[dirn, :, cols] = a.reshape(ts, LRU_BLOCK)
            b_scr[dirn, :, cols] = bt.reshape(ts, LRU_BLOCK)

    prepare(0, xf_ref, xfb_ref, xfa_ref, si)
    prepare(1, xr_ref, xrb_ref, xra_ref, ns - 1 - si)

    def body(g, carry):
        cf, cr = carry
        rf = pl.ds(pl.multiple_of(g * SUBLANES, SUBLANES), SUBLANES)
        hf = b_scr[0, rf, :] + a_scr[0, rf, :] * cf
        hf_ref[rf, :] = hf
        cf = jnp.broadcast_to(hf[SUBLANES - 1:SUBLANES, :], (SUBLANES, w))
        rr = pl.ds(pl.multiple_of((ngroups - 1 - g) * SUBLANES, SUBLANES), SUBLANES)
        hr = b_scr[1, rr, :] + a_scr[1, rr, :] * cr
        hr_ref[rr, :] = hr
        cr = jnp.broadcast_to(hr[0:1, :], (SUBLANES, w))
        return cf, cr

    cf, cr = lax.fori_loop(0, ngroups, body, (carry_scr[0], carry_scr[1]))
    carry_scr[0] = cf
    carry_scr[1] = cr


def _lru_call(xb, conv_w, conv_b, wr_bf, wi_bf, b_r, b_i, lam):
    b, s, w = xb.shape
    ts = 256
    ns = s // ts
    hb = ts // SUBLANES
    nh = s // SUBLANES
    fwd = lambda i, j: (i, j, 0)
    rev = lambda i, j: (i, ns - 1 - j, 0)
    fwd_before = lambda i, j: (i, jnp.maximum(j * hb - 1, 0), 0)
    fwd_after = lambda i, j: (i, jnp.minimum((j + 1) * hb, nh - 1), 0)
    rev_before = lambda i, j: (i, jnp.maximum((ns - 1 - j) * hb - 1, 0), 0)
    rev_after = lambda i, j: (i, jnp.minimum((ns - j) * hb, nh - 1), 0)
    main = lambda im: pl.BlockSpec((None, ts, w), im)
    halo = lambda im: pl.BlockSpec((None, SUBLANES, w), im)
    return pl.pallas_call(
        _lru_kernel,
        grid=(b, ns),
        in_specs=[main(fwd), halo(fwd_before), halo(fwd_after),
                  main(rev), halo(rev_before), halo(rev_after),
                  _const_spec(conv_w.shape), _const_spec(conv_b.shape),
                  _const_spec(wr_bf.shape), _const_spec(wi_bf.shape),
                  _const_spec(b_r.shape), _const_spec(b_i.shape), _const_spec(lam.shape)],
        out_specs=[main(fwd), main(rev)],
        out_shape=[jax.ShapeDtypeStruct((b, s, w), F32)] * 2,
        scratch_shapes=[pltpu.VMEM((2, ts, w), F32), pltpu.VMEM((2, ts, w), F32),
                        pltpu.VMEM((2, SUBLANES, w), F32)],
        compiler_params=_cparams("arbitrary", "arbitrary"),
    )(xb, xb, xb, xb, xb, xb, conv_w, conv_b, wr_bf, wi_bf, b_r, b_i, lam)


def _outproj_kernel(att_ref, hf_ref, hr_ref, gate_ref, xa_ref, xb2_ref, mod_ref, ga_ref, gl_ref, w_ref, g2_ref,
                    x1_ref, h2t_ref, *, n_first):
    modv = mod_ref[...]
    g1, sh2, sc2 = modv[2:3], modv[3:4], modv[4:5]
    an = _rms(att_ref[...].astype(F32), ga_ref[...])
    lru = (hf_ref[...] + hr_ref[...]) * _gelu(gate_ref[...])
    ln = _rms(lru, gl_ref[...])
    aw = an.shape[-1]
    mix = (jnp.dot(an.astype(BF16), w_ref[0:aw, :], preferred_element_type=F32)
           + jnp.dot(ln.astype(BF16), w_ref[aw:, :], preferred_element_type=F32))
    x1 = jnp.where(pl.program_id(0) < n_first, xa_ref[...], xb2_ref[...]) + g1 * mix
    x1_ref[...] = x1
    h2 = _rms(x1, g2_ref[...]) * (1.0 + sc2) + sh2
    h2t_ref[...] = h2.T.astype(BF16)


def _outproj_call(att, hf, hr, gate, xa, xb2, mod3, ga, gl, w_out_bf, g2):
    n_first, s, d = xa.shape
    b = n_first + xb2.shape[0]
    aw, lw = att.shape[-1], hf.shape[-1]
    tm = 256
    nt = s // tm
    tok = lambda i, j: (i, j, 0)
    return pl.pallas_call(
        functools.partial(_outproj_kernel, n_first=n_first),
        grid=(b, nt),
        in_specs=[pl.BlockSpec((None, tm, aw), tok), pl.BlockSpec((None, tm, lw), tok),
                  pl.BlockSpec((None, tm, lw), tok), pl.BlockSpec((None, tm, lw), tok),
                  *_two_group_specs((None, tm, d), n_first, nt),
                  pl.BlockSpec((None, N_MOD, d), lambda i, j: (i, 0, 0)),
                  _const_spec((1, aw)), _const_spec((1, lw)), _const_spec(w_out_bf.shape),
                  _const_spec((1, d))],
        out_specs=[pl.BlockSpec((None, tm, d), tok),
                   pl.BlockSpec((d, tm), lambda i, j: (0, i * nt + j))],
        out_shape=[jax.ShapeDtypeStruct((b, s, d), F32), jax.ShapeDtypeStruct((d, b * s), BF16)],
        compiler_params=_cparams("arbitrary", "arbitrary"),
    )(att, hf, hr, gate, xa, xb2, mod3, ga, gl, w_out_bf, g2)


def _top16_rows(s, want_rank):
    t = s.shape[1]
    rowi = lax.broadcasted_iota(jnp.int32, (PEER_TOPK, t), 0)
    out = jnp.full((PEER_TOPK, t), NEG_INF, F32)
    rank = jnp.full(s.shape, float(PEER_TOPK), F32) if want_rank else None
    cur = s
    for kk in range(PEER_TOPK):
        m = jnp.max(cur, axis=0, keepdims=True)
        out = jnp.where(rowi == kk, m, out)
        hit = cur == m
        if want_rank:
            rank = jnp.where(hit, float(kk), rank)
        cur = jnp.where(hit, NEG_INF, cur)
    return out, rank


def _product_key_select(s1, s2):
    t = s1.shape[1]
    a16, _ = _top16_rows(s1, False)
    b16, rank2 = _top16_rows(s2, True)
    row8 = lax.broadcasted_iota(jnp.int32, (SUBLANES, t), 0)
    a_lo, a_hi, b_lo, b_hi = a16[0:8], a16[8:16], b16[0:8], b16[8:16]
    arow = lambda i: a16[i:i + 1, :]
    brow = lambda j: b16[j:j + 1, :]
    pieces = [
        arow(0) + b_lo,
        arow(0) + b_hi,
        arow(1) + b_lo,
        a_hi + brow(0),
        jnp.where(row8 < 4, arow(3) + b_lo, a_lo + brow(0)),
        jnp.where(row8 < 4, arow(2) + b_lo, a_lo + brow(1)),
        jnp.where(row8 == 4, arow(2) + b_lo,
                  jnp.where(row8 == 2, arow(4) + b_lo, NEG_INF)),
    ]
    top = a16[0:1, :] + b16[0:1, :]
    z = jnp.zeros((1, t), F32)
    tau = top
    for kk in range(PEER_TOPK):
        m = pieces[0]
        for pc in pieces[1:]:
            m = jnp.maximum(m, pc)
        m = jnp.max(m, axis=0, keepdims=True)
        z = z + jnp.exp(m - top)
        tau = m
        pieces = [jnp.where(pc == m, NEG_INF, pc) for pc in pieces]
    cnt = jnp.zeros((PEER_TOPK, t), F32)
    for l in range(PEER_TOPK):
        cnt = cnt + jnp.where(a16 + b16[l:l + 1, :] >= tau, 1.0, 0.0)
    n1 = jnp.zeros(s1.shape, F32)
    for r in range(PEER_TOPK):
        n1 = jnp.where(s1 == a16[r:r + 1, :], cnt[r:r + 1, :], n1)
    return n1, jnp.exp(s1 - a16[0:1, :]) * (1.0 / z), rank2, jnp.exp(s2 - b16[0:1, :])


def _peerq_kernel(h2t_ref, wq_ref, sk_ref, n_ref, a_ref, cj_ref, bj_ref, qt_scr):
    tq = h2t_ref.shape[1]
    hrows = 2 * PEER_KEYS

    def project(h):
        rows = pl.ds(pl.multiple_of(h * hrows, hrows), hrows)
        qt_scr[rows, :] = jnp.dot(wq_ref[rows, :], h2t_ref[...], preferred_element_type=F32).astype(BF16)

    project(0)

    def head(h, carry):
        r1 = pl.ds(pl.multiple_of(2 * h * PEER_KEYS, PEER_KEYS), PEER_KEYS)
        r2 = pl.ds(pl.multiple_of((2 * h + 1) * PEER_KEYS, PEER_KEYS), PEER_KEYS)
        s1 = jnp.dot(sk_ref[2 * h], qt_scr[r1, :], preferred_element_type=F32)
        s2 = jnp.dot(sk_ref[2 * h + 1], qt_scr[r2, :], preferred_element_type=F32)
        project(jnp.minimum(h + 1, PEER_HEADS - 1))
        for st in range(tq // LANES):
            cols = slice(st * LANES, (st + 1) * LANES)
            n1, a, rank2, b = _product_key_select(s1[:, cols], s2[:, cols])
            n_ref[h, :, cols] = n1
            a_ref[h, :, cols] = a
            cj_ref[h, :, cols] = rank2.astype(BF16)
            bj_ref[h, :, cols] = b.astype(BF16)
        return carry

    lax.fori_loop(0, PEER_HEADS, head, 0)


def _peerq_call(h2t, wq_t_bf, sk_bf):
    d, t = h2t.shape
    tq = 512
    big = lambda: pl.BlockSpec((PEER_HEADS, PEER_KEYS, tq), lambda i: (0, 0, i))
    shp = lambda dt: jax.ShapeDtypeStruct((PEER_HEADS, PEER_KEYS, t), dt)
    return pl.pallas_call(
        _peerq_kernel,
        grid=(t // tq,),
        in_specs=[pl.BlockSpec((d, tq), lambda i: (0, i)),
                  _const_spec(wq_t_bf.shape), _const_spec(sk_bf.shape)],
        out_specs=[big(), big(), big(), big()],
        out_shape=[shp(F32), shp(F32), shp(BF16), shp(BF16)],
        scratch_shapes=[pltpu.VMEM((wq_t_bf.shape[0], tq), BF16)],
        compiler_params=_cparams("arbitrary"),
    )(h2t, wq_t_bf, sk_bf)


PEER_STRIP = 256


PEER_VALUE_K = 256


def _peer_kernel(h2t_ref, u_ref, vt_ref, n_ref, a_ref, cj_ref, bj_ref, acc_ref, s_even, s_odd, c_scr):
    c = pl.program_id(1)
    last = pl.num_programs(1) - 1
    ce, tm = s_even.shape

    def step(s_write, s_read, do_score=True, do_finish=True):
        per_slice = PEER_VALUE_K // PEER_KEYS
        half = tm // 2

        def weigh(ks):
            for il in range(ks * per_slice, (ks + 1) * per_slice):
                rows = slice(il * PEER_KEYS, (il + 1) * PEER_KEYS)
                for si in range(tm // PEER_STRIP):
                    cols = slice(si * PEER_STRIP, (si + 1) * PEER_STRIP)
                    w = jnp.zeros((PEER_KEYS, PEER_STRIP), BF16)
                    for h in range(PEER_HEADS):
                        nrow = n_ref[h, il:il + 1, cols].astype(BF16)
                        arow = a_ref[h, il:il + 1, cols].astype(BF16)
                        sel = cj_ref[h, :, cols] < nrow
                        w = w + jnp.where(sel, bj_ref[h, :, cols], jnp.zeros((), BF16)) * arow
                    act = _gelu(s_read[rows, cols]).astype(BF16)
                    c_scr[rows, cols] = w * act

        def value(ks):
            kk = slice(ks * PEER_VALUE_K, (ks + 1) * PEER_VALUE_K)
            acc_ref[...] += jnp.dot(vt_ref[:, kk], c_scr[kk, :], preferred_element_type=F32)

        def score(piece):
            cols = slice(piece * half, (piece + 1) * half)
            s_write[:, cols] = jnp.dot(u_ref[...], h2t_ref[:, cols], preferred_element_type=F32)

        order = (("weigh", 0), ("score", 0), ("weigh", 1), ("value", 0), ("value", 1),
                 ("weigh", 2), ("score", 1), ("weigh", 3), ("value", 2), ("value", 3))
        for kind, idx in order:
            if kind == "score":
                if do_score:
                    score(idx)
            elif do_finish:
                (weigh if kind == "weigh" else value)(idx)

    @pl.when(c == 0)
    def _():
        acc_ref[...] = jnp.zeros_like(acc_ref)
        step(s_even, s_odd, do_finish=False)

    @pl.when((c > 0) & (c < last) & (c % 2 == 0))
    def _():
        step(s_even, s_odd)

    @pl.when((c > 0) & (c < last) & (c % 2 == 1))
    def _():
        step(s_odd, s_even)

    @pl.when((c == last) & (c % 2 == 0))
    def _():
        step(s_even, s_odd, do_score=False)

    @pl.when((c == last) & (c % 2 == 1))
    def _():
        step(s_odd, s_even, do_score=False)


def _peer_call(h2t, u_bf, vt_bf, nt, at, cjt, bjt):
    d, t = h2t.shape
    ne = u_bf.shape[0]
    tm = 512
    ce = 1024
    nc = ne // ce
    nsub = ce // PEER_KEYS
    prev = lambda c: jnp.maximum(c - 1, 0)
    sub = lambda: pl.BlockSpec((PEER_HEADS, nsub, tm), lambda i, c: (0, prev(c), i))
    allk = lambda: pl.BlockSpec((PEER_HEADS, PEER_KEYS, tm), lambda i, c: (0, 0, i))
    return pl.pallas_call(
        _peer_kernel,
        grid=(t // tm, nc + 1),
        in_specs=[pl.BlockSpec((d, tm), lambda i, c: (0, i)),
                  pl.BlockSpec((ce, d), lambda i, c: (jnp.minimum(c, nc - 1), 0)),
                  pl.BlockSpec((d, ce), lambda i, c: (0, prev(c))),
                  sub(), sub(), allk(), allk()],
        out_specs=pl.BlockSpec((d, tm), lambda i, c: (0, i)),
        out_shape=jax.ShapeDtypeStruct((d, t), F32),
        scratch_shapes=[pltpu.VMEM((ce, tm), F32), pltpu.VMEM((ce, tm), F32), pltpu.VMEM((ce, tm), BF16)],
        compiler_params=_cparams("arbitrary", "arbitrary"),
    )(h2t, u_bf, vt_bf, nt, at, cjt, bjt)


def _final_kernel(pt_ref, x1_ref, mod_ref, gf_ref, ya_ref, yb_ref, *, n_first):
    g2 = mod_ref[...][5:6]
    x2 = x1_ref[...] + g2 * pt_ref[...].T
    y = _rms(x2, gf_ref[...])

    @pl.when(pl.program_id(0) < n_first)
    def _():
        ya_ref[...] = y

    @pl.when(pl.program_id(0) >= n_first)
    def _():
        yb_ref[...] = y


def _final_call(peer_t, x1, mod3, gf, n_first):
    b, s, d = x1.shape
    tm = 256
    nt = s // tm
    tok = lambda i, j: (i, j, 0)
    return pl.pallas_call(
        functools.partial(_final_kernel, n_first=n_first),
        grid=(b, nt),
        in_specs=[pl.BlockSpec((d, tm), lambda i, j: (0, i * nt + j)),
                  pl.BlockSpec((None, tm, d), tok),
                  pl.BlockSpec((None, N_MOD, d), lambda i, j: (i, 0, 0)),
                  _const_spec((1, d))],
        out_specs=list(_two_group_specs((None, tm, d), n_first, nt)),
        out_shape=[jax.ShapeDtypeStruct((n_first, s, d), F32), jax.ShapeDtypeStruct((b - n_first, s, d), F32)],
        compiler_params=_cparams("arbitrary", "arbitrary"),
    )(peer_t, x1, mod3, gf)


def _layer(xa, xb2, c, w_mod, b_mod, norm1_g, w_in, conv_w, conv_b, lru_w_r, lru_b_r, lru_w_i, lru_b_i,
           lru_lambda, attn_out_g, lru_out_g, w_out, norm2_g, peer_w_q, peer_sub_keys, peer_u, peer_v,
           norm_final_g):
    n_first, s, d = xa.shape
    b = n_first + xb2.shape[0]
    c8 = jnp.pad(c, ((0, (-b) % SUBLANES), (0, 0)))
    mod = _mod_call(c8, w_mod, b_mod[None, :])
    mod3 = mod.reshape(c8.shape[0], N_MOD, d)[:b]
    cosf, sinf = _rope_call(s)
    *qkv, xb, gate = _inproj_call(xa, xb2, mod3, norm1_g[None, :], w_in.astype(BF16), cosf, sinf)
    att = _attention(*qkv)
    hf, hr = _lru_call(xb, conv_w, conv_b[None, :], lru_w_r.astype(BF16), lru_w_i.astype(BF16),
                       lru_b_r, lru_b_i, lru_lambda)
    x1, h2t = _outproj_call(att, hf, hr, gate, xa, xb2, mod3, attn_out_g[None, :], lru_out_g[None, :],
                            w_out.astype(BF16), norm2_g[None, :])
    sk = peer_sub_keys.reshape(PEER_HEADS * 2, PEER_KEYS, -1).astype(BF16)
    nt, at, cjt, bjt = _peerq_call(h2t, peer_w_q.T.astype(BF16), sk)
    peer_t = _peer_call(h2t, peer_u.astype(BF16), peer_v.T.astype(BF16), nt, at, cjt, bjt)
    return _final_call(peer_t, x1, mod3, norm_final_g[None, :], n_first)


def kernel(x_prompt, x_sample, c_prompt, c_sample, w_mod, b_mod, norm1_g, w_in, conv_w, conv_b, lru_w_r,
           lru_b_r, lru_w_i, lru_b_i, lru_lambda, attn_out_g, lru_out_g, w_out, norm2_g, peer_w_q,
           peer_sub_keys, peer_u, peer_v, norm_final_g):
    assert w_mod.shape[0] == 1, "single-layer stack expected"
    assert x_prompt.shape[1:] == x_sample.shape[1:]
    c = jnp.concatenate([c_prompt, c_sample], axis=0)
    y_prompt, y_sample = _layer(
        x_prompt, x_sample, c, w_mod[0], b_mod[0], norm1_g[0], w_in[0], conv_w[0], conv_b[0], lru_w_r[0],
        lru_b_r[0], lru_w_i[0], lru_b_i[0], lru_lambda[0], attn_out_g[0], lru_out_g[0], w_out[0], norm2_g[0],
        peer_w_q[0], peer_sub_keys[0], peer_u[0], peer_v[0], norm_final_g)
    return (y_prompt, y_sample)
```

```python
import functools
import math

import jax
import jax.numpy as jnp
from jax import lax
from jax.experimental import pallas as pl
from jax.experimental.pallas import tpu as pltpu

F32 = jnp.float32
BF16 = jnp.bfloat16

HEAD_DIM = 128
N_HEADS = 8
ATTN_WIDTH = N_HEADS * HEAD_DIM
LRU_BLOCK = 128
DILATIONS = (16, 4, 1)
BAND_RADIUS = 64
ROPE_THETA = 10000.0
LRU_C = 8.0
PEER_HEADS = 8
PEER_KEYS = 128
PEER_TOPK = 16
N_MOD = 6
EPS = 1e-6
MASK_VALUE = -1e30
NEG_INF = float("-inf")

V7X_VMEM_BYTES = 64 * 1024 * 1024
VMEM_LIMIT = V7X_VMEM_BYTES - 8 * 1024 * 1024
LANES = 128
SUBLANES = 8


def _cparams(*sem):
    return pltpu.CompilerParams(dimension_semantics=sem, vmem_limit_bytes=VMEM_LIMIT)


def _const_spec(shape):
    nd = len(shape)
    return pl.BlockSpec(shape, lambda *_: (0,) * nd, pipeline_mode=pl.Buffered(1))


def _two_group_specs(block, n_first, n_tiles):
    first = pl.BlockSpec(block, lambda i, j: (jnp.minimum(i, n_first - 1), jnp.where(i < n_first, j, n_tiles - 1), 0))
    second = pl.BlockSpec(block, lambda i, j: (jnp.maximum(i - n_first, 0), jnp.where(i < n_first, 0, j), 0))
    return first, second


def _rms(x, g):
    return x * lax.rsqrt(jnp.mean(x * x, axis=-1, keepdims=True) + EPS) * g


def _gelu(x):
    alpha = -2.0 * math.sqrt(2.0 / math.pi) * math.log2(math.e)
    e = jnp.exp2(x * (alpha + (alpha * 0.044715) * (x * x)))
    return x * (1.0 / (1.0 + e))


def _mod_kernel(c_ref, w_ref, b_ref, o_ref):
    c = c_ref[...]
    sc = c * jax.nn.sigmoid(c)
    o_ref[...] = jnp.dot(sc, w_ref[...], precision=lax.Precision.HIGHEST,
                         preferred_element_type=F32) + b_ref[...]


def _mod_call(c8, w_mod, b_mod):
    rows, d = c8.shape
    n = w_mod.shape[1]
    tn = 1024
    return pl.pallas_call(
        _mod_kernel,
        grid=(n // tn,),
        in_specs=[pl.BlockSpec((rows, d), lambda j: (0, 0)),
                  pl.BlockSpec((d, tn), lambda j: (0, j)),
                  pl.BlockSpec((1, tn), lambda j: (0, j))],
        out_specs=pl.BlockSpec((rows, tn), lambda j: (0, j)),
        out_shape=jax.ShapeDtypeStruct((rows, n), F32),
        compiler_params=_cparams("arbitrary"),
    )(c8, w_mod, b_mod)


def _rope_kernel(inv_ref, cos_ref, sin_ref):
    ts = cos_ref.shape[0]
    pos = (pl.program_id(0) * ts + lax.broadcasted_iota(jnp.int32, (ts, HEAD_DIM), 0)).astype(F32)
    lane = lax.broadcasted_iota(jnp.int32, (ts, HEAD_DIM), 1)
    ang = pos * inv_ref[...]
    cos_ref[...] = jnp.cos(ang)
    sn = jnp.sin(ang)
    sin_ref[...] = jnp.where(lane < HEAD_DIM // 2, -sn, sn)


def _rope_call(seq):
    half = HEAD_DIM // 2
    inv = ROPE_THETA ** (-jnp.arange(half, dtype=F32) / half)
    inv2 = jnp.concatenate([inv, inv])[None, :]
    ts = min(seq, 1024)
    return pl.pallas_call(
        _rope_kernel,
        grid=(seq // ts,),
        in_specs=[pl.BlockSpec((1, HEAD_DIM), lambda i: (0, 0))],
        out_specs=[pl.BlockSpec((ts, HEAD_DIM), lambda i: (i, 0))] * 2,
        out_shape=[jax.ShapeDtypeStruct((seq, HEAD_DIM), F32)] * 2,
        compiler_params=_cparams("arbitrary"),
    )(inv2)


def _inproj_kernel(xa_ref, xb2_ref, mod_ref, g_ref, w_ref, cos_ref, sin_ref, p4_ref, p16_ref,
                   q1_ref, k1_ref, v1_ref, q4_ref, k4_ref, v4_ref, q16_ref, k16_ref, v16_ref,
                   xb_ref, gate_ref, *, n_first):
    x = jnp.where(pl.program_id(0) < n_first, xa_ref[...], xb2_ref[...])
    modv = mod_ref[...]
    sh1, sc1 = modv[0:1], modv[1:2]
    h = _rms(x, g_ref[...]) * (1.0 + sc1) + sh1
    hb = h.astype(BF16)
    cosv = cos_ref[...]
    sinv = sin_ref[...]
    aw = ATTN_WIDTH

    def rope(z, scale):
        parts = []
        for hd in range(N_HEADS):
            zs = z[:, hd * HEAD_DIM:(hd + 1) * HEAD_DIM]
            rot = pltpu.roll(zs, HEAD_DIM // 2, axis=1)
            r = zs * cosv + rot * sinv
            if scale != 1.0:
                r = r * scale
            parts.append(r.astype(BF16))
        return jnp.concatenate(parts, axis=1)

    def store_layouts(zb, o1_ref, o4_ref, o16_ref):
        o1_ref[...] = zb
        for p_ref, o_ref in ((p4_ref, o4_ref), (p16_ref, o16_ref)):
            perm = jnp.dot(p_ref[...], zb, preferred_element_type=F32).astype(BF16)
            o_ref[...] = perm.reshape(o_ref.shape)

    store_layouts(rope(jnp.dot(hb, w_ref[:, 0:aw], preferred_element_type=F32), HEAD_DIM ** -0.5 * math.log2(math.e)),
                  q1_ref, q4_ref, q16_ref)
    store_layouts(rope(jnp.dot(hb, w_ref[:, aw:2 * aw], preferred_element_type=F32), 1.0),
                  k1_ref, k4_ref, k16_ref)
    store_layouts(jnp.dot(hb, w_ref[:, 2 * aw:3 * aw], preferred_element_type=F32).astype(BF16),
                  v1_ref, v4_ref, v16_ref)
    lw = xb_ref.shape[-1]
    xb_ref[...] = jnp.dot(hb, w_ref[:, 3 * aw:3 * aw + lw], preferred_element_type=F32)
    gate_ref[...] = jnp.dot(hb, w_ref[:, 3 * aw + lw:3 * aw + 2 * lw], preferred_element_type=F32)


def _residue_perm(tm, dil):
    src = jnp.arange(tm)
    dst = (src % dil) * (tm // dil) + src // dil
    return jnp.zeros((tm, tm), BF16).at[dst, src].set(1.0)


def _inproj_call(xa, xb2, mod3, g1, w_in_bf, cosf, sinf):
    n_first, s, d = xa.shape
    b = n_first + xb2.shape[0]
    aw = ATTN_WIDTH
    lw = (w_in_bf.shape[1] - 3 * aw) // 2
    tm = 256
    tok = lambda i, j: (i, j, 0)
    grouped = lambda i, j: (i, 0, j, 0)
    lay1 = pl.BlockSpec((None, tm, aw), tok)
    lay4 = pl.BlockSpec((None, 4, tm // 4, aw), grouped)
    lay16 = pl.BlockSpec((None, 16, tm // 16, aw), grouped)
    shp1 = jax.ShapeDtypeStruct((b, s, aw), BF16)
    shp4 = jax.ShapeDtypeStruct((b, 4, s // 4, aw), BF16)
    shp16 = jax.ShapeDtypeStruct((b, 16, s // 16, aw), BF16)
    return pl.pallas_call(
        functools.partial(_inproj_kernel, n_first=n_first),
        grid=(b, s // tm),
        in_specs=[*_two_group_specs((None, tm, d), n_first, s // tm),
                  pl.BlockSpec((None, N_MOD, d), lambda i, j: (i, 0, 0)),
                  _const_spec((1, d)),
                  _const_spec(w_in_bf.shape),
                  pl.BlockSpec((tm, HEAD_DIM), lambda i, j: (j, 0)),
                  pl.BlockSpec((tm, HEAD_DIM), lambda i, j: (j, 0)),
                  _const_spec((tm, tm)), _const_spec((tm, tm))],
        out_specs=[lay1] * 3 + [lay4] * 3 + [lay16] * 3 + [pl.BlockSpec((None, tm, lw), tok)] * 2,
        out_shape=[shp1] * 3 + [shp4] * 3 + [shp16] * 3 + [jax.ShapeDtypeStruct((b, s, lw), F32)] * 2,
        compiler_params=_cparams("arbitrary", "arbitrary"),
    )(xa, xb2, mod3, g1, w_in_bf, cosf, sinf, _residue_perm(tm, 4), _residue_perm(tm, 16))


ATT_QB = 128
ATT_KW = ATT_QB + 2 * BAND_RADIUS


ATT_NB = 2048


def _attn_kernel(q1_ref, k1_ref, v1_ref, q4_ref, k4_ref, v4_ref, q16_ref, k16_ref, v16_ref, y_ref,
                 o_scr, l_scr, *, seq):
    blk = pl.program_id(2)
    nb = y_ref.shape[0]
    rel0 = (lax.broadcasted_iota(jnp.int32, (ATT_QB, ATT_KW), 1)
            - lax.broadcasted_iota(jnp.int32, (ATT_QB, ATT_KW), 0))
    branches = ((q16_ref, k16_ref, v16_ref, 16), (q4_ref, k4_ref, v4_ref, 4), (q1_ref, k1_ref, v1_ref, 1))
    for bi, (q_ref, k_ref, v_ref, dil) in enumerate(branches):
        per = nb // dil
        nj = per // ATT_QB
        sl = seq // dil

        def sub(idx, carry, bi=bi, q_ref=q_ref, k_ref=k_ref, v_ref=v_ref, dil=dil, per=per, nj=nj, sl=sl):
            r = idx // nj
            j = idx % nj
            l0 = blk * per + j * ATT_QB
            ks = pl.multiple_of(jnp.clip(l0 - BAND_RADIUS, 0, sl - ATT_KW), BAND_RADIUS)
            qrows = pl.ds(pl.multiple_of(j * ATT_QB, ATT_QB), ATT_QB)
            if dil == 1:
                q, kw, vw = q_ref[qrows, :], k_ref[pl.ds(ks, ATT_KW), :], v_ref[pl.ds(ks, ATT_KW), :]
            else:
                q, kw, vw = q_ref[r, qrows, :], k_ref[r, pl.ds(ks, ATT_KW), :], v_ref[r, pl.ds(ks, ATT_KW), :]
            s = lax.dot_general(q, kw, (((1,), (1,)), ((), ())), preferred_element_type=F32)
            shifted = (rel0 + (ks - l0 + BAND_RADIUS)).astype(jnp.uint32)
            s = jnp.where(shifted <= 2 * BAND_RADIUS, s, MASK_VALUE)
            m = jnp.max(s, axis=-1, keepdims=True)
            p = jnp.exp2(s - m)
            den = jnp.sum(p, axis=-1, keepdims=True)
            o = jnp.dot(p.astype(BF16), vw, preferred_element_type=F32) * (1.0 / den)
            lse = jnp.broadcast_to(m + jnp.log2(den), (ATT_QB, HEAD_DIM))
            if dil == 1:
                dst = qrows
            else:
                dst = pl.ds(j * (ATT_QB * dil) + r, ATT_QB, stride=dil)
            o_scr[bi, dst, :] = o
            l_scr[bi, dst, :] = lse
            return carry

        lax.fori_loop(0, dil * nj, sub, 0, unroll=True)

    def merge(j, carry):
        rows = pl.ds(pl.multiple_of(j * ATT_QB, ATT_QB), ATT_QB)
        l0, l1, l2 = l_scr[0, rows, :], l_scr[1, rows, :], l_scr[2, rows, :]
        mx = jnp.maximum(jnp.maximum(l0, l1), l2)
        w0, w1, w2 = jnp.exp2(l0 - mx), jnp.exp2(l1 - mx), jnp.exp2(l2 - mx)
        y = (o_scr[0, rows, :] * w0 + o_scr[1, rows, :] * w1 + o_scr[2, rows, :] * w2) * (1.0 / (w0 + w1 + w2))
        y_ref[rows, :] = y.astype(y_ref.dtype)
        return carry

    lax.fori_loop(0, nb // ATT_QB, merge, 0)


def _attention(q1, k1, v1, q4, k4, v4, q16, k16, v16):
    b, s, aw = q1.shape
    nb = min(s, ATT_NB)
    assert s % nb == 0 and s // 16 >= ATT_KW and nb % (16 * ATT_QB) == 0
    nat_q = pl.BlockSpec((None, nb, HEAD_DIM), lambda i, h, j: (i, j, h))
    nat_kv = pl.BlockSpec((None, s, HEAD_DIM), lambda i, h, j: (i, 0, h))
    grp_q = lambda dil: pl.BlockSpec((None, dil, nb // dil, HEAD_DIM), lambda i, h, j: (i, 0, j, h))
    grp_kv = lambda dil: pl.BlockSpec((None, dil, s // dil, HEAD_DIM), lambda i, h, j: (i, 0, 0, h))
    return pl.pallas_call(
        functools.partial(_attn_kernel, seq=s),
        grid=(b, N_HEADS, s // nb),
        in_specs=[nat_q, nat_kv, nat_kv, grp_q(4), grp_kv(4), grp_kv(4), grp_q(16), grp_kv(16), grp_kv(16)],
        out_specs=pl.BlockSpec((None, nb, HEAD_DIM), lambda i, h, j: (i, j, h)),
        out_shape=jax.ShapeDtypeStruct((b, s, aw), BF16),
        scratch_shapes=[pltpu.VMEM((len(DILATIONS), nb, HEAD_DIM), F32),
                        pltpu.VMEM((len(DILATIONS), nb, HEAD_DIM), F32)],
        compiler_params=_cparams("arbitrary", "arbitrary", "arbitrary"),
    )(q1, k1, v1, q4, k4, v4, q16, k16, v16)


def _lru_kernel(xf_ref, xfb_ref, xfa_ref, xr_ref, xrb_ref, xra_ref, cw_ref, cb_ref, wr_ref, wi_ref,
                br_ref, bi_ref, lam_ref, hf_ref, hr_ref, a_scr, b_scr, carry_scr):
    si = pl.program_id(1)
    ns = pl.num_programs(1)
    ts, w = xf_ref.shape
    ngroups = ts // SUBLANES

    @pl.when(si == 0)
    def _():
        carry_scr[...] = jnp.zeros_like(carry_scr)

    sub = lax.broadcasted_iota(jnp.int32, (ngroups, SUBLANES, LRU_BLOCK), 1)
    cw = cw_ref[...]
    cb = cb_ref[...]

    def prepare(dirn, main_ref, before_ref, after_ref, tile):
        main = main_ref[...]
        before = jnp.where(tile == 0, 0.0, before_ref[...])
        after = jnp.where(tile == ns - 1, 0.0, after_ref[...])
        rowf = lax.broadcasted_iota(jnp.int32, (ts, w), 0)
        xm1 = jnp.where(rowf == 0, before[7:8], pltpu.roll(main, 1, axis=0))
        xm2 = jnp.where(rowf == 0, before[6:7],
                        jnp.where(rowf == 1, before[7:8], pltpu.roll(main, 2, axis=0)))
        xp1 = jnp.where(rowf == ts - 1, after[0:1], pltpu.roll(main, ts - 1, axis=0))
        xc = cb + xm2 * cw[0:1] + xm1 * cw[1:2] + main * cw[2:3] + xp1 * cw[3:4]
        xcb = xc.astype(BF16)
        lam = lam_ref[dirn:dirn + 1, :]
        sp = jnp.maximum(-lam, 0.0) + jnp.log1p(jnp.exp(-jnp.abs(lam)))
        for n in range(w // LRU_BLOCK):
            cols = slice(n * LRU_BLOCK, (n + 1) * LRU_BLOCK)
            blk = xcb[:, cols]
            r = jax.nn.sigmoid(jnp.dot(blk, wr_ref[dirn, n], preferred_element_type=F32)
                               + br_ref[dirn:dirn + 1, cols])
            gi = jax.nn.sigmoid(jnp.dot(blk, wi_ref[dirn, n], preferred_element_type=F32)
                                + bi_ref[dirn:dirn + 1, cols])
            a = jnp.exp(-LRU_C * r * sp[:, cols])
            bt = jnp.sqrt(jnp.maximum(1.0 - a * a, 0.0)) * (gi * xc[:, cols])
            a = a.reshape(ngroups, SUBLANES, LRU_BLOCK)
            bt = bt.reshape(ngroups, SUBLANES, LRU_BLOCK)
            for shift in (1, 2, 4):
                if dirn == 0:
                    a_sh = pltpu.roll(a, shift, axis=1)
                    b_sh = pltpu.roll(bt, shift, axis=1)
                    ok = sub >= shift
                else:
                    a_sh = pltpu.roll(a, SUBLANES - shift, axis=1)
                    b_sh = pltpu.roll(bt, SUBLANES - shift, axis=1)
                    ok = sub < SUBLANES - shift
                bt = jnp.where(ok, a * b_sh + bt, bt)
                a = jnp.where(ok, a * a_sh, a)
            a_scr[dirn, :, cols] = a.reshape(ts, LRU_BLOCK)
            b_scr[dirn, :, cols] = bt.reshape(ts, LRU_BLOCK)

    prepare(0, xf_ref, xfb_ref, xfa_ref, si)
    prepare(1, xr_ref, xrb_ref, xra_ref, ns - 1 - si)

    def body(g, carry):
        cf, cr = carry
        rf = pl.ds(pl.multiple_of(g * SUBLANES, SUBLANES), SUBLANES)
        hf = b_scr[0, rf, :] + a_scr[0, rf, :] * cf
        hf_ref[rf, :] = hf
        cf = jnp.broadcast_to(hf[SUBLANES - 1:SUBLANES, :], (SUBLANES, w))
        rr = pl.ds(pl.multiple_of((ngroups - 1 - g) * SUBLANES, SUBLANES), SUBLANES)
        hr = b_scr[1, rr, :] + a_scr[1, rr, :] * cr
        hr_ref[rr, :] = hr
        cr = jnp.broadcast_to(hr[0:1, :], (SUBLANES, w))
        return cf, cr

    cf, cr = lax.fori_loop(0, ngroups, body, (carry_scr[0], carry_scr[1]))
    carry_scr[0] = cf
    carry_scr[1] = cr


def _lru_call(xb, conv_w, conv_b, wr_bf, wi_bf, b_r, b_i, lam):
    b, s, w = xb.shape
    ts = 256
    ns = s // ts
    hb = ts // SUBLANES
    nh = s // SUBLANES
    fwd = lambda i, j: (i, j, 0)
    rev = lambda i, j: (i, ns - 1 - j, 0)
    fwd_before = lambda i, j: (i, jnp.maximum(j * hb - 1, 0), 0)
    fwd_after = lambda i, j: (i, jnp.minimum((j + 1) * hb, nh - 1), 0)
    rev_before = lambda i, j: (i, jnp.maximum((ns - 1 - j) * hb - 1, 0), 0)
    rev_after = lambda i, j: (i, jnp.minimum((ns - j) * hb, nh - 1), 0)
    main = lambda im: pl.BlockSpec((None, ts, w), im)
    halo = lambda im: pl.BlockSpec((None, SUBLANES, w), im)
    return pl.pallas_call(
        _lru_kernel,
        grid=(b, ns),
        in_specs=[main(fwd), halo(fwd_before), halo(fwd_after),
                  main(rev), halo(rev_before), halo(rev_after),
                  _const_spec(conv_w.shape), _const_spec(conv_b.shape),
                  _const_spec(wr_bf.shape), _const_spec(wi_bf.shape),
                  _const_spec(b_r.shape), _const_spec(b_i.shape), _const_spec(lam.shape)],
        out_specs=[main(fwd), main(rev)],
        out_shape=[jax.ShapeDtypeStruct((b, s, w), F32)] * 2,
        scratch_shapes=[pltpu.VMEM((2, ts, w), F32), pltpu.VMEM((2, ts, w), F32),
                        pltpu.VMEM((2, SUBLANES, w), F32)],
        compiler_params=_cparams("arbitrary", "arbitrary"),
    )(xb, xb, xb, xb, xb, xb, conv_w, conv_b, wr_bf, wi_bf, b_r, b_i, lam)


def _outproj_kernel(att_ref, hf_ref, hr_ref, gate_ref, xa_ref, xb2_ref, mod_ref, ga_ref, gl_ref, w_ref, g2_ref,
                    x1_ref, h2t_ref, *, n_first):
    modv = mod_ref[...]
    g1, sh2, sc2 = modv[2:3], modv[3:4], modv[4:5]
    an = _rms(att_ref[...].astype(F32), ga_ref[...])
    lru = (hf_ref[...] + hr_ref[...]) * _gelu(gate_ref[...])
    ln = _rms(lru, gl_ref[...])
    aw = an.shape[-1]
    mix = (jnp.dot(an.astype(BF16), w_ref[0:aw, :], preferred_element_type=F32)
           + jnp.dot(ln.astype(BF16), w_ref[aw:, :], preferred_element_type=F32))
    x1 = jnp.where(pl.program_id(0) < n_first, xa_ref[...], xb2_ref[...]) + g1 * mix
    x1_ref[...] = x1
    h2 = _rms(x1, g2_ref[...]) * (1.0 + sc2) + sh2
    h2t_ref[...] = h2.T.astype(BF16)


def _outproj_call(att, hf, hr, gate, xa, xb2, mod3, ga, gl, w_out_bf, g2):
    n_first, s, d = xa.shape
    b = n_first + xb2.shape[0]
    aw, lw = att.shape[-1], hf.shape[-1]
    tm = 256
    nt = s // tm
    tok = lambda i, j: (i, j, 0)
    return pl.pallas_call(
        functools.partial(_outproj_kernel, n_first=n_first),
        grid=(b, nt),
        in_specs=[pl.BlockSpec((None, tm, aw), tok), pl.BlockSpec((None, tm, lw), tok),
                  pl.BlockSpec((None, tm, lw), tok), pl.BlockSpec((None, tm, lw), tok),
                  *_two_group_specs((None, tm, d), n_first, nt),
                  pl.BlockSpec((None, N_MOD, d), lambda i, j: (i, 0, 0)),
                  _const_spec((1, aw)), _const_spec((1, lw)), _const_spec(w_out_bf.shape),
                  _const_spec((1, d))],
        out_specs=[pl.BlockSpec((None, tm, d), tok),
                   pl.BlockSpec((d, tm), lambda i, j: (0, i * nt + j))],
        out_shape=[jax.ShapeDtypeStruct((b, s, d), F32), jax.ShapeDtypeStruct((d, b * s), BF16)],
        compiler_params=_cparams("arbitrary", "arbitrary"),
    )(att, hf, hr, gate, xa, xb2, mod3, ga, gl, w_out_bf, g2)


def _top16_rows(s, want_rank):
    t = s.shape[1]
    rowi = lax.broadcasted_iota(jnp.int32, (PEER_TOPK, t), 0)
    out = jnp.full((PEER_TOPK, t), NEG_INF, F32)
    rank = jnp.full(s.shape, float(PEER_TOPK), F32) if want_rank else None
    cur = s
    for kk in range(PEER_TOPK):
        m = jnp.max(cur, axis=0, keepdims=True)
        out = jnp.where(rowi == kk, m, out)
        hit = cur == m
        if want_rank:
            rank = jnp.where(hit, float(kk), rank)
        cur = jnp.where(hit, NEG_INF, cur)
    return out, rank


def _product_key_select(s1, s2):
    t = s1.shape[1]
    a16, _ = _top16_rows(s1, False)
    b16, rank2 = _top16_rows(s2, True)
    row8 = lax.broadcasted_iota(jnp.int32, (SUBLANES, t), 0)
    a_lo, a_hi, b_lo, b_hi = a16[0:8], a16[8:16], b16[0:8], b16[8:16]
    arow = lambda i: a16[i:i + 1, :]
    brow = lambda j: b16[j:j + 1, :]
    pieces = [
        arow(0) + b_lo,
        arow(0) + b_hi,
        arow(1) + b_lo,
        a_hi + brow(0),
        jnp.where(row8 < 4, arow(3) + b_lo, a_lo + brow(0)),
        jnp.where(row8 < 4, arow(2) + b_lo, a_lo + brow(1)),
        jnp.where(row8 == 4, arow(2) + b_lo,
                  jnp.where(row8 == 2, arow(4) + b_lo, NEG_INF)),
    ]
    top = a16[0:1, :] + b16[0:1, :]
    z = jnp.zeros((1, t), F32)
    tau = top
    for kk in range(PEER_TOPK):
        m = pieces[0]
        for pc in pieces[1:]:
            m = jnp.maximum(m, pc)
        m = jnp.max(m, axis=0, keepdims=True)
        z = z + jnp.exp(m - top)
        tau = m
        pieces = [jnp.where(pc == m, NEG_INF, pc) for pc in pieces]
    cnt = jnp.zeros((PEER_TOPK, t), F32)
    for l in range(PEER_TOPK):
        cnt = cnt + jnp.where(a16 + b16[l:l + 1, :] >= tau, 1.0, 0.0)
    n1 = jnp.zeros(s1.shape, F32)
    for r in range(PEER_TOPK):
        n1 = jnp.where(s1 == a16[r:r + 1, :], cnt[r:r + 1, :], n1)
    return n1, jnp.exp(s1 - a16[0:1, :]) * (1.0 / z), rank2, jnp.exp(s2 - b16[0:1, :])


def _peerq_kernel(h2t_ref, wq_ref, sk_ref, n_ref, a_ref, cj_ref, bj_ref, qt_scr):
    tq = h2t_ref.shape[1]
    hrows = 2 * PEER_KEYS

    def project(h):
        rows = pl.ds(pl.multiple_of(h * hrows, hrows), hrows)
        qt_scr[rows, :] = jnp.dot(wq_ref[rows, :], h2t_ref[...], preferred_element_type=F32).astype(BF16)

    project(0)

    def head(h, carry):
        r1 = pl.ds(pl.multiple_of(2 * h * PEER_KEYS, PEER_KEYS), PEER_KEYS)
        r2 = pl.ds(pl.multiple_of((2 * h + 1) * PEER_KEYS, PEER_KEYS), PEER_KEYS)
        s1 = jnp.dot(sk_ref[2 * h], qt_scr[r1, :], preferred_element_type=F32)
        s2 = jnp.dot(sk_ref[2 * h + 1], qt_scr[r2, :], preferred_element_type=F32)
        project(jnp.minimum(h + 1, PEER_HEADS - 1))
        for st in range(tq // LANES):
            cols = slice(st * LANES, (st + 1) * LANES)
            n1, a, rank2, b = _product_key_select(s1[:, cols], s2[:, cols])
            n_ref[h, :, cols] = n1
            a_ref[h, :, cols] = a
            cj_ref[h, :, cols] = rank2.astype(BF16)
            bj_ref[h, :, cols] = b.astype(BF16)
        return carry

    lax.fori_loop(0, PEER_HEADS, head, 0)


def _peerq_call(h2t, wq_t_bf, sk_bf):
    d, t = h2t.shape
    tq = 512
    big = lambda: pl.BlockSpec((PEER_HEADS, PEER_KEYS, tq), lambda i: (0, 0, i))
    shp = lambda dt: jax.ShapeDtypeStruct((PEER_HEADS, PEER_KEYS, t), dt)
    return pl.pallas_call(
        _peerq_kernel,
        grid=(t // tq,),
        in_specs=[pl.BlockSpec((d, tq), lambda i: (0, i)),
                  _const_spec(wq_t_bf.shape), _const_spec(sk_bf.shape)],
        out_specs=[big(), big(), big(), big()],
        out_shape=[shp(F32), shp(F32), shp(BF16), shp(BF16)],
        scratch_shapes=[pltpu.VMEM((wq_t_bf.shape[0], tq), BF16)],
        compiler_params=_cparams("arbitrary"),
    )(h2t, wq_t_bf, sk_bf)


PEER_STRIP = 256


PEER_VALUE_K = 256


def _peer_kernel(h2t_ref, u_ref, vt_ref, n_ref, a_ref, cj_ref, bj_ref, acc_ref, s_even, s_odd, c_scr):
    c = pl.program_id(1)
    last = pl.num_programs(1) - 1
    ce, tm = s_even.shape

    def step(s_write, s_read, do_score=True, do_finish=True):
        per_slice = PEER_VALUE_K // PEER_KEYS
        half = tm // 2

        def weigh(ks):
            for il in range(ks * per_slice, (ks + 1) * per_slice):
                rows = slice(il * PEER_KEYS, (il + 1) * PEER_KEYS)
                for si in range(tm // PEER_STRIP):
                    cols = slice(si * PEER_STRIP, (si + 1) * PEER_STRIP)
                    w = jnp.zeros((PEER_KEYS, PEER_STRIP), BF16)
                    for h in range(PEER_HEADS):
                        nrow = n_ref[h, il:il + 1, cols].astype(BF16)
                        arow = a_ref[h, il:il + 1, cols].astype(BF16)
                        sel = cj_ref[h, :, cols] < nrow
                        w = w + jnp.where(sel, bj_ref[h, :, cols], jnp.zeros((), BF16)) * arow
                    act = _gelu(s_read[rows, cols]).astype(BF16)
                    c_scr[rows, cols] = w * act

        def value(ks):
            kk = slice(ks * PEER_VALUE_K, (ks + 1) * PEER_VALUE_K)
            acc_ref[...] += jnp.dot(vt_ref[:, kk], c_scr[kk, :], preferred_element_type=F32)

        def score(piece):
            cols = slice(piece * half, (piece + 1) * half)
            s_write[:, cols] = jnp.dot(u_ref[...], h2t_ref[:, cols], preferred_element_type=F32)

        order = (("weigh", 0), ("score", 0), ("weigh", 1), ("value", 0), ("value", 1),
                 ("weigh", 2), ("score", 1), ("weigh", 3), ("value", 2), ("value", 3))
        for kind, idx in order:
            if kind == "score":
                if do_score:
                    score(idx)
            elif do_finish:
                (weigh if kind == "weigh" else value)(idx)

    @pl.when(c == 0)
    def _():
        acc_ref[...] = jnp.zeros_like(acc_ref)
        step(s_even, s_odd, do_finish=False)

    @pl.when((c > 0) & (c < last) & (c % 2 == 0))
    def _():
        step(s_even, s_odd)

    @pl.when((c > 0) & (c < last) & (c % 2 == 1))
    def _():
        step(s_odd, s_even)

    @pl.when((c == last) & (c % 2 == 0))
    def _():
        step(s_even, s_odd, do_score=False)

    @pl.when((c == last) & (c % 2 == 1))
    def _():
        step(s_odd, s_even, do_score=False)


def _peer_call(h2t, u_bf, vt_bf, nt, at, cjt, bjt):
    d, t = h2t.shape
    ne = u_bf.shape[0]
    tm = 512
    ce = 1024
    nc = ne // ce
    nsub = ce // PEER_KEYS
    prev = lambda c: jnp.maximum(c - 1, 0)
    sub = lambda: pl.BlockSpec((PEER_HEADS, nsub, tm), lambda i, c: (0, prev(c), i))
    allk = lambda: pl.BlockSpec((PEER_HEADS, PEER_KEYS, tm), lambda i, c: (0, 0, i))
    return pl.pallas_call(
        _peer_kernel,
        grid=(t // tm, nc + 1),
        in_specs=[pl.BlockSpec((d, tm), lambda i, c: (0, i)),
                  pl.BlockSpec((ce, d), lambda i, c: (jnp.minimum(c, nc - 1), 0)),
                  pl.BlockSpec((d, ce), lambda i, c: (0, prev(c))),
                  sub(), sub(), allk(), allk()],
        out_specs=pl.BlockSpec((d, tm), lambda i, c: (0, i)),
        out_shape=jax.ShapeDtypeStruct((d, t), F32),
        scratch_shapes=[pltpu.VMEM((ce, tm), F32), pltpu.VMEM((ce, tm), F32), pltpu.VMEM((ce, tm), BF16)],
        compiler_params=_cparams("arbitrary", "arbitrary"),
    )(h2t, u_bf, vt_bf, nt, at, cjt, bjt)


def _final_kernel(pt_ref, x1_ref, mod_ref, gf_ref, ya_ref, yb_ref, *, n_first):
    g2 = mod_ref[...][5:6]
    x2 = x1_ref[...] + g2 * pt_ref[...].T
    y = _rms(x2, gf_ref[...])

    @pl.when(pl.program_id(0) < n_first)
    def _():
        ya_ref[...] = y

    @pl.when(pl.program_id(0) >= n_first)
    def _():
        yb_ref[...] = y


def _final_call(peer_t, x1, mod3, gf, n_first):
    b, s, d = x1.shape
    tm = 256
    nt = s // tm
    tok = lambda i, j: (i, j, 0)
    return pl.pallas_call(
        functools.partial(_final_kernel, n_first=n_first),
        grid=(b, nt),
        in_specs=[pl.BlockSpec((d, tm), lambda i, j: (0, i * nt + j)),
                  pl.BlockSpec((None, tm, d), tok),
                  pl.BlockSpec((None, N_MOD, d), lambda i, j: (i, 0, 0)),
                  _const_spec((1, d))],
        out_specs=list(_two_group_specs((None, tm, d), n_first, nt)),
        out_shape=[jax.ShapeDtypeStruct((n_first, s, d), F32), jax.ShapeDtypeStruct((b - n_first, s, d), F32)],
        compiler_params=_cparams("arbitrary", "arbitrary"),
    )(peer_t, x1, mod3, gf)


def _layer(xa, xb2, c, w_mod, b_mod, norm1_g, w_in, conv_w, conv_b, lru_w_r, lru_b_r, lru_w_i, lru_b_i,
           lru_lambda, attn_out_g, lru_out_g, w_out, norm2_g, peer_w_q, peer_sub_keys, peer_u, peer_v,
           norm_final_g):
    n_first, s, d = xa.shape
    b = n_first + xb2.shape[0]
    c8 = jnp.pad(c, ((0, (-b) % SUBLANES), (0, 0)))
    mod = _mod_call(c8, w_mod, b_mod[None, :])
    mod3 = mod.reshape(c8.shape[0], N_MOD, d)[:b]
    cosf, sinf = _rope_call(s)
    *qkv, xb, gate = _inproj_call(xa, xb2, mod3, norm1_g[None, :], w_in.astype(BF16), cosf, sinf)
    att = _attention(*qkv)
    hf, hr = _lru_call(xb, conv_w, conv_b[None, :], lru_w_r.astype(BF16), lru_w_i.astype(BF16),
                       lru_b_r, lru_b_i, lru_lambda)
    x1, h2t = _outproj_call(att, hf, hr, gate, xa, xb2, mod3, attn_out_g[None, :], lru_out_g[None, :],
                            w_out.astype(BF16), norm2_g[None, :])
    sk = peer_sub_keys.reshape(PEER_HEADS * 2, PEER_KEYS, -1).astype(BF16)
    nt, at, cjt, bjt = _peerq_call(h2t, peer_w_q.T.astype(BF16), sk)
    peer_t = _peer_call(h2t, peer_u.astype(BF16), peer_v.T.astype(BF16), nt, at, cjt, bjt)
    return _final_call(peer_t, x1, mod3, norm_final_g[None, :], n_first)


def kernel(x_prompt, x_sample, c_prompt, c_sample, w_mod, b_mod, norm1_g, w_in, conv_w, conv_b, lru_w_r,
           lru_b_r, lru_w_i, lru_b_i, lru_lambda, attn_out_g, lru_out_g, w_out, norm2_g, peer_w_q,
           peer_sub_keys, peer_u, peer_v, norm_final_g):
    assert w_mod.shape[0] == 1, "single-layer stack expected"
    assert x_prompt.shape[1:] == x_sample.shape[1:]
    c = jnp.concatenate([c_prompt, c_sample], axis=0)
    y_prompt, y_sample = _layer(
        x_prompt, x_sample, c, w_mod[0], b_mod[0], norm1_g[0], w_in[0], conv_w[0], conv_b[0], lru_w_r[0],
        lru_b_r[0], lru_w_i[0], lru_b_i[0], lru_lambda[0], attn_out_g[0], lru_out_g[0], w_out[0], norm2_g[0],
        peer_w_q[0], peer_sub_keys[0], peer_u[0], peer_v[0], norm_final_g)
    return (y_prompt, y_sample)
```

```python
import functools
import math

import jax
import jax.numpy as jnp
from jax import lax
from jax.experimental import pallas as pl
from jax.experimental.pallas import tpu as pltpu

F32 = jnp.float32
BF16 = jnp.bfloat16

HEAD_DIM = 128
N_HEADS = 8
ATTN_WIDTH = N_HEADS * HEAD_DIM
LRU_BLOCK = 128
DILATIONS = (16, 4, 1)
BAND_RADIUS = 64
ROPE_THETA = 10000.0
LRU_C = 8.0
PEER_HEADS = 8
PEER_KEYS = 128
PEER_TOPK = 16
N_MOD = 6
EPS = 1e-6
MASK_VALUE = -1e30
NEG_INF = float("-inf")

V7X_VMEM_BYTES = 64 * 1024 * 1024
VMEM_LIMIT = V7X_VMEM_BYTES - 8 * 1024 * 1024
LANES = 128
SUBLANES = 8


def _cparams(*sem):
    return pltpu.CompilerParams(dimension_semantics=sem, vmem_limit_bytes=VMEM_LIMIT)


def _const_spec(shape):
    nd = len(shape)
    return pl.BlockSpec(shape, lambda *_: (0,) * nd, pipeline_mode=pl.Buffered(1))


def _two_group_specs(block, n_first, n_tiles):
    first = pl.BlockSpec(block, lambda i, j: (jnp.minimum(i, n_first - 1), jnp.where(i < n_first, j, n_tiles - 1), 0))
    second = pl.BlockSpec(block, lambda i, j: (jnp.maximum(i - n_first, 0), jnp.where(i < n_first, 0, j), 0))
    return first, second


def _rms(x, g):
    return x * lax.rsqrt(jnp.mean(x * x, axis=-1, keepdims=True) + EPS) * g


def _gelu(x):
    alpha = -2.0 * math.sqrt(2.0 / math.pi) * math.log2(math.e)
    e = jnp.exp2(x * (alpha + (alpha * 0.044715) * (x * x)))
    return x * (1.0 / (1.0 + e))


def _mod_kernel(c_ref, w_ref, b_ref, o_ref):
    c = c_ref[...]
    sc = c * jax.nn.sigmoid(c)
    o_ref[...] = jnp.dot(sc, w_ref[...], precision=lax.Precision.HIGHEST,
                         preferred_element_type=F32) + b_ref[...]


def _mod_call(c8, w_mod, b_mod):
    rows, d = c8.shape
    n = w_mod.shape[1]
    tn = 1024
    return pl.pallas_call(
        _mod_kernel,
        grid=(n // tn,),
        in_specs=[pl.BlockSpec((rows, d), lambda j: (0, 0)),
                  pl.BlockSpec((d, tn), lambda j: (0, j)),
                  pl.BlockSpec((1, tn), lambda j: (0, j))],
        out_specs=pl.BlockSpec((rows, tn), lambda j: (0, j)),
        out_shape=jax.ShapeDtypeStruct((rows, n), F32),
        compiler_params=_cparams("arbitrary"),
    )(c8, w_mod, b_mod)


def _rope_kernel(inv_ref, cos_ref, sin_ref):
    ts = cos_ref.shape[0]
    pos = (pl.program_id(0) * ts + lax.broadcasted_iota(jnp.int32, (ts, HEAD_DIM), 0)).astype(F32)
    lane = lax.broadcasted_iota(jnp.int32, (ts, HEAD_DIM), 1)
    ang = pos * inv_ref[...]
    cos_ref[...] = jnp.cos(ang)
    sn = jnp.sin(ang)
    sin_ref[...] = jnp.where(lane < HEAD_DIM // 2, -sn, sn)


def _rope_call(seq):
    half = HEAD_DIM // 2
    inv = ROPE_THETA ** (-jnp.arange(half, dtype=F32) / half)
    inv2 = jnp.concatenate([inv, inv])[None, :]
    ts = min(seq, 1024)
    return pl.pallas_call(
        _rope_kernel,
        grid=(seq // ts,),
        in_specs=[pl.BlockSpec((1, HEAD_DIM), lambda i: (0, 0))],
        out_specs=[pl.BlockSpec((ts, HEAD_DIM), lambda i: (i, 0))] * 2,
        out_shape=[jax.ShapeDtypeStruct((seq, HEAD_DIM), F32)] * 2,
        compiler_params=_cparams("arbitrary"),
    )(inv2)


def _inproj_kernel(xa_ref, xb2_ref, mod_ref, g_ref, w_ref, cos_ref, sin_ref, p4_ref, p16_ref,
                   q1_ref, k1_ref, v1_ref, q4_ref, k4_ref, v4_ref, q16_ref, k16_ref, v16_ref,
                   xb_ref, gate_ref, *, n_first):
    x = jnp.where(pl.program_id(0) < n_first, xa_ref[...], xb2_ref[...])
    modv = mod_ref[...]
    sh1, sc1 = modv[0:1], modv[1:2]
    h = _rms(x, g_ref[...]) * (1.0 + sc1) + sh1
    hb = h.astype(BF16)
    cosv = cos_ref[...]
    sinv = sin_ref[...]
    aw = ATTN_WIDTH

    def rope(z, scale):
        parts = []
        for hd in range(N_HEADS):
            zs = z[:, hd * HEAD_DIM:(hd + 1) * HEAD_DIM]
            rot = pltpu.roll(zs, HEAD_DIM // 2, axis=1)
            r = zs * cosv + rot * sinv
            if scale != 1.0:
                r = r * scale
            parts.append(r.astype(BF16))
        return jnp.concatenate(parts, axis=1)

    def store_layouts(zb, o1_ref, o4_ref, o16_ref):
        o1_ref[...] = zb
        for p_ref, o_ref in ((p4_ref, o4_ref), (p16_ref, o16_ref)):
            perm = jnp.dot(p_ref[...], zb, preferred_element_type=F32).astype(BF16)
            o_ref[...] = perm.reshape(o_ref.shape)

    store_layouts(rope(jnp.dot(hb, w_ref[:, 0:aw], preferred_element_type=F32), HEAD_DIM ** -0.5 * math.log2(math.e)),
                  q1_ref, q4_ref, q16_ref)
    store_layouts(rope(jnp.dot(hb, w_ref[:, aw:2 * aw], preferred_element_type=F32), 1.0),
                  k1_ref, k4_ref, k16_ref)
    store_layouts(jnp.dot(hb, w_ref[:, 2 * aw:3 * aw], preferred_element_type=F32).astype(BF16),
                  v1_ref, v4_ref, v16_ref)
    lw = xb_ref.shape[-1]
    xb_ref[...] = jnp.dot(hb, w_ref[:, 3 * aw:3 * aw + lw], preferred_element_type=F32)
    gate_ref[...] = jnp.dot(hb, w_ref[:, 3 * aw + lw:3 * aw + 2 * lw], preferred_element_type=F32)


def _residue_perm(tm, dil):
    src = jnp.arange(tm)
    dst = (src % dil) * (tm // dil) + src // dil
    return jnp.zeros((tm, tm), BF16).at[dst, src].set(1.0)


def _inproj_call(xa, xb2, mod3, g1, w_in_bf, cosf, sinf):
    n_first, s, d = xa.shape
    b = n_first + xb2.shape[0]
    aw = ATTN_WIDTH
    lw = (w_in_bf.shape[1] - 3 * aw) // 2
    tm = 256
    tok = lambda i, j: (i, j, 0)
    grouped = lambda i, j: (i, 0, j, 0)
    lay1 = pl.BlockSpec((None, tm, aw), tok)
    lay4 = pl.BlockSpec((None, 4, tm // 4, aw), grouped)
    lay16 = pl.BlockSpec((None, 16, tm // 16, aw), grouped)
    shp1 = jax.ShapeDtypeStruct((b, s, aw), BF16)
    shp4 = jax.ShapeDtypeStruct((b, 4, s // 4, aw), BF16)
    shp16 = jax.ShapeDtypeStruct((b, 16, s // 16, aw), BF16)
    return pl.pallas_call(
        functools.partial(_inproj_kernel, n_first=n_first),
        grid=(b, s // tm),
        in_specs=[*_two_group_specs((None, tm, d), n_first, s // tm),
                  pl.BlockSpec((None, N_MOD, d), lambda i, j: (i, 0, 0)),
                  _const_spec((1, d)),
                  _const_spec(w_in_bf.shape),
                  pl.BlockSpec((tm, HEAD_DIM), lambda i, j: (j, 0)),
                  pl.BlockSpec((tm, HEAD_DIM), lambda i, j: (j, 0)),
                  _const_spec((tm, tm)), _const_spec((tm, tm))],
        out_specs=[lay1] * 3 + [lay4] * 3 + [lay16] * 3 + [pl.BlockSpec((None, tm, lw), tok)] * 2,
        out_shape=[shp1] * 3 + [shp4] * 3 + [shp16] * 3 + [jax.ShapeDtypeStruct((b, s, lw), F32)] * 2,
        compiler_params=_cparams("arbitrary", "arbitrary"),
    )(xa, xb2, mod3, g1, w_in_bf, cosf, sinf, _residue_perm(tm, 4), _residue_perm(tm, 16))


ATT_QB = 128
ATT_KW = ATT_QB + 2 * BAND_RADIUS


ATT_NB = 2048


def _attn_kernel(q1_ref, k1_ref, v1_ref, q4_ref, k4_ref, v4_ref, q16_ref, k16_ref, v16_ref, y_ref,
                 o_scr, l_scr, *, seq):
    blk = pl.program_id(2)
    nb = y_ref.shape[0]
    rel0 = (lax.broadcasted_iota(jnp.int32, (ATT_QB, ATT_KW), 1)
            - lax.broadcasted_iota(jnp.int32, (ATT_QB, ATT_KW), 0))
    branches = ((q16_ref, k16_ref, v16_ref, 16), (q4_ref, k4_ref, v4_ref, 4), (q1_ref, k1_ref, v1_ref, 1))
    for bi, (q_ref, k_ref, v_ref, dil) in enumerate(branches):
        per = nb // dil
        nj = per // ATT_QB
        sl = seq // dil

        def sub(idx, carry, bi=bi, q_ref=q_ref, k_ref=k_ref, v_ref=v_ref, dil=dil, per=per, nj=nj, sl=sl):
            r = idx // nj
            j = idx % nj
            l0 = blk * per + j * ATT_QB
            ks = pl.multiple_of(jnp.clip(l0 - BAND_RADIUS, 0, sl - ATT_KW), BAND_RADIUS)
            qrows = pl.ds(pl.multiple_of(j * ATT_QB, ATT_QB), ATT_QB)
            if dil == 1:
                q, kw, vw = q_ref[qrows, :], k_ref[pl.ds(ks, ATT_KW), :], v_ref[pl.ds(ks, ATT_KW), :]
            else:
                q, kw, vw = q_ref[r, qrows, :], k_ref[r, pl.ds(ks, ATT_KW), :], v_ref[r, pl.ds(ks, ATT_KW), :]
            s = lax.dot_general(q, kw, (((1,), (1,)), ((), ())), preferred_element_type=F32)
            shifted = (rel0 + (ks - l0 + BAND_RADIUS)).astype(jnp.uint32)
            s = jnp.where(shifted <= 2 * BAND_RADIUS, s, MASK_VALUE)
            m = jnp.max(s, axis=-1, keepdims=True)
            p = jnp.exp2(s - m)
            den = jnp.sum(p, axis=-1, keepdims=True)
            o = jnp.dot(p.astype(BF16), vw, preferred_element_type=F32) * (1.0 / den)
            lse = jnp.broadcast_to(m + jnp.log2(den), (ATT_QB, HEAD_DIM))
            if dil == 1:
                dst = qrows
            else:
                dst = pl.ds(j * (ATT_QB * dil) + r, ATT_QB, stride=dil)
            o_scr[bi, dst, :] = o
            l_scr[bi, dst, :] = lse
            return carry

        lax.fori_loop(0, dil * nj, sub, 0, unroll=True)

    def merge(j, carry):
        rows = pl.ds(pl.multiple_of(j * ATT_QB, ATT_QB), ATT_QB)
        l0, l1, l2 = l_scr[0, rows, :], l_scr[1, rows, :], l_scr[2, rows, :]
        mx = jnp.maximum(jnp.maximum(l0, l1), l2)
        w0, w1, w2 = jnp.exp2(l0 - mx), jnp.exp2(l1 - mx), jnp.exp2(l2 - mx)
        y = (o_scr[0, rows, :] * w0 + o_scr[1, rows, :] * w1 + o_scr[2, rows, :] * w2) * (1.0 / (w0 + w1 + w2))
        y_ref[rows, :] = y.astype(y_ref.dtype)
        return carry

    lax.fori_loop(0, nb // ATT_QB, merge, 0)


def _attention(q1, k1, v1, q4, k4, v4, q16, k16, v16):
    b, s, aw = q1.shape
    nb = min(s, ATT_NB)
    assert s % nb == 0 and s // 16 >= ATT_KW and nb % (16 * ATT_QB) == 0
    nat_q = pl.BlockSpec((None, nb, HEAD_DIM), lambda i, h, j: (i, j, h))
    nat_kv = pl.BlockSpec((None, s, HEAD_DIM), lambda i, h, j: (i, 0, h))
    grp_q = lambda dil: pl.BlockSpec((None, dil, nb // dil, HEAD_DIM), lambda i, h, j: (i, 0, j, h))
    grp_kv = lambda dil: pl.BlockSpec((None, dil, s // dil, HEAD_DIM), lambda i, h, j: (i, 0, 0, h))
    return pl.pallas_call(
        functools.partial(_attn_kernel, seq=s),
        grid=(b, N_HEADS, s // nb),
        in_specs=[nat_q, nat_kv, nat_kv, grp_q(4), grp_kv(4), grp_kv(4), grp_q(16), grp_kv(16), grp_kv(16)],
        out_specs=pl.BlockSpec((None, nb, HEAD_DIM), lambda i, h, j: (i, j, h)),
        out_shape=jax.ShapeDtypeStruct((b, s, aw), BF16),
        scratch_shapes=[pltpu.VMEM((len(DILATIONS), nb, HEAD_DIM), F32),
                        pltpu.VMEM((len(DILATIONS), nb, HEAD_DIM), F32)],
        compiler_params=_cparams("arbitrary", "arbitrary", "arbitrary"),
    )(q1, k1, v1, q4, k4, v4, q16, k16, v16)


def _lru_kernel(xf_ref, xfb_ref, xfa_ref, xr_ref, xrb_ref, xra_ref, cw_ref, cb_ref, wr_ref, wi_ref,
                br_ref, bi_ref, lam_ref, hf_ref, hr_ref, a_scr, b_scr, carry_scr):
    si = pl.program_id(1)
    ns = pl.num_programs(1)
    ts, w = xf_ref.shape
    ngroups = ts // SUBLANES

    @pl.when(si == 0)
    def _():
        carry_scr[...] = jnp.zeros_like(carry_scr)

    sub = lax.broadcasted_iota(jnp.int32, (ngroups, SUBLANES, LRU_BLOCK), 1)
    cw = cw_ref[...]
    cb = cb_ref[...]

    def prepare(dirn, main_ref, before_ref, after_ref, tile):
        main = main_ref[...]
        before = jnp.where(tile == 0, 0.0, before_ref[...])
        after = jnp.where(tile == ns - 1, 0.0, after_ref[...])
        rowf = lax.broadcasted_iota(jnp.int32, (ts, w), 0)
        xm1 = jnp.where(rowf == 0, before[7:8], pltpu.roll(main, 1, axis=0))
        xm2 = jnp.where(rowf == 0, before[6:7],
                        jnp.where(rowf == 1, before[7:8], pltpu.roll(main, 2, axis=0)))
        xp1 = jnp.where(rowf == ts - 1, after[0:1], pltpu.roll(main, ts - 1, axis=0))
        xc = cb + xm2 * cw[0:1] + xm1 * cw[1:2] + main * cw[2:3] + xp1 * cw[3:4]
        xcb = xc.astype(BF16)
        lam = lam_ref[dirn:dirn + 1, :]
        sp = jnp.maximum(-lam, 0.0) + jnp.log1p(jnp.exp(-jnp.abs(lam)))
        for n in range(w // LRU_BLOCK):
            cols = slice(n * LRU_BLOCK, (n + 1) * LRU_BLOCK)
            blk = xcb[:, cols]
            r = jax.nn.sigmoid(jnp.dot(blk, wr_ref[dirn, n], preferred_element_type=F32)
                               + br_ref[dirn:dirn + 1, cols])
            gi = jax.nn.sigmoid(jnp.dot(blk, wi_ref[dirn, n], preferred_element_type=F32)
                                + bi_ref[dirn:dirn + 1, cols])
            a = jnp.exp(-LRU_C * r * sp[:, cols])
            bt = jnp.sqrt(jnp.maximum(1.0 - a * a, 0.0)) * (gi * xc[:, cols])
            a = a.reshape(ngroups, SUBLANES, LRU_BLOCK)
            bt = bt.reshape(ngroups, SUBLANES, LRU_BLOCK)
            for shift in (1, 2, 4):
                if dirn == 0:
                    a_sh = pltpu.roll(a, shift, axis=1)
                    b_sh = pltpu.roll(bt, shift, axis=1)
                    ok = sub >= shift
                else:
                    a_sh = pltpu.roll(a, SUBLANES - shift, axis=1)
                    b_sh = pltpu.roll(bt, SUBLANES - shift, axis=1)
                    ok = sub < SUBLANES - shift
                bt = jnp.where(ok, a * b_sh + bt, bt)
                a = jnp.where(ok, a * a_sh, a)
            a_scr[dirn, :, cols] = a.reshape(ts, LRU_BLOCK)
            b_scr[dirn, :, cols] = bt.reshape(ts, LRU_BLOCK)

    prepare(0, xf_ref, xfb_ref, xfa_ref, si)
    prepare(1, xr_ref, xrb_ref, xra_ref, ns - 1 - si)

    def body(g, carry):
        cf, cr = carry
        rf = pl.ds(pl.multiple_of(g * SUBLANES, SUBLANES), SUBLANES)
        hf = b_scr[0, rf, :] + a_scr[0, rf, :] * cf
        hf_ref[rf, :] = hf
        cf = jnp.broadcast_to(hf[SUBLANES - 1:SUBLANES, :], (SUBLANES, w))
        rr = pl.ds(pl.multiple_of((ngroups - 1 - g) * SUBLANES, SUBLANES), SUBLANES)
        hr = b_scr[1, rr, :] + a_scr[1, rr, :] * cr
        hr_ref[rr, :] = hr
        cr = jnp.broadcast_to(hr[0:1, :], (SUBLANES, w))
        return cf, cr

    cf, cr = lax.fori_loop(0, ngroups, body, (carry_scr[0], carry_scr[1]))
    carry_scr[0] = cf
    carry_scr[1] = cr


def _lru_call(xb, conv_w, conv_b, wr_bf, wi_bf, b_r, b_i, lam):
    b, s, w = xb.shape
    ts = 256
    ns = s // ts
    hb = ts // SUBLANES
    nh = s // SUBLANES
    fwd = lambda i, j: (i, j, 0)
    rev = lambda i, j: (i, ns - 1 - j, 0)
    fwd_before = lambda i, j: (i, jnp.maximum(j * hb - 1, 0), 0)
    fwd_after = lambda i, j: (i, jnp.minimum((j + 1) * hb, nh - 1), 0)
    rev_before = lambda i, j: (i, jnp.maximum((ns - 1 - j) * hb - 1, 0), 0)
    rev_after = lambda i, j: (i, jnp.minimum((ns - j) * hb, nh - 1), 0)
    main = lambda im: pl.BlockSpec((None, ts, w), im)
    halo = lambda im: pl.BlockSpec((None, SUBLANES, w), im)
    return pl.pallas_call(
        _lru_kernel,
        grid=(b, ns),
        in_specs=[main(fwd), halo(fwd_before), halo(fwd_after),
                  main(rev), halo(rev_before), halo(rev_after),
                  _const_spec(conv_w.shape), _const_spec(conv_b.shape),
                  _const_spec(wr_bf.shape), _const_spec(wi_bf.shape),
                  _const_spec(b_r.shape), _const_spec(b_i.shape), _const_spec(lam.shape)],
        out_specs=[main(fwd), main(rev)],
        out_shape=[jax.ShapeDtypeStruct((b, s, w), F32)] * 2,
        scratch_shapes=[pltpu.VMEM((2, ts, w), F32), pltpu.VMEM((2, ts, w), F32),
                        pltpu.VMEM((2, SUBLANES, w), F32)],
        compiler_params=_cparams("arbitrary", "arbitrary"),
    )(xb, xb, xb, xb, xb, xb, conv_w, conv_b, wr_bf, wi_bf, b_r, b_i, lam)


def _outproj_kernel(att_ref, hf_ref, hr_ref, gate_ref, xa_ref, xb2_ref, mod_ref, ga_ref, gl_ref, w_ref, g2_ref,
                    x1_ref, h2t_ref, *, n_first):
    modv = mod_ref[...]
    g1, sh2, sc2 = modv[2:3], modv[3:4], modv[4:5]
    an = _rms(att_ref[...].astype(F32), ga_ref[...])
    lru = (hf_ref[...] + hr_ref[...]) * _gelu(gate_ref[...])
    ln = _rms(lru, gl_ref[...])
    aw = an.shape[-1]
    mix = (jnp.dot(an.astype(BF16), w_ref[0:aw, :], preferred_element_type=F32)
           + jnp.dot(ln.astype(BF16), w_ref[aw:, :], preferred_element_type=F32))
    x1 = jnp.where(pl.program_id(0) < n_first, xa_ref[...], xb2_ref[...]) + g1 * mix
    x1_ref[...] = x1
    h2 = _rms(x1, g2_ref[...]) * (1.0 + sc2) + sh2
    h2t_ref[...] = h2.T.astype(BF16)


def _outproj_call(att, hf, hr, gate, xa, xb2, mod3, ga, gl, w_out_bf, g2):
    n_first, s, d = xa.shape
    b = n_first + xb2.shape[0]
    aw, lw = att.shape[-1], hf.shape[-1]
    tm = 256
    nt = s // tm
    tok = lambda i, j: (i, j, 0)
    return pl.pallas_call(
        functools.partial(_outproj_kernel, n_first=n_first),
        grid=(b, nt),
        in_specs=[pl.BlockSpec((None, tm, aw), tok), pl.BlockSpec((None, tm, lw), tok),
                  pl.BlockSpec((None, tm, lw), tok), pl.BlockSpec((None, tm, lw), tok),
                  *_two_group_specs((None, tm, d), n_first, nt),
                  pl.BlockSpec((None, N_MOD, d), lambda i, j: (i, 0, 0)),
                  _const_spec((1, aw)), _const_spec((1, lw)), _const_spec(w_out_bf.shape),
                  _const_spec((1, d))],
        out_specs=[pl.BlockSpec((None, tm, d), tok),
                   pl.BlockSpec((d, tm), lambda i, j: (0, i * nt + j))],
        out_shape=[jax.ShapeDtypeStruct((b, s, d), F32), jax.ShapeDtypeStruct((d, b * s), BF16)],
        compiler_params=_cparams("arbitrary", "arbitrary"),
    )(att, hf, hr, gate, xa, xb2, mod3, ga, gl, w_out_bf, g2)


def _bitonic_merge(x):
    j = len(x) // 2
    while j >= 1:
        for i in range(len(x)):
            l = i ^ j
            if l > i:
                x[i], x[l] = jnp.maximum(x[i], x[l]), jnp.minimum(x[i], x[l])
        j //= 2
    return x


def _top16_network(s):
    n, t = s.shape
    x = [s[SUBLANES * v:SUBLANES * (v + 1), :] for v in range(n // SUBLANES)]
    assert len(x) == PEER_TOPK
    k = 2
    while k <= PEER_TOPK:
        j = k // 2
        while j >= 1:
            for i in range(PEER_TOPK):
                l = i ^ j
                if l > i:
                    hi, lo = jnp.maximum(x[i], x[l]), jnp.minimum(x[i], x[l])
                    x[i], x[l] = (hi, lo) if (i & k) == 0 else (lo, hi)
            j //= 2
        k *= 2
    for shift in (4, 2, 1):
        other = [pltpu.roll(v, shift, axis=0) for v in x]
        x = _bitonic_merge([jnp.maximum(x[k], other[PEER_TOPK - 1 - k]) for k in range(PEER_TOPK)])
    row8 = lax.broadcasted_iota(jnp.int32, (SUBLANES, t), 0)
    halves = []
    for base in (0, SUBLANES):
        acc = x[base]
        for k in range(1, SUBLANES):
            acc = jnp.where(row8 == k, x[base + k], acc)
        halves.append(acc)
    return jnp.concatenate(halves, axis=0)


def _product_key_select(s1, s2):
    t = s1.shape[1]
    a16 = _top16_network(s1)
    b16 = _top16_network(s2)
    rank2 = jnp.full(s2.shape, float(PEER_TOPK), F32)
    for kk in reversed(range(PEER_TOPK)):
        rank2 = jnp.where(s2 >= b16[kk:kk + 1, :], float(kk), rank2)
    row8 = lax.broadcasted_iota(jnp.int32, (SUBLANES, t), 0)
    a_lo, a_hi, b_lo, b_hi = a16[0:8], a16[8:16], b16[0:8], b16[8:16]
    arow = lambda i: a16[i:i + 1, :]
    brow = lambda j: b16[j:j + 1, :]
    pieces = [
        arow(0) + b_lo,
        arow(0) + b_hi,
        arow(1) + b_lo,
        a_hi + brow(0),
        jnp.where(row8 < 4, arow(3) + b_lo, a_lo + brow(0)),
        jnp.where(row8 < 4, arow(2) + b_lo, a_lo + brow(1)),
        jnp.where(row8 == 4, arow(2) + b_lo,
                  jnp.where(row8 == 2, arow(4) + b_lo, NEG_INF)),
    ]
    top = a16[0:1, :] + b16[0:1, :]
    z = jnp.zeros((1, t), F32)
    tau = top
    for kk in range(PEER_TOPK):
        m = pieces[0]
        for pc in pieces[1:]:
            m = jnp.maximum(m, pc)
        m = jnp.max(m, axis=0, keepdims=True)
        z = z + jnp.exp(m - top)
        tau = m
        pieces = [jnp.where(pc == m, NEG_INF, pc) for pc in pieces]
    cnt = jnp.zeros((PEER_TOPK, t), F32)
    for l in range(PEER_TOPK):
        cnt = cnt + jnp.where(a16 + b16[l:l + 1, :] >= tau, 1.0, 0.0)
    n1 = jnp.zeros(s1.shape, F32)
    for r in range(PEER_TOPK):
        n1 = jnp.where(s1 == a16[r:r + 1, :], cnt[r:r + 1, :], n1)
    return n1, jnp.exp(s1 - a16[0:1, :]) * (1.0 / z), rank2, jnp.exp(s2 - b16[0:1, :])


def _peerq_kernel(h2t_ref, wq_ref, sk_ref, n_ref, a_ref, cj_ref, bj_ref, qt_scr):
    tq = h2t_ref.shape[1]
    hrows = 2 * PEER_KEYS

    def project(h):
        rows = pl.ds(pl.multiple_of(h * hrows, hrows), hrows)
        qt_scr[rows, :] = jnp.dot(wq_ref[rows, :], h2t_ref[...], preferred_element_type=F32).astype(BF16)

    project(0)

    def head(h, carry):
        r1 = pl.ds(pl.multiple_of(2 * h * PEER_KEYS, PEER_KEYS), PEER_KEYS)
        r2 = pl.ds(pl.multiple_of((2 * h + 1) * PEER_KEYS, PEER_KEYS), PEER_KEYS)
        s1 = jnp.dot(sk_ref[2 * h], qt_scr[r1, :], preferred_element_type=F32)
        s2 = jnp.dot(sk_ref[2 * h + 1], qt_scr[r2, :], preferred_element_type=F32)
        project(jnp.minimum(h + 1, PEER_HEADS - 1))
        for st in range(tq // LANES):
            cols = slice(st * LANES, (st + 1) * LANES)
            n1, a, rank2, b = _product_key_select(s1[:, cols], s2[:, cols])
            n_ref[h, :, cols] = n1
            a_ref[h, :, cols] = a
            cj_ref[h, :, cols] = rank2.astype(BF16)
            bj_ref[h, :, cols] = b.astype(BF16)
        return carry

    lax.fori_loop(0, PEER_HEADS, head, 0)


def _peerq_call(h2t, wq_t_bf, sk_bf):
    d, t = h2t.shape
    tq = 512
    big = lambda: pl.BlockSpec((PEER_HEADS, PEER_KEYS, tq), lambda i: (0, 0, i))
    shp = lambda dt: jax.ShapeDtypeStruct((PEER_HEADS, PEER_KEYS, t), dt)
    return pl.pallas_call(
        _peerq_kernel,
        grid=(t // tq,),
        in_specs=[pl.BlockSpec((d, tq), lambda i: (0, i)),
                  _const_spec(wq_t_bf.shape), _const_spec(sk_bf.shape)],
        out_specs=[big(), big(), big(), big()],
        out_shape=[shp(F32), shp(F32), shp(BF16), shp(BF16)],
        scratch_shapes=[pltpu.VMEM((wq_t_bf.shape[0], tq), BF16)],
        compiler_params=_cparams("arbitrary"),
    )(h2t, wq_t_bf, sk_bf)


PEER_STRIP = 256


PEER_VALUE_K = 256


def _peer_kernel(h2t_ref, u_ref, vt_ref, n_ref, a_ref, cj_ref, bj_ref, acc_ref, s_even, s_odd, c_scr):
    c = pl.program_id(1)
    last = pl.num_programs(1) - 1
    ce, tm = s_even.shape

    def step(s_write, s_read, do_score=True, do_finish=True):
        per_slice = PEER_VALUE_K // PEER_KEYS
        half = tm // 2

        def weigh(ks):
            for il in range(ks * per_slice, (ks + 1) * per_slice):
                rows = slice(il * PEER_KEYS, (il + 1) * PEER_KEYS)
                for si in range(tm // PEER_STRIP):
                    cols = slice(si * PEER_STRIP, (si + 1) * PEER_STRIP)
                    w = jnp.zeros((PEER_KEYS, PEER_STRIP), BF16)
                    for h in range(PEER_HEADS):
                        nrow = n_ref[h, il:il + 1, cols].astype(BF16)
                        arow = a_ref[h, il:il + 1, cols].astype(BF16)
                        sel = cj_ref[h, :, cols] < nrow
                        w = w + jnp.where(sel, bj_ref[h, :, cols], jnp.zeros((), BF16)) * arow
                    act = _gelu(s_read[rows, cols]).astype(BF16)
                    c_scr[rows, cols] = w * act

        def value(ks):
            kk = slice(ks * PEER_VALUE_K, (ks + 1) * PEER_VALUE_K)
            acc_ref[...] += jnp.dot(vt_ref[:, kk], c_scr[kk, :], preferred_element_type=F32)

        def score(piece):
            cols = slice(piece * half, (piece + 1) * half)
            s_write[:, cols] = jnp.dot(u_ref[...], h2t_ref[:, cols], preferred_element_type=F32)

        order = (("weigh", 0), ("score", 0), ("weigh", 1), ("value", 0), ("value", 1),
                 ("weigh", 2), ("score", 1), ("weigh", 3), ("value", 2), ("value", 3))
        for kind, idx in order:
            if kind == "score":
                if do_score:
                    score(idx)
            elif do_finish:
                (weigh if kind == "weigh" else value)(idx)

    @pl.when(c == 0)
    def _():
        acc_ref[...] = jnp.zeros_like(acc_ref)
        step(s_even, s_odd, do_finish=False)

    @pl.when((c > 0) & (c < last) & (c % 2 == 0))
    def _():
        step(s_even, s_odd)

    @pl.when((c > 0) & (c < last) & (c % 2 == 1))
    def _():
        step(s_odd, s_even)

    @pl.when((c == last) & (c % 2 == 0))
    def _():
        step(s_even, s_odd, do_score=False)

    @pl.when((c == last) & (c % 2 == 1))
    def _():
        step(s_odd, s_even, do_score=False)


def _peer_call(h2t, u_bf, vt_bf, nt, at, cjt, bjt):
    d, t = h2t.shape
    ne = u_bf.shape[0]
    tm = 512
    ce = 1024
    nc = ne // ce
    nsub = ce // PEER_KEYS
    prev = lambda c: jnp.maximum(c - 1, 0)
    sub = lambda: pl.BlockSpec((PEER_HEADS, nsub, tm), lambda i, c: (0, prev(c), i))
    allk = lambda: pl.BlockSpec((PEER_HEADS, PEER_KEYS, tm), lambda i, c: (0, 0, i))
    return pl.pallas_call(
        _peer_kernel,
        grid=(t // tm, nc + 1),
        in_specs=[pl.BlockSpec((d, tm), lambda i, c: (0, i)),
                  pl.BlockSpec((ce, d), lambda i, c: (jnp.minimum(c, nc - 1), 0)),
                  pl.BlockSpec((d, ce), lambda i, c: (0, prev(c))),
                  sub(), sub(), allk(), allk()],
        out_specs=pl.BlockSpec((d, tm), lambda i, c: (0, i)),
        out_shape=jax.ShapeDtypeStruct((d, t), F32),
        scratch_shapes=[pltpu.VMEM((ce, tm), F32), pltpu.VMEM((ce, tm), F32), pltpu.VMEM((ce, tm), BF16)],
        compiler_params=_cparams("arbitrary", "arbitrary"),
    )(h2t, u_bf, vt_bf, nt, at, cjt, bjt)


def _final_kernel(pt_ref, x1_ref, mod_ref, gf_ref, ya_ref, yb_ref, *, n_first):
    g2 = mod_ref[...][5:6]
    x2 = x1_ref[...] + g2 * pt_ref[...].T
    y = _rms(x2, gf_ref[...])

    @pl.when(pl.program_id(0) < n_first)
    def _():
        ya_ref[...] = y

    @pl.when(pl.program_id(0) >= n_first)
    def _():
        yb_ref[...] = y


def _final_call(peer_t, x1, mod3, gf, n_first):
    b, s, d = x1.shape
    tm = 256
    nt = s // tm
    tok = lambda i, j: (i, j, 0)
    return pl.pallas_call(
        functools.partial(_final_kernel, n_first=n_first),
        grid=(b, nt),
        in_specs=[pl.BlockSpec((d, tm), lambda i, j: (0, i * nt + j)),
                  pl.BlockSpec((None, tm, d), tok),
                  pl.BlockSpec((None, N_MOD, d), lambda i, j: (i, 0, 0)),
                  _const_spec((1, d))],
        out_specs=list(_two_group_specs((None, tm, d), n_first, nt)),
        out_shape=[jax.ShapeDtypeStruct((n_first, s, d), F32), jax.ShapeDtypeStruct((b - n_first, s, d), F32)],
        compiler_params=_cparams("arbitrary", "arbitrary"),
    )(peer_t, x1, mod3, gf)


def _layer(xa, xb2, c, w_mod, b_mod, norm1_g, w_in, conv_w, conv_b, lru_w_r, lru_b_r, lru_w_i, lru_b_i,
           lru_lambda, attn_out_g, lru_out_g, w_out, norm2_g, peer_w_q, peer_sub_keys, peer_u, peer_v,
           norm_final_g):
    n_first, s, d = xa.shape
    b = n_first + xb2.shape[0]
    c8 = jnp.pad(c, ((0, (-b) % SUBLANES), (0, 0)))
    mod = _mod_call(c8, w_mod, b_mod[None, :])
    mod3 = mod.reshape(c8.shape[0], N_MOD, d)[:b]
    cosf, sinf = _rope_call(s)
    *qkv, xb, gate = _inproj_call(xa, xb2, mod3, norm1_g[None, :], w_in.astype(BF16), cosf, sinf)
    att = _attention(*qkv)
    hf, hr = _lru_call(xb, conv_w, conv_b[None, :], lru_w_r.astype(BF16), lru_w_i.astype(BF16),
                       lru_b_r, lru_b_i, lru_lambda)
    x1, h2t = _outproj_call(att, hf, hr, gate, xa, xb2, mod3, attn_out_g[None, :], lru_out_g[None, :],
                            w_out.astype(BF16), norm2_g[None, :])
    sk = peer_sub_keys.reshape(PEER_HEADS * 2, PEER_KEYS, -1).astype(BF16)
    nt, at, cjt, bjt = _peerq_call(h2t, peer_w_q.T.astype(BF16), sk)
    peer_t = _peer_call(h2t, peer_u.astype(BF16), peer_v.T.astype(BF16), nt, at, cjt, bjt)
    return _final_call(peer_t, x1, mod3, norm_final_g[None, :], n_first)


def kernel(x_prompt, x_sample, c_prompt, c_sample, w_mod, b_mod, norm1_g, w_in, conv_w, conv_b, lru_w_r,
           lru_b_r, lru_w_i, lru_b_i, lru_lambda, attn_out_g, lru_out_g, w_out, norm2_g, peer_w_q,
           peer_sub_keys, peer_u, peer_v, norm_final_g):
    assert w_mod.shape[0] == 1, "single-layer stack expected"
    assert x_prompt.shape[1:] == x_sample.shape[1:]
    c = jnp.concatenate([c_prompt, c_sample], axis=0)
    y_prompt, y_sample = _layer(
        x_prompt, x_sample, c, w_mod[0], b_mod[0], norm1_g[0], w_in[0], conv_w[0], conv_b[0], lru_w_r[0],
        lru_b_r[0], lru_w_i[0], lru_b_i[0], lru_lambda[0], attn_out_g[0], lru_out_g[0], w_out[0], norm2_g[0],
        peer_w_q[0], peer_sub_keys[0], peer_u[0], peer_v[0], norm_final_g)
    return (y_prompt, y_sample)
```

```python
import functools
import math

import jax
import jax.numpy as jnp
from jax import lax
from jax.experimental import pallas as pl
from jax.experimental.pallas import tpu as pltpu

F32 = jnp.float32
BF16 = jnp.bfloat16

HEAD_DIM = 128
N_HEADS = 8
ATTN_WIDTH = N_HEADS * HEAD_DIM
LRU_BLOCK = 128
DILATIONS = (16, 4, 1)
BAND_RADIUS = 64
ROPE_THETA = 10000.0
LRU_C = 8.0
PEER_HEADS = 8
PEER_KEYS = 128
PEER_TOPK = 16
N_MOD = 6
EPS = 1e-6
MASK_VALUE = -1e30
NEG_INF = float("-inf")

V7X_VMEM_BYTES = 64 * 1024 * 1024
VMEM_LIMIT = V7X_VMEM_BYTES - 8 * 1024 * 1024
LANES = 128
SUBLANES = 8


def _cparams(*sem):
    return pltpu.CompilerParams(dimension_semantics=sem, vmem_limit_bytes=VMEM_LIMIT)


def _const_spec(shape):
    nd = len(shape)
    return pl.BlockSpec(shape, lambda *_: (0,) * nd, pipeline_mode=pl.Buffered(1))


def _two_group_specs(block, n_first, n_tiles):
    first = pl.BlockSpec(block, lambda i, j: (jnp.minimum(i, n_first - 1), jnp.where(i < n_first, j, n_tiles - 1), 0))
    second = pl.BlockSpec(block, lambda i, j: (jnp.maximum(i - n_first, 0), jnp.where(i < n_first, 0, j), 0))
    return first, second


def _rms(x, g):
    return x * lax.rsqrt(jnp.mean(x * x, axis=-1, keepdims=True) + EPS) * g


def _gelu(x):
    alpha = -2.0 * math.sqrt(2.0 / math.pi) * math.log2(math.e)
    e = jnp.exp2(x * (alpha + (alpha * 0.044715) * (x * x)))
    return x * (1.0 / (1.0 + e))


def _mod_kernel(c_ref, w_ref, b_ref, o_ref):
    c = c_ref[...]
    sc = c * jax.nn.sigmoid(c)
    o_ref[...] = jnp.dot(sc, w_ref[...], precision=lax.Precision.HIGHEST,
                         preferred_element_type=F32) + b_ref[...]


def _mod_call(c8, w_mod, b_mod):
    rows, d = c8.shape
    n = w_mod.shape[1]
    tn = 1024
    return pl.pallas_call(
        _mod_kernel,
        grid=(n // tn,),
        in_specs=[pl.BlockSpec((rows, d), lambda j: (0, 0)),
                  pl.BlockSpec((d, tn), lambda j: (0, j)),
                  pl.BlockSpec((1, tn), lambda j: (0, j))],
        out_specs=pl.BlockSpec((rows, tn), lambda j: (0, j)),
        out_shape=jax.ShapeDtypeStruct((rows, n), F32),
        compiler_params=_cparams("arbitrary"),
    )(c8, w_mod, b_mod)


def _rope_kernel(inv_ref, cos_ref, sin_ref):
    ts = cos_ref.shape[0]
    pos = (pl.program_id(0) * ts + lax.broadcasted_iota(jnp.int32, (ts, HEAD_DIM), 0)).astype(F32)
    lane = lax.broadcasted_iota(jnp.int32, (ts, HEAD_DIM), 1)
    ang = pos * inv_ref[...]
    cos_ref[...] = jnp.cos(ang)
    sn = jnp.sin(ang)
    sin_ref[...] = jnp.where(lane < HEAD_DIM // 2, -sn, sn)


def _rope_call(seq):
    half = HEAD_DIM // 2
    inv = ROPE_THETA ** (-jnp.arange(half, dtype=F32) / half)
    inv2 = jnp.concatenate([inv, inv])[None, :]
    ts = min(seq, 1024)
    return pl.pallas_call(
        _rope_kernel,
        grid=(seq // ts,),
        in_specs=[pl.BlockSpec((1, HEAD_DIM), lambda i: (0, 0))],
        out_specs=[pl.BlockSpec((ts, HEAD_DIM), lambda i: (i, 0))] * 2,
        out_shape=[jax.ShapeDtypeStruct((seq, HEAD_DIM), F32)] * 2,
        compiler_params=_cparams("arbitrary"),
    )(inv2)


def _inproj_kernel(xa_ref, xb2_ref, mod_ref, g_ref, w_ref, cos_ref, sin_ref, p4_ref, p16_ref,
                   q1_ref, k1_ref, v1_ref, q4_ref, k4_ref, v4_ref, q16_ref, k16_ref, v16_ref,
                   xb_ref, gate_ref, *, n_first):
    x = jnp.where(pl.program_id(0) < n_first, xa_ref[...], xb2_ref[...])
    modv = mod_ref[...]
    sh1, sc1 = modv[0:1], modv[1:2]
    h = _rms(x, g_ref[...]) * (1.0 + sc1) + sh1
    hb = h.astype(BF16)
    cosv = cos_ref[...]
    sinv = sin_ref[...]
    aw = ATTN_WIDTH

    def rope(z, scale):
        parts = []
        for hd in range(N_HEADS):
            zs = z[:, hd * HEAD_DIM:(hd + 1) * HEAD_DIM]
            rot = pltpu.roll(zs, HEAD_DIM // 2, axis=1)
            r = zs * cosv + rot * sinv
            if scale != 1.0:
                r = r * scale
            parts.append(r.astype(BF16))
        return jnp.concatenate(parts, axis=1)

    def store_layouts(zb, o1_ref, o4_ref, o16_ref):
        o1_ref[...] = zb
        for p_ref, o_ref in ((p4_ref, o4_ref), (p16_ref, o16_ref)):
            perm = jnp.dot(p_ref[...], zb, preferred_element_type=F32).astype(BF16)
            o_ref[...] = perm.reshape(o_ref.shape)

    store_layouts(rope(jnp.dot(hb, w_ref[:, 0:aw], preferred_element_type=F32), HEAD_DIM ** -0.5 * math.log2(math.e)),
                  q1_ref, q4_ref, q16_ref)
    store_layouts(rope(jnp.dot(hb, w_ref[:, aw:2 * aw], preferred_element_type=F32), 1.0),
                  k1_ref, k4_ref, k16_ref)
    store_layouts(jnp.dot(hb, w_ref[:, 2 * aw:3 * aw], preferred_element_type=F32).astype(BF16),
                  v1_ref, v4_ref, v16_ref)
    lw = xb_ref.shape[-1]
    xb_ref[...] = jnp.dot(hb, w_ref[:, 3 * aw:3 * aw + lw], preferred_element_type=F32)
    gate_ref[...] = jnp.dot(hb, w_ref[:, 3 * aw + lw:3 * aw + 2 * lw],
                            preferred_element_type=F32).astype(gate_ref.dtype)


def _residue_perm(tm, dil):
    src = jnp.arange(tm)
    dst = (src % dil) * (tm // dil) + src // dil
    return jnp.zeros((tm, tm), BF16).at[dst, src].set(1.0)


def _inproj_call(xa, xb2, mod3, g1, w_in_bf, cosf, sinf):
    n_first, s, d = xa.shape
    b = n_first + xb2.shape[0]
    aw = ATTN_WIDTH
    lw = (w_in_bf.shape[1] - 3 * aw) // 2
    tm = 256
    tok = lambda i, j: (i, j, 0)
    grouped = lambda i, j: (i, 0, j, 0)
    lay1 = pl.BlockSpec((None, tm, aw), tok)
    lay4 = pl.BlockSpec((None, 4, tm // 4, aw), grouped)
    lay16 = pl.BlockSpec((None, 16, tm // 16, aw), grouped)
    shp1 = jax.ShapeDtypeStruct((b, s, aw), BF16)
    shp4 = jax.ShapeDtypeStruct((b, 4, s // 4, aw), BF16)
    shp16 = jax.ShapeDtypeStruct((b, 16, s // 16, aw), BF16)
    return pl.pallas_call(
        functools.partial(_inproj_kernel, n_first=n_first),
        grid=(b, s // tm),
        in_specs=[*_two_group_specs((None, tm, d), n_first, s // tm),
                  pl.BlockSpec((None, N_MOD, d), lambda i, j: (i, 0, 0)),
                  _const_spec((1, d)),
                  _const_spec(w_in_bf.shape),
                  pl.BlockSpec((tm, HEAD_DIM), lambda i, j: (j, 0)),
                  pl.BlockSpec((tm, HEAD_DIM), lambda i, j: (j, 0)),
                  _const_spec((tm, tm)), _const_spec((tm, tm))],
        out_specs=[lay1] * 3 + [lay4] * 3 + [lay16] * 3 + [pl.BlockSpec((None, tm, lw), tok)] * 2,
        out_shape=[shp1] * 3 + [shp4] * 3 + [shp16] * 3
        + [jax.ShapeDtypeStruct((b, s, lw), F32), jax.ShapeDtypeStruct((b, s, lw), BF16)],
        compiler_params=_cparams("arbitrary", "arbitrary"),
    )(xa, xb2, mod3, g1, w_in_bf, cosf, sinf, _residue_perm(tm, 4), _residue_perm(tm, 16))


ATT_QB = 128
ATT_KW = ATT_QB + 2 * BAND_RADIUS


ATT_NB = 2048


def _attn_kernel(q1_ref, k1_ref, v1_ref, q4_ref, k4_ref, v4_ref, q16_ref, k16_ref, v16_ref, y_ref,
                 o_scr, l_scr, *, seq):
    blk = pl.program_id(2)
    nb = y_ref.shape[0]
    rel0 = (lax.broadcasted_iota(jnp.int32, (ATT_QB, ATT_KW), 1)
            - lax.broadcasted_iota(jnp.int32, (ATT_QB, ATT_KW), 0))
    branches = ((q16_ref, k16_ref, v16_ref, 16), (q4_ref, k4_ref, v4_ref, 4), (q1_ref, k1_ref, v1_ref, 1))
    for bi, (q_ref, k_ref, v_ref, dil) in enumerate(branches):
        per = nb // dil
        nj = per // ATT_QB
        sl = seq // dil

        def sub(idx, carry, bi=bi, q_ref=q_ref, k_ref=k_ref, v_ref=v_ref, dil=dil, per=per, nj=nj, sl=sl):
            r = idx // nj
            j = idx % nj
            l0 = blk * per + j * ATT_QB
            ks = pl.multiple_of(jnp.clip(l0 - BAND_RADIUS, 0, sl - ATT_KW), BAND_RADIUS)
            qrows = pl.ds(pl.multiple_of(j * ATT_QB, ATT_QB), ATT_QB)
            if dil == 1:
                q, kw, vw = q_ref[qrows, :], k_ref[pl.ds(ks, ATT_KW), :], v_ref[pl.ds(ks, ATT_KW), :]
            else:
                q, kw, vw = q_ref[r, qrows, :], k_ref[r, pl.ds(ks, ATT_KW), :], v_ref[r, pl.ds(ks, ATT_KW), :]
            s = lax.dot_general(q, kw, (((1,), (1,)), ((), ())), preferred_element_type=F32)
            shifted = (rel0 + (ks - l0 + BAND_RADIUS)).astype(jnp.uint32)
            s = jnp.where(shifted <= 2 * BAND_RADIUS, s, MASK_VALUE)
            m = jnp.max(s, axis=-1, keepdims=True)
            p = jnp.exp2(s - m)
            den = jnp.sum(p, axis=-1, keepdims=True)
            o = jnp.dot(p.astype(BF16), vw, preferred_element_type=F32) * (1.0 / den)
            lse = jnp.broadcast_to(m + jnp.log2(den), (ATT_QB, HEAD_DIM))
            if dil != 1:
                dst = pl.ds(j * (ATT_QB * dil) + r, ATT_QB, stride=dil)
                o_scr[bi, dst, :] = o
                l_scr[bi, dst, :] = lse
            else:
                l0, l1 = l_scr[0, qrows, :], l_scr[1, qrows, :]
                mx = jnp.maximum(jnp.maximum(l0, l1), lse)
                w0, w1, w2 = jnp.exp2(l0 - mx), jnp.exp2(l1 - mx), jnp.exp2(lse - mx)
                y = (o_scr[0, qrows, :] * w0 + o_scr[1, qrows, :] * w1 + o * w2) * (1.0 / (w0 + w1 + w2))
                y_ref[qrows, :] = y.astype(y_ref.dtype)
            return carry

        lax.fori_loop(0, dil * nj, sub, 0, unroll=True)


def _attention(q1, k1, v1, q4, k4, v4, q16, k16, v16):
    b, s, aw = q1.shape
    nb = min(s, ATT_NB)
    assert s % nb == 0 and s // 16 >= ATT_KW and nb % (16 * ATT_QB) == 0
    nat_q = pl.BlockSpec((None, nb, HEAD_DIM), lambda i, h, j: (i, j, h))
    nat_kv = pl.BlockSpec((None, s, HEAD_DIM), lambda i, h, j: (i, 0, h))
    grp_q = lambda dil: pl.BlockSpec((None, dil, nb // dil, HEAD_DIM), lambda i, h, j: (i, 0, j, h))
    grp_kv = lambda dil: pl.BlockSpec((None, dil, s // dil, HEAD_DIM), lambda i, h, j: (i, 0, 0, h))
    return pl.pallas_call(
        functools.partial(_attn_kernel, seq=s),
        grid=(b, N_HEADS, s // nb),
        in_specs=[nat_q, nat_kv, nat_kv, grp_q(4), grp_kv(4), grp_kv(4), grp_q(16), grp_kv(16), grp_kv(16)],
        out_specs=pl.BlockSpec((None, nb, HEAD_DIM), lambda i, h, j: (i, j, h)),
        out_shape=jax.ShapeDtypeStruct((b, s, aw), BF16),
        scratch_shapes=[pltpu.VMEM((len(DILATIONS) - 1, nb, HEAD_DIM), F32),
                        pltpu.VMEM((len(DILATIONS) - 1, nb, HEAD_DIM), F32)],
        compiler_params=_cparams("arbitrary", "arbitrary", "arbitrary"),
    )(q1, k1, v1, q4, k4, v4, q16, k16, v16)


def _lru_kernel(xf_ref, xfb_ref, xfa_ref, xr_ref, xrb_ref, xra_ref, cw_ref, cb_ref, wr_ref, wi_ref,
                br_ref, bi_ref, lam_ref, hf_ref, hr_ref, a_scr, b_scr, carry_scr):
    si = pl.program_id(1)
    ns = pl.num_programs(1)
    ts, w = xf_ref.shape
    ngroups = ts // SUBLANES

    @pl.when(si == 0)
    def _():
        carry_scr[...] = jnp.zeros_like(carry_scr)

    sub = lax.broadcasted_iota(jnp.int32, (ngroups, SUBLANES, LRU_BLOCK), 1)
    cw = cw_ref[...]
    cb = cb_ref[...]

    def prepare(dirn, main_ref, before_ref, after_ref, tile):
        main = main_ref[...]
        before = jnp.where(tile == 0, 0.0, before_ref[...])
        after = jnp.where(tile == ns - 1, 0.0, after_ref[...])
        row8 = lax.broadcasted_iota(jnp.int32, (SUBLANES, w), 0)
        xm1 = pltpu.roll(main, 1, axis=0)
        xm1 = jnp.concatenate([jnp.where(row8 == 0, before[7:8], xm1[0:SUBLANES]), xm1[SUBLANES:]], axis=0)
        xm2 = pltpu.roll(main, 2, axis=0)
        xm2 = jnp.concatenate([jnp.where(row8 == 0, before[6:7],
                                         jnp.where(row8 == 1, before[7:8], xm2[0:SUBLANES])),
                               xm2[SUBLANES:]], axis=0)
        xp1 = pltpu.roll(main, ts - 1, axis=0)
        xp1 = jnp.concatenate([xp1[:ts - SUBLANES],
                               jnp.where(row8 == SUBLANES - 1, after[0:1], xp1[ts - SUBLANES:])], axis=0)
        xc = cb + xm2 * cw[0:1] + xm1 * cw[1:2] + main * cw[2:3] + xp1 * cw[3:4]
        xcb = xc.astype(BF16)
        lam = lam_ref[dirn:dirn + 1, :]
        sp = jnp.maximum(-lam, 0.0) + jnp.log1p(jnp.exp(-jnp.abs(lam)))
        decay = (-LRU_C * math.log2(math.e)) * sp
        for n in range(w // LRU_BLOCK):
            cols = slice(n * LRU_BLOCK, (n + 1) * LRU_BLOCK)
            blk = xcb[:, cols]
            r = jax.nn.sigmoid(jnp.dot(blk, wr_ref[dirn, n], preferred_element_type=F32)
                               + br_ref[dirn:dirn + 1, cols])
            gi = jax.nn.sigmoid(jnp.dot(blk, wi_ref[dirn, n], preferred_element_type=F32)
                                + bi_ref[dirn:dirn + 1, cols])
            a = jnp.exp2(r * decay[:, cols])
            bt = jnp.sqrt(jnp.maximum(1.0 - a * a, 0.0)) * (gi * xc[:, cols])
            a = a.reshape(ngroups, SUBLANES, LRU_BLOCK)
            bt = bt.reshape(ngroups, SUBLANES, LRU_BLOCK)
            for shift in (1, 2, 4):
                if dirn == 0:
                    a_sh = pltpu.roll(a, shift, axis=1)
                    b_sh = pltpu.roll(bt, shift, axis=1)
                    ok = sub >= shift
                else:
                    a_sh = pltpu.roll(a, SUBLANES - shift, axis=1)
                    b_sh = pltpu.roll(bt, SUBLANES - shift, axis=1)
                    ok = sub < SUBLANES - shift
                bt = jnp.where(ok, a * b_sh + bt, bt)
                a = jnp.where(ok, a * a_sh, a)
            a_scr[dirn, :, cols] = a.reshape(ts, LRU_BLOCK)
            b_scr[dirn, :, cols] = bt.reshape(ts, LRU_BLOCK)

    prepare(0, xf_ref, xfb_ref, xfa_ref, si)
    prepare(1, xr_ref, xrb_ref, xra_ref, ns - 1 - si)

    def body(g, carry):
        cf, cr = carry
        rf = pl.ds(pl.multiple_of(g * SUBLANES, SUBLANES), SUBLANES)
        hf = b_scr[0, rf, :] + a_scr[0, rf, :] * cf
        hf_ref[rf, :] = hf
        cf = jnp.broadcast_to(hf[SUBLANES - 1:SUBLANES, :], (SUBLANES, w))
        rr = pl.ds(pl.multiple_of((ngroups - 1 - g) * SUBLANES, SUBLANES), SUBLANES)
        hr = b_scr[1, rr, :] + a_scr[1, rr, :] * cr
        hr_ref[rr, :] = hr
        cr = jnp.broadcast_to(hr[0:1, :], (SUBLANES, w))
        return cf, cr

    cf, cr = lax.fori_loop(0, ngroups, body, (carry_scr[0], carry_scr[1]))
    carry_scr[0] = cf
    carry_scr[1] = cr


def _lru_call(xb, conv_w, conv_b, wr_bf, wi_bf, b_r, b_i, lam):
    b, s, w = xb.shape
    ts = 256
    ns = s // ts
    hb = ts // SUBLANES
    nh = s // SUBLANES
    fwd = lambda i, j: (i, j, 0)
    rev = lambda i, j: (i, ns - 1 - j, 0)
    fwd_before = lambda i, j: (i, jnp.maximum(j * hb - 1, 0), 0)
    fwd_after = lambda i, j: (i, jnp.minimum((j + 1) * hb, nh - 1), 0)
    rev_before = lambda i, j: (i, jnp.maximum((ns - 1 - j) * hb - 1, 0), 0)
    rev_after = lambda i, j: (i, jnp.minimum((ns - j) * hb, nh - 1), 0)
    main = lambda im: pl.BlockSpec((None, ts, w), im)
    halo = lambda im: pl.BlockSpec((None, SUBLANES, w), im)
    return pl.pallas_call(
        _lru_kernel,
        grid=(b, ns),
        in_specs=[main(fwd), halo(fwd_before), halo(fwd_after),
                  main(rev), halo(rev_before), halo(rev_after),
                  _const_spec(conv_w.shape), _const_spec(conv_b.shape),
                  _const_spec(wr_bf.shape), _const_spec(wi_bf.shape),
                  _const_spec(b_r.shape), _const_spec(b_i.shape), _const_spec(lam.shape)],
        out_specs=[main(fwd), main(rev)],
        out_shape=[jax.ShapeDtypeStruct((b, s, w), F32)] * 2,
        scratch_shapes=[pltpu.VMEM((2, ts, w), F32), pltpu.VMEM((2, ts, w), F32),
                        pltpu.VMEM((2, SUBLANES, w), F32)],
        compiler_params=_cparams("arbitrary", "arbitrary"),
    )(xb, xb, xb, xb, xb, xb, conv_w, conv_b, wr_bf, wi_bf, b_r, b_i, lam)


def _outproj_kernel(att_ref, hf_ref, hr_ref, gate_ref, xa_ref, xb2_ref, mod_ref, ga_ref, gl_ref, w_ref, g2_ref,
                    x1_ref, h2t_ref, *, n_first):
    modv = mod_ref[...]
    g1, sh2, sc2 = modv[2:3], modv[3:4], modv[4:5]
    an = _rms(att_ref[...].astype(F32), ga_ref[...])
    lru = (hf_ref[...] + hr_ref[...]) * _gelu(gate_ref[...].astype(F32))
    ln = _rms(lru, gl_ref[...])
    aw = an.shape[-1]
    mix = (jnp.dot(an.astype(BF16), w_ref[0:aw, :], preferred_element_type=F32)
           + jnp.dot(ln.astype(BF16), w_ref[aw:, :], preferred_element_type=F32))
    x1 = jnp.where(pl.program_id(0) < n_first, xa_ref[...], xb2_ref[...]) + g1 * mix
    x1_ref[...] = x1
    h2 = _rms(x1, g2_ref[...]) * (1.0 + sc2) + sh2
    h2t_ref[...] = h2.T.astype(BF16)


def _outproj_call(att, hf, hr, gate, xa, xb2, mod3, ga, gl, w_out_bf, g2):
    n_first, s, d = xa.shape
    b = n_first + xb2.shape[0]
    aw, lw = att.shape[-1], hf.shape[-1]
    tm = 256
    nt = s // tm
    tok = lambda i, j: (i, j, 0)
    return pl.pallas_call(
        functools.partial(_outproj_kernel, n_first=n_first),
        grid=(b, nt),
        in_specs=[pl.BlockSpec((None, tm, aw), tok), pl.BlockSpec((None, tm, lw), tok),
                  pl.BlockSpec((None, tm, lw), tok), pl.BlockSpec((None, tm, lw), tok),
                  *_two_group_specs((None, tm, d), n_first, nt),
                  pl.BlockSpec((None, N_MOD, d), lambda i, j: (i, 0, 0)),
                  _const_spec((1, aw)), _const_spec((1, lw)), _const_spec(w_out_bf.shape),
                  _const_spec((1, d))],
        out_specs=[pl.BlockSpec((None, tm, d), tok),
                   pl.BlockSpec((d, tm), lambda i, j: (0, i * nt + j))],
        out_shape=[jax.ShapeDtypeStruct((b, s, d), F32), jax.ShapeDtypeStruct((d, b * s), BF16)],
        compiler_params=_cparams("arbitrary", "arbitrary"),
    )(att, hf, hr, gate, xa, xb2, mod3, ga, gl, w_out_bf, g2)


def _bitonic_merge(x):
    j = len(x) // 2
    while j >= 1:
        for i in range(len(x)):
            l = i ^ j
            if l > i:
                x[i], x[l] = jnp.maximum(x[i], x[l]), jnp.minimum(x[i], x[l])
        j //= 2
    return x


def _top16_network(s):
    n, t = s.shape
    x = [s[SUBLANES * v:SUBLANES * (v + 1), :] for v in range(n // SUBLANES)]
    assert len(x) == PEER_TOPK
    k = 2
    while k <= PEER_TOPK:
        j = k // 2
        while j >= 1:
            for i in range(PEER_TOPK):
                l = i ^ j
                if l > i:
                    hi, lo = jnp.maximum(x[i], x[l]), jnp.minimum(x[i], x[l])
                    x[i], x[l] = (hi, lo) if (i & k) == 0 else (lo, hi)
            j //= 2
        k *= 2
    for shift in (4, 2, 1):
        other = [pltpu.roll(v, shift, axis=0) for v in x]
        x = _bitonic_merge([jnp.maximum(x[k], other[PEER_TOPK - 1 - k]) for k in range(PEER_TOPK)])
    row8 = lax.broadcasted_iota(jnp.int32, (SUBLANES, t), 0)
    halves = []
    for base in (0, SUBLANES):
        acc = x[base]
        for k in range(1, SUBLANES):
            acc = jnp.where(row8 == k, x[base + k], acc)
        halves.append(acc)
    return jnp.concatenate(halves, axis=0)


def _product_key_select(s1, s2):
    t = s1.shape[1]
    a16 = _top16_network(s1)
    b16 = _top16_network(s2)
    rank2 = jnp.full(s2.shape, float(PEER_TOPK), F32)
    for kk in reversed(range(PEER_TOPK)):
        rank2 = jnp.where(s2 >= b16[kk:kk + 1, :], float(kk), rank2)
    row8 = lax.broadcasted_iota(jnp.int32, (SUBLANES, t), 0)
    a_lo, a_hi, b_lo, b_hi = a16[0:8], a16[8:16], b16[0:8], b16[8:16]
    arow = lambda i: a16[i:i + 1, :]
    brow = lambda j: b16[j:j + 1, :]
    pieces = [
        arow(0) + b_lo,
        arow(0) + b_hi,
        arow(1) + b_lo,
        a_hi + brow(0),
        jnp.where(row8 < 4, arow(3) + b_lo, a_lo + brow(0)),
        jnp.where(row8 < 4, arow(2) + b_lo, a_lo + brow(1)),
        jnp.where(row8 == 4, arow(2) + b_lo,
                  jnp.where(row8 == 2, arow(4) + b_lo, NEG_INF)),
    ]
    top = a16[0:1, :] + b16[0:1, :]
    z = jnp.zeros((1, t), F32)
    tau = top
    for kk in range(PEER_TOPK):
        m = pieces[0]
        for pc in pieces[1:]:
            m = jnp.maximum(m, pc)
        m = jnp.max(m, axis=0, keepdims=True)
        z = z + jnp.exp(m - top)
        tau = m
        pieces = [jnp.where(pc == m, NEG_INF, pc) for pc in pieces]
    cnt = jnp.zeros((PEER_TOPK, t), F32)
    for l in range(PEER_TOPK):
        cnt = cnt + jnp.where(a16 + b16[l:l + 1, :] >= tau, 1.0, 0.0)
    n1 = jnp.zeros(s1.shape, F32)
    for r in range(PEER_TOPK):
        n1 = jnp.where(s1 == a16[r:r + 1, :], cnt[r:r + 1, :], n1)
    return n1, jnp.exp(s1 - a16[0:1, :]) * (1.0 / z), rank2, jnp.exp(s2 - b16[0:1, :])


def _peerq_kernel(h2t_ref, wq_ref, sk_ref, n_ref, a_ref, cj_ref, bj_ref, qt_scr):
    tq = h2t_ref.shape[1]
    hrows = 2 * PEER_KEYS

    def project(h):
        rows = pl.ds(pl.multiple_of(h * hrows, hrows), hrows)
        qt_scr[rows, :] = jnp.dot(wq_ref[rows, :], h2t_ref[...], preferred_element_type=F32).astype(BF16)

    project(0)

    def head(h, carry):
        r1 = pl.ds(pl.multiple_of(2 * h * PEER_KEYS, PEER_KEYS), PEER_KEYS)
        r2 = pl.ds(pl.multiple_of((2 * h + 1) * PEER_KEYS, PEER_KEYS), PEER_KEYS)
        s1 = jnp.dot(sk_ref[2 * h], qt_scr[r1, :], preferred_element_type=F32)
        s2 = jnp.dot(sk_ref[2 * h + 1], qt_scr[r2, :], preferred_element_type=F32)
        project(jnp.minimum(h + 1, PEER_HEADS - 1))
        for st in range(tq // LANES):
            cols = slice(st * LANES, (st + 1) * LANES)
            n1, a, rank2, b = _product_key_select(s1[:, cols], s2[:, cols])
            n_ref[h, :, cols] = n1
            a_ref[h, :, cols] = a
            cj_ref[h, :, cols] = rank2.astype(BF16)
            bj_ref[h, :, cols] = b.astype(BF16)
        return carry

    lax.fori_loop(0, PEER_HEADS, head, 0)


def _peerq_call(h2t, wq_t_bf, sk_bf):
    d, t = h2t.shape
    tq = 512
    big = lambda: pl.BlockSpec((PEER_HEADS, PEER_KEYS, tq), lambda i: (0, 0, i))
    shp = lambda dt: jax.ShapeDtypeStruct((PEER_HEADS, PEER_KEYS, t), dt)
    return pl.pallas_call(
        _peerq_kernel,
        grid=(t // tq,),
        in_specs=[pl.BlockSpec((d, tq), lambda i: (0, i)),
                  _const_spec(wq_t_bf.shape), _const_spec(sk_bf.shape)],
        out_specs=[big(), big(), big(), big()],
        out_shape=[shp(F32), shp(F32), shp(BF16), shp(BF16)],
        scratch_shapes=[pltpu.VMEM((wq_t_bf.shape[0], tq), BF16)],
        compiler_params=_cparams("arbitrary"),
    )(h2t, wq_t_bf, sk_bf)


PEER_STRIP = 256


PEER_VALUE_K = 256


def _peer_kernel(h2t_ref, u_ref, vt_ref, n_ref, a_ref, cj_ref, bj_ref, acc_ref, s_even, s_odd, c_scr):
    c = pl.program_id(1)
    last = pl.num_programs(1) - 1
    ce, tm = s_even.shape

    nsub = ce // PEER_KEYS
    chunks_per_block = n_ref.shape[1] // nsub

    def step(s_write, s_read, parity, do_score=True, do_finish=True):
        per_slice = PEER_VALUE_K // PEER_KEYS
        half = tm // 2
        row_off = ((parity + 1) % chunks_per_block) * nsub

        def weigh(ks):
            for il in range(ks * per_slice, (ks + 1) * per_slice):
                rows = slice(il * PEER_KEYS, (il + 1) * PEER_KEYS)
                krow = slice(row_off + il, row_off + il + 1)
                for si in range(tm // PEER_STRIP):
                    cols = slice(si * PEER_STRIP, (si + 1) * PEER_STRIP)
                    w = jnp.zeros((PEER_KEYS, PEER_STRIP), BF16)
                    for h in range(PEER_HEADS):
                        nrow = n_ref[h, krow, cols].astype(BF16)
                        arow = a_ref[h, krow, cols].astype(BF16)
                        sel = cj_ref[h, :, cols] < nrow
                        w = w + jnp.where(sel, bj_ref[h, :, cols], jnp.zeros((), BF16)) * arow
                    act = _gelu(s_read[rows, cols]).astype(BF16)
                    c_scr[rows, cols] = w * act

        def value(ks):
            kk = slice(ks * PEER_VALUE_K, (ks + 1) * PEER_VALUE_K)
            acc_ref[...] += jnp.dot(vt_ref[:, kk], c_scr[kk, :], preferred_element_type=F32)

        def score(piece):
            cols = slice(piece * half, (piece + 1) * half)
            s_write[:, cols] = jnp.dot(u_ref[...], h2t_ref[:, cols], preferred_element_type=F32)

        if ce // PEER_VALUE_K == 4:
            order = (("weigh", 0), ("score", 0), ("weigh", 1), ("value", 0), ("weigh", 2), ("value", 1),
                     ("score", 1), ("weigh", 3), ("value", 2), ("value", 3))
        else:
            assert ce // PEER_VALUE_K == 2
            order = (("weigh", 0), ("score", 0), ("weigh", 1), ("value", 0), ("score", 1), ("value", 1))
        for kind, idx in order:
            if kind == "score":
                if do_score:
                    score(idx)
            elif do_finish:
                (weigh if kind == "weigh" else value)(idx)

    @pl.when(c == 0)
    def _():
        acc_ref[...] = jnp.zeros_like(acc_ref)
        step(s_even, s_odd, 0, do_finish=False)

    @pl.when((c > 0) & (c < last) & (c % 2 == 0))
    def _():
        step(s_even, s_odd, 0)

    @pl.when((c > 0) & (c < last) & (c % 2 == 1))
    def _():
        step(s_odd, s_even, 1)

    @pl.when((c == last) & (c % 2 == 0))
    def _():
        step(s_even, s_odd, 0, do_score=False)

    @pl.when((c == last) & (c % 2 == 1))
    def _():
        step(s_odd, s_even, 1, do_score=False)


def _peer_call(h2t, u_bf, vt_bf, nt, at, cjt, bjt):
    d, t = h2t.shape
    ne = u_bf.shape[0]
    tm = 512
    ce = 1024
    nc = ne // ce
    chunks_per_block = SUBLANES * PEER_KEYS // ce
    assert chunks_per_block in (1, 2) and t % tm == 0
    prev = lambda c: jnp.maximum(c - 1, 0)
    sub = lambda: pl.BlockSpec((PEER_HEADS, SUBLANES, tm), lambda i, c: (0, prev(c) // chunks_per_block, i))
    allk = lambda: pl.BlockSpec((PEER_HEADS, PEER_KEYS, tm), lambda i, c: (0, 0, i))
    return pl.pallas_call(
        _peer_kernel,
        grid=(t // tm, nc + 1),
        in_specs=[pl.BlockSpec((d, tm), lambda i, c: (0, i)),
                  pl.BlockSpec((ce, d), lambda i, c: (jnp.minimum(c, nc - 1), 0)),
                  pl.BlockSpec((d, ce), lambda i, c: (0, prev(c))),
                  sub(), sub(), allk(), allk()],
        out_specs=pl.BlockSpec((d, tm), lambda i, c: (0, i)),
        out_shape=jax.ShapeDtypeStruct((d, t), F32),
        scratch_shapes=[pltpu.VMEM((ce, tm), F32), pltpu.VMEM((ce, tm), F32), pltpu.VMEM((ce, tm), BF16)],
        compiler_params=_cparams("arbitrary", "arbitrary"),
    )(h2t, u_bf, vt_bf, nt, at, cjt, bjt)


def _final_kernel(pt_ref, x1_ref, mod_ref, gf_ref, ya_ref, yb_ref, *, n_first):
    g2 = mod_ref[...][5:6]
    x2 = x1_ref[...] + g2 * pt_ref[...].T
    y = _rms(x2, gf_ref[...])

    @pl.when(pl.program_id(0) < n_first)
    def _():
        ya_ref[...] = y

    @pl.when(pl.program_id(0) >= n_first)
    def _():
        yb_ref[...] = y


def _final_call(peer_t, x1, mod3, gf, n_first):
    b, s, d = x1.shape
    tm = 256
    nt = s // tm
    tok = lambda i, j: (i, j, 0)
    return pl.pallas_call(
        functools.partial(_final_kernel, n_first=n_first),
        grid=(b, nt),
        in_specs=[pl.BlockSpec((d, tm), lambda i, j: (0, i * nt + j)),
                  pl.BlockSpec((None, tm, d), tok),
                  pl.BlockSpec((None, N_MOD, d), lambda i, j: (i, 0, 0)),
                  _const_spec((1, d))],
        out_specs=list(_two_group_specs((None, tm, d), n_first, nt)),
        out_shape=[jax.ShapeDtypeStruct((n_first, s, d), F32), jax.ShapeDtypeStruct((b - n_first, s, d), F32)],
        compiler_params=_cparams("arbitrary", "arbitrary"),
    )(peer_t, x1, mod3, gf)


def _layer(xa, xb2, c, w_mod, b_mod, norm1_g, w_in, conv_w, conv_b, lru_w_r, lru_b_r, lru_w_i, lru_b_i,
           lru_lambda, attn_out_g, lru_out_g, w_out, norm2_g, peer_w_q, peer_sub_keys, peer_u, peer_v,
           norm_final_g):
    n_first, s, d = xa.shape
    b = n_first + xb2.shape[0]
    c8 = jnp.pad(c, ((0, (-b) % SUBLANES), (0, 0)))
    mod = _mod_call(c8, w_mod, b_mod[None, :])
    mod3 = mod.reshape(c8.shape[0], N_MOD, d)[:b]
    cosf, sinf = _rope_call(s)
    *qkv, xb, gate = _inproj_call(xa, xb2, mod3, norm1_g[None, :], w_in.astype(BF16), cosf, sinf)
    att = _attention(*qkv)
    hf, hr = _lru_call(xb, conv_w, conv_b[None, :], lru_w_r.astype(BF16), lru_w_i.astype(BF16),
                       lru_b_r, lru_b_i, lru_lambda)
    x1, h2t = _outproj_call(att, hf, hr, gate, xa, xb2, mod3, attn_out_g[None, :], lru_out_g[None, :],
                            w_out.astype(BF16), norm2_g[None, :])
    sk = peer_sub_keys.reshape(PEER_HEADS * 2, PEER_KEYS, -1).astype(BF16)
    nt, at, cjt, bjt = _peerq_call(h2t, peer_w_q.T.astype(BF16), sk)
    peer_t = _peer_call(h2t, peer_u.astype(BF16), peer_v.T.astype(BF16), nt, at, cjt, bjt)
    return _final_call(peer_t, x1, mod3, norm_final_g[None, :], n_first)


def kernel(x_prompt, x_sample, c_prompt, c_sample, w_mod, b_mod, norm1_g, w_in, conv_w, conv_b, lru_w_r,
           lru_b_r, lru_w_i, lru_b_i, lru_lambda, attn_out_g, lru_out_g, w_out, norm2_g, peer_w_q,
           peer_sub_keys, peer_u, peer_v, norm_final_g):
    assert w_mod.shape[0] == 1, "single-layer stack expected"
    assert x_prompt.shape[1:] == x_sample.shape[1:]
    c = jnp.concatenate([c_prompt, c_sample], axis=0)
    y_prompt, y_sample = _layer(
        x_prompt, x_sample, c, w_mod[0], b_mod[0], norm1_g[0], w_in[0], conv_w[0], conv_b[0], lru_w_r[0],
        lru_b_r[0], lru_w_i[0], lru_b_i[0], lru_lambda[0], attn_out_g[0], lru_out_g[0], w_out[0], norm2_g[0],
        peer_w_q[0], peer_sub_keys[0], peer_u[0], peer_v[0], norm_final_g)
    return (y_prompt, y_sample)
```

```python
import functools
import math

import jax
import jax.numpy as jnp
from jax import lax
from jax.experimental import pallas as pl
from jax.experimental.pallas import tpu as pltpu

F32 = jnp.float32
BF16 = jnp.bfloat16

HEAD_DIM = 128
N_HEADS = 8
ATTN_WIDTH = N_HEADS * HEAD_DIM
LRU_BLOCK = 128
DILATIONS = (16, 4, 1)
BAND_RADIUS = 64
ROPE_THETA = 10000.0
LRU_C = 8.0
PEER_HEADS = 8
PEER_KEYS = 128
PEER_TOPK = 16
N_MOD = 6
EPS = 1e-6
MASK_VALUE = -1e30
NEG_INF = float("-inf")

V7X_VMEM_BYTES = 64 * 1024 * 1024
VMEM_LIMIT = V7X_VMEM_BYTES - 8 * 1024 * 1024
LANES = 128
SUBLANES = 8


def _cparams(*sem):
    return pltpu.CompilerParams(dimension_semantics=sem, vmem_limit_bytes=VMEM_LIMIT)


def _const_spec(shape):
    nd = len(shape)
    return pl.BlockSpec(shape, lambda *_: (0,) * nd, pipeline_mode=pl.Buffered(1))


def _two_group_specs(block, n_first, n_tiles):
    first = pl.BlockSpec(block, lambda i, j: (jnp.minimum(i, n_first - 1), jnp.where(i < n_first, j, n_tiles - 1), 0))
    second = pl.BlockSpec(block, lambda i, j: (jnp.maximum(i - n_first, 0), jnp.where(i < n_first, 0, j), 0))
    return first, second


def _rms(x, g):
    return x * lax.rsqrt(jnp.mean(x * x, axis=-1, keepdims=True) + EPS) * g


def _gelu(x):
    alpha = -2.0 * math.sqrt(2.0 / math.pi) * math.log2(math.e)
    e = jnp.exp2(x * (alpha + (alpha * 0.044715) * (x * x)))
    return x * (1.0 / (1.0 + e))


def _mod_kernel(c_ref, w_ref, b_ref, o_ref):
    c = c_ref[...]
    sc = c * jax.nn.sigmoid(c)
    o_ref[...] = jnp.dot(sc, w_ref[...], precision=lax.Precision.HIGHEST,
                         preferred_element_type=F32) + b_ref[...]


def _mod_call(c8, w_mod, b_mod):
    rows, d = c8.shape
    n = w_mod.shape[1]
    tn = 1024
    return pl.pallas_call(
        _mod_kernel,
        grid=(n // tn,),
        in_specs=[pl.BlockSpec((rows, d), lambda j: (0, 0)),
                  pl.BlockSpec((d, tn), lambda j: (0, j)),
                  pl.BlockSpec((1, tn), lambda j: (0, j))],
        out_specs=pl.BlockSpec((rows, tn), lambda j: (0, j)),
        out_shape=jax.ShapeDtypeStruct((rows, n), F32),
        compiler_params=_cparams("arbitrary"),
    )(c8, w_mod, b_mod)


def _rope_kernel(inv_ref, cos_ref, sin_ref):
    ts = cos_ref.shape[0]
    pos = (pl.program_id(0) * ts + lax.broadcasted_iota(jnp.int32, (ts, HEAD_DIM), 0)).astype(F32)
    lane = lax.broadcasted_iota(jnp.int32, (ts, HEAD_DIM), 1)
    ang = pos * inv_ref[...]
    cos_ref[...] = jnp.cos(ang)
    sn = jnp.sin(ang)
    sin_ref[...] = jnp.where(lane < HEAD_DIM // 2, -sn, sn)


def _rope_call(seq):
    half = HEAD_DIM // 2
    inv = ROPE_THETA ** (-jnp.arange(half, dtype=F32) / half)
    inv2 = jnp.concatenate([inv, inv])[None, :]
    ts = min(seq, 1024)
    return pl.pallas_call(
        _rope_kernel,
        grid=(seq // ts,),
        in_specs=[pl.BlockSpec((1, HEAD_DIM), lambda i: (0, 0))],
        out_specs=[pl.BlockSpec((ts, HEAD_DIM), lambda i: (i, 0))] * 2,
        out_shape=[jax.ShapeDtypeStruct((seq, HEAD_DIM), F32)] * 2,
        compiler_params=_cparams("arbitrary"),
    )(inv2)


def _inproj_kernel(xa_ref, xb2_ref, mod_ref, g_ref, w_ref, cos_ref, sin_ref, p4_ref, p16_ref,
                   q1_ref, k1_ref, v1_ref, q4_ref, k4_ref, v4_ref, q16_ref, k16_ref, v16_ref,
                   xb_ref, gate_ref, *, n_first):
    x = jnp.where(pl.program_id(0) < n_first, xa_ref[...], xb2_ref[...])
    modv = mod_ref[...]
    sh1, sc1 = modv[0:1], modv[1:2]
    h = _rms(x, g_ref[...]) * (1.0 + sc1) + sh1
    hb = h.astype(BF16)
    cosv = cos_ref[...]
    sinv = sin_ref[...]
    aw = ATTN_WIDTH

    def rope(z, scale):
        parts = []
        for hd in range(N_HEADS):
            zs = z[:, hd * HEAD_DIM:(hd + 1) * HEAD_DIM]
            rot = pltpu.roll(zs, HEAD_DIM // 2, axis=1)
            r = zs * cosv + rot * sinv
            if scale != 1.0:
                r = r * scale
            parts.append(r.astype(BF16))
        return jnp.concatenate(parts, axis=1)

    def store_layouts(zb, o1_ref, o4_ref, o16_ref):
        o1_ref[...] = zb
        for p_ref, o_ref in ((p4_ref, o4_ref), (p16_ref, o16_ref)):
            perm = jnp.dot(p_ref[...], zb, preferred_element_type=F32).astype(BF16)
            o_ref[...] = perm.reshape(o_ref.shape)

    store_layouts(rope(jnp.dot(hb, w_ref[:, 0:aw], preferred_element_type=F32), HEAD_DIM ** -0.5 * math.log2(math.e)),
                  q1_ref, q4_ref, q16_ref)
    store_layouts(rope(jnp.dot(hb, w_ref[:, aw:2 * aw], preferred_element_type=F32), 1.0),
                  k1_ref, k4_ref, k16_ref)
    store_layouts(jnp.dot(hb, w_ref[:, 2 * aw:3 * aw], preferred_element_type=F32).astype(BF16),
                  v1_ref, v4_ref, v16_ref)
    lw = xb_ref.shape[-1]
    xb_ref[...] = jnp.dot(hb, w_ref[:, 3 * aw:3 * aw + lw], preferred_element_type=F32)
    gate_ref[...] = jnp.dot(hb, w_ref[:, 3 * aw + lw:3 * aw + 2 * lw],
                            preferred_element_type=F32).astype(gate_ref.dtype)


def _residue_perm(tm, dil):
    src = jnp.arange(tm)
    dst = (src % dil) * (tm // dil) + src // dil
    return jnp.zeros((tm, tm), BF16).at[dst, src].set(1.0)


def _inproj_call(xa, xb2, mod3, g1, w_in_bf, cosf, sinf):
    n_first, s, d = xa.shape
    b = n_first + xb2.shape[0]
    aw = ATTN_WIDTH
    lw = (w_in_bf.shape[1] - 3 * aw) // 2
    tm = 256
    tok = lambda i, j: (i, j, 0)
    grouped = lambda i, j: (i, 0, j, 0)
    lay1 = pl.BlockSpec((None, tm, aw), tok)
    lay4 = pl.BlockSpec((None, 4, tm // 4, aw), grouped)
    lay16 = pl.BlockSpec((None, 16, tm // 16, aw), grouped)
    shp1 = jax.ShapeDtypeStruct((b, s, aw), BF16)
    shp4 = jax.ShapeDtypeStruct((b, 4, s // 4, aw), BF16)
    shp16 = jax.ShapeDtypeStruct((b, 16, s // 16, aw), BF16)
    return pl.pallas_call(
        functools.partial(_inproj_kernel, n_first=n_first),
        grid=(b, s // tm),
        in_specs=[*_two_group_specs((None, tm, d), n_first, s // tm),
                  pl.BlockSpec((None, N_MOD, d), lambda i, j: (i, 0, 0)),
                  _const_spec((1, d)),
                  _const_spec(w_in_bf.shape),
                  pl.BlockSpec((tm, HEAD_DIM), lambda i, j: (j, 0)),
                  pl.BlockSpec((tm, HEAD_DIM), lambda i, j: (j, 0)),
                  _const_spec((tm, tm)), _const_spec((tm, tm))],
        out_specs=[lay1] * 3 + [lay4] * 3 + [lay16] * 3 + [pl.BlockSpec((None, tm, lw), tok)] * 2,
        out_shape=[shp1] * 3 + [shp4] * 3 + [shp16] * 3
        + [jax.ShapeDtypeStruct((b, s, lw), F32), jax.ShapeDtypeStruct((b, s, lw), BF16)],
        compiler_params=_cparams("arbitrary", "arbitrary"),
    )(xa, xb2, mod3, g1, w_in_bf, cosf, sinf, _residue_perm(tm, 4), _residue_perm(tm, 16))


ATT_QB = 128
ATT_KW = ATT_QB + 2 * BAND_RADIUS


ATT_NB = 2048


def _attn_kernel(q1_ref, k1_ref, v1_ref, q4_ref, k4_ref, v4_ref, q16_ref, k16_ref, v16_ref, y_ref,
                 o_scr, l_scr, *, seq):
    blk = pl.program_id(2)
    nb = y_ref.shape[0]
    rel0 = (lax.broadcasted_iota(jnp.int32, (ATT_QB, ATT_KW), 1)
            - lax.broadcasted_iota(jnp.int32, (ATT_QB, ATT_KW), 0))
    branches = ((q16_ref, k16_ref, v16_ref, 16), (q4_ref, k4_ref, v4_ref, 4), (q1_ref, k1_ref, v1_ref, 1))
    for bi, (q_ref, k_ref, v_ref, dil) in enumerate(branches):
        per = nb // dil
        nj = per // ATT_QB
        sl = seq // dil

        def sub(idx, carry, bi=bi, q_ref=q_ref, k_ref=k_ref, v_ref=v_ref, dil=dil, per=per, nj=nj, sl=sl):
            r = idx // nj
            j = idx % nj
            l0 = blk * per + j * ATT_QB
            ks = pl.multiple_of(jnp.clip(l0 - BAND_RADIUS, 0, sl - ATT_KW), BAND_RADIUS)
            qrows = pl.ds(pl.multiple_of(j * ATT_QB, ATT_QB), ATT_QB)
            if dil == 1:
                q, kw, vw = q_ref[qrows, :], k_ref[pl.ds(ks, ATT_KW), :], v_ref[pl.ds(ks, ATT_KW), :]
            else:
                q, kw, vw = q_ref[r, qrows, :], k_ref[r, pl.ds(ks, ATT_KW), :], v_ref[r, pl.ds(ks, ATT_KW), :]
            s = lax.dot_general(q, kw, (((1,), (1,)), ((), ())), preferred_element_type=F32)
            shifted = (rel0 + (ks - l0 + BAND_RADIUS)).astype(jnp.uint32)
            s = jnp.where(shifted <= 2 * BAND_RADIUS, s, MASK_VALUE)
            m = jnp.max(s, axis=-1, keepdims=True)
            p = jnp.exp2(s - m)
            den = jnp.sum(p, axis=-1, keepdims=True)
            o = jnp.dot(p.astype(BF16), vw, preferred_element_type=F32) * (1.0 / den)
            lse = jnp.broadcast_to(m + jnp.log2(den), (ATT_QB, HEAD_DIM))
            if dil != 1:
                dst = pl.ds(j * (ATT_QB * dil) + r, ATT_QB, stride=dil)
                o_scr[bi, dst, :] = o
                l_scr[bi, dst, :] = lse
            else:
                l0, l1 = l_scr[0, qrows, :], l_scr[1, qrows, :]
                mx = jnp.maximum(jnp.maximum(l0, l1), lse)
                w0, w1, w2 = jnp.exp2(l0 - mx), jnp.exp2(l1 - mx), jnp.exp2(lse - mx)
                y = (o_scr[0, qrows, :] * w0 + o_scr[1, qrows, :] * w1 + o * w2) * (1.0 / (w0 + w1 + w2))
                y_ref[qrows, :] = y.astype(y_ref.dtype)
            return carry

        lax.fori_loop(0, dil * nj, sub, 0, unroll=True)


def _attention(q1, k1, v1, q4, k4, v4, q16, k16, v16):
    b, s, aw = q1.shape
    nb = min(s, ATT_NB)
    assert s % nb == 0 and s // 16 >= ATT_KW and nb % (16 * ATT_QB) == 0
    nat_q = pl.BlockSpec((None, nb, HEAD_DIM), lambda i, h, j: (i, j, h))
    nat_kv = pl.BlockSpec((None, s, HEAD_DIM), lambda i, h, j: (i, 0, h))
    grp_q = lambda dil: pl.BlockSpec((None, dil, nb // dil, HEAD_DIM), lambda i, h, j: (i, 0, j, h))
    grp_kv = lambda dil: pl.BlockSpec((None, dil, s // dil, HEAD_DIM), lambda i, h, j: (i, 0, 0, h))
    return pl.pallas_call(
        functools.partial(_attn_kernel, seq=s),
        grid=(b, N_HEADS, s // nb),
        in_specs=[nat_q, nat_kv, nat_kv, grp_q(4), grp_kv(4), grp_kv(4), grp_q(16), grp_kv(16), grp_kv(16)],
        out_specs=pl.BlockSpec((None, nb, HEAD_DIM), lambda i, h, j: (i, j, h)),
        out_shape=jax.ShapeDtypeStruct((b, s, aw), BF16),
        scratch_shapes=[pltpu.VMEM((len(DILATIONS) - 1, nb, HEAD_DIM), F32),
                        pltpu.VMEM((len(DILATIONS) - 1, nb, HEAD_DIM), F32)],
        compiler_params=_cparams("arbitrary", "arbitrary", "arbitrary"),
    )(q1, k1, v1, q4, k4, v4, q16, k16, v16)


def _lru_kernel(xf_ref, xfb_ref, xfa_ref, xr_ref, xrb_ref, xra_ref, cw_ref, cb_ref, wr_ref, wi_ref,
                br_ref, bi_ref, lam_ref, hf_ref, hr_ref, a_scr, b_scr, carry_scr):
    si = pl.program_id(1)
    ns = pl.num_programs(1)
    ts, w = xf_ref.shape
    ngroups = ts // SUBLANES

    @pl.when(si == 0)
    def _():
        carry_scr[...] = jnp.zeros_like(carry_scr)

    sub = lax.broadcasted_iota(jnp.int32, (ngroups, SUBLANES, LRU_BLOCK), 1)
    cw = cw_ref[...]
    cb = cb_ref[...]

    def prepare(dirn, main_ref, before_ref, after_ref, tile):
        main = main_ref[...]
        before = jnp.where(tile == 0, 0.0, before_ref[...])
        after = jnp.where(tile == ns - 1, 0.0, after_ref[...])
        row8 = lax.broadcasted_iota(jnp.int32, (SUBLANES, w), 0)
        xm1 = pltpu.roll(main, 1, axis=0)
        xm1 = jnp.concatenate([jnp.where(row8 == 0, before[7:8], xm1[0:SUBLANES]), xm1[SUBLANES:]], axis=0)
        xm2 = pltpu.roll(main, 2, axis=0)
        xm2 = jnp.concatenate([jnp.where(row8 == 0, before[6:7],
                                         jnp.where(row8 == 1, before[7:8], xm2[0:SUBLANES])),
                               xm2[SUBLANES:]], axis=0)
        xp1 = pltpu.roll(main, ts - 1, axis=0)
        xp1 = jnp.concatenate([xp1[:ts - SUBLANES],
                               jnp.where(row8 == SUBLANES - 1, after[0:1], xp1[ts - SUBLANES:])], axis=0)
        xc = cb + xm2 * cw[0:1] + xm1 * cw[1:2] + main * cw[2:3] + xp1 * cw[3:4]
        xcb = xc.astype(BF16)
        lam = lam_ref[dirn:dirn + 1, :]
        sp = jnp.maximum(-lam, 0.0) + jnp.log1p(jnp.exp(-jnp.abs(lam)))
        decay = (-LRU_C * math.log2(math.e)) * sp
        for n in range(w // LRU_BLOCK):
            cols = slice(n * LRU_BLOCK, (n + 1) * LRU_BLOCK)
            blk = xcb[:, cols]
            r = jax.nn.sigmoid(jnp.dot(blk, wr_ref[dirn, n], preferred_element_type=F32)
                               + br_ref[dirn:dirn + 1, cols])
            gi = jax.nn.sigmoid(jnp.dot(blk, wi_ref[dirn, n], preferred_element_type=F32)
                                + bi_ref[dirn:dirn + 1, cols])
            a = jnp.exp2(r * decay[:, cols])
            bt = jnp.sqrt(jnp.maximum(1.0 - a * a, 0.0)) * (gi * xc[:, cols])
            a = a.reshape(ngroups, SUBLANES, LRU_BLOCK)
            bt = bt.reshape(ngroups, SUBLANES, LRU_BLOCK)
            for shift in (1, 2, 4):
                if dirn == 0:
                    a_sh = pltpu.roll(a, shift, axis=1)
                    b_sh = pltpu.roll(bt, shift, axis=1)
                    ok = sub >= shift
                else:
                    a_sh = pltpu.roll(a, SUBLANES - shift, axis=1)
                    b_sh = pltpu.roll(bt, SUBLANES - shift, axis=1)
                    ok = sub < SUBLANES - shift
                bt = jnp.where(ok, a * b_sh + bt, bt)
                a = jnp.where(ok, a * a_sh, a)
            a_scr[dirn, :, cols] = a.reshape(ts, LRU_BLOCK)
            b_scr[dirn, :, cols] = bt.reshape(ts, LRU_BLOCK)

    prepare(0, xf_ref, xfb_ref, xfa_ref, si)
    prepare(1, xr_ref, xrb_ref, xra_ref, ns - 1 - si)

    def body(g, carry):
        cf, cr = carry
        rf = pl.ds(pl.multiple_of(g * SUBLANES, SUBLANES), SUBLANES)
        hf = b_scr[0, rf, :] + a_scr[0, rf, :] * cf
        hf_ref[rf, :] = hf
        cf = jnp.broadcast_to(hf[SUBLANES - 1:SUBLANES, :], (SUBLANES, w))
        rr = pl.ds(pl.multiple_of((ngroups - 1 - g) * SUBLANES, SUBLANES), SUBLANES)
        hr = b_scr[1, rr, :] + a_scr[1, rr, :] * cr
        hr_ref[rr, :] = hr
        cr = jnp.broadcast_to(hr[0:1, :], (SUBLANES, w))
        return cf, cr

    cf, cr = lax.fori_loop(0, ngroups, body, (carry_scr[0], carry_scr[1]))
    carry_scr[0] = cf
    carry_scr[1] = cr


def _lru_call(xb, conv_w, conv_b, wr_bf, wi_bf, b_r, b_i, lam):
    b, s, w = xb.shape
    ts = 256
    ns = s // ts
    hb = ts // SUBLANES
    nh = s // SUBLANES
    fwd = lambda i, j: (i, j, 0)
    rev = lambda i, j: (i, ns - 1 - j, 0)
    fwd_before = lambda i, j: (i, jnp.maximum(j * hb - 1, 0), 0)
    fwd_after = lambda i, j: (i, jnp.minimum((j + 1) * hb, nh - 1), 0)
    rev_before = lambda i, j: (i, jnp.maximum((ns - 1 - j) * hb - 1, 0), 0)
    rev_after = lambda i, j: (i, jnp.minimum((ns - j) * hb, nh - 1), 0)
    main = lambda im: pl.BlockSpec((None, ts, w), im)
    halo = lambda im: pl.BlockSpec((None, SUBLANES, w), im)
    return pl.pallas_call(
        _lru_kernel,
        grid=(b, ns),
        in_specs=[main(fwd), halo(fwd_before), halo(fwd_after),
                  main(rev), halo(rev_before), halo(rev_after),
                  _const_spec(conv_w.shape), _const_spec(conv_b.shape),
                  _const_spec(wr_bf.shape), _const_spec(wi_bf.shape),
                  _const_spec(b_r.shape), _const_spec(b_i.shape), _const_spec(lam.shape)],
        out_specs=[main(fwd), main(rev)],
        out_shape=[jax.ShapeDtypeStruct((b, s, w), F32)] * 2,
        scratch_shapes=[pltpu.VMEM((2, ts, w), F32), pltpu.VMEM((2, ts, w), F32),
                        pltpu.VMEM((2, SUBLANES, w), F32)],
        compiler_params=_cparams("arbitrary", "arbitrary"),
    )(xb, xb, xb, xb, xb, xb, conv_w, conv_b, wr_bf, wi_bf, b_r, b_i, lam)


def _outproj_kernel(att_ref, hf_ref, hr_ref, gate_ref, xa_ref, xb2_ref, mod_ref, ga_ref, gl_ref, w_ref, g2_ref,
                    x1_ref, h2t_ref, *, n_first):
    modv = mod_ref[...]
    g1, sh2, sc2 = modv[2:3], modv[3:4], modv[4:5]
    an = _rms(att_ref[...].astype(F32), ga_ref[...])
    lru = (hf_ref[...] + hr_ref[...]) * _gelu(gate_ref[...].astype(F32))
    ln = _rms(lru, gl_ref[...])
    aw = an.shape[-1]
    mix = (jnp.dot(an.astype(BF16), w_ref[0:aw, :], preferred_element_type=F32)
           + jnp.dot(ln.astype(BF16), w_ref[aw:, :], preferred_element_type=F32))
    x1 = jnp.where(pl.program_id(0) < n_first, xa_ref[...], xb2_ref[...]) + g1 * mix
    x1_ref[...] = x1
    h2 = _rms(x1, g2_ref[...]) * (1.0 + sc2) + sh2
    h2t_ref[...] = h2.T.astype(BF16)


def _outproj_call(att, hf, hr, gate, xa, xb2, mod3, ga, gl, w_out_bf, g2):
    n_first, s, d = xa.shape
    b = n_first + xb2.shape[0]
    aw, lw = att.shape[-1], hf.shape[-1]
    tm = 512
    nt = s // tm
    tok = lambda i, j: (i, j, 0)
    return pl.pallas_call(
        functools.partial(_outproj_kernel, n_first=n_first),
        grid=(b, nt),
        in_specs=[pl.BlockSpec((None, tm, aw), tok), pl.BlockSpec((None, tm, lw), tok),
                  pl.BlockSpec((None, tm, lw), tok), pl.BlockSpec((None, tm, lw), tok),
                  *_two_group_specs((None, tm, d), n_first, nt),
                  pl.BlockSpec((None, N_MOD, d), lambda i, j: (i, 0, 0)),
                  _const_spec((1, aw)), _const_spec((1, lw)), _const_spec(w_out_bf.shape),
                  _const_spec((1, d))],
        out_specs=[pl.BlockSpec((None, tm, d), tok),
                   pl.BlockSpec((d, tm), lambda i, j: (0, i * nt + j))],
        out_shape=[jax.ShapeDtypeStruct((b, s, d), F32), jax.ShapeDtypeStruct((d, b * s), BF16)],
        compiler_params=_cparams("arbitrary", "arbitrary"),
    )(att, hf, hr, gate, xa, xb2, mod3, ga, gl, w_out_bf, g2)


def _bitonic_merge(x):
    j = len(x) // 2
    while j >= 1:
        for i in range(len(x)):
            l = i ^ j
            if l > i:
                x[i], x[l] = jnp.maximum(x[i], x[l]), jnp.minimum(x[i], x[l])
        j //= 2
    return x


def _top16_network(s):
    n, t = s.shape
    x = [s[SUBLANES * v:SUBLANES * (v + 1), :] for v in range(n // SUBLANES)]
    assert len(x) == PEER_TOPK
    k = 2
    while k <= PEER_TOPK:
        j = k // 2
        while j >= 1:
            for i in range(PEER_TOPK):
                l = i ^ j
                if l > i:
                    hi, lo = jnp.maximum(x[i], x[l]), jnp.minimum(x[i], x[l])
                    x[i], x[l] = (hi, lo) if (i & k) == 0 else (lo, hi)
            j //= 2
        k *= 2
    for shift in (4, 2, 1):
        other = [pltpu.roll(v, shift, axis=0) for v in x]
        x = _bitonic_merge([jnp.maximum(x[k], other[PEER_TOPK - 1 - k]) for k in range(PEER_TOPK)])
    row8 = lax.broadcasted_iota(jnp.int32, (SUBLANES, t), 0)
    halves = []
    for base in (0, SUBLANES):
        acc = x[base]
        for k in range(1, SUBLANES):
            acc = jnp.where(row8 == k, x[base + k], acc)
        halves.append(acc)
    return jnp.concatenate(halves, axis=0)


def _product_key_select(s1, s2):
    t = s1.shape[1]
    a16 = _top16_network(s1)
    b16 = _top16_network(s2)
    rank2 = jnp.full(s2.shape, float(PEER_TOPK), F32)
    for kk in reversed(range(PEER_TOPK)):
        rank2 = jnp.where(s2 >= b16[kk:kk + 1, :], float(kk), rank2)
    row8 = lax.broadcasted_iota(jnp.int32, (SUBLANES, t), 0)
    a_lo, a_hi, b_lo, b_hi = a16[0:8], a16[8:16], b16[0:8], b16[8:16]
    arow = lambda i: a16[i:i + 1, :]
    brow = lambda j: b16[j:j + 1, :]
    pieces = [
        arow(0) + b_lo,
        arow(0) + b_hi,
        arow(1) + b_lo,
        a_hi + brow(0),
        jnp.where(row8 < 4, arow(3) + b_lo, a_lo + brow(0)),
        jnp.where(row8 < 4, arow(2) + b_lo, a_lo + brow(1)),
        jnp.where(row8 == 4, arow(2) + b_lo,
                  jnp.where(row8 == 2, arow(4) + b_lo, NEG_INF)),
    ]
    top = a16[0:1, :] + b16[0:1, :]
    z = jnp.zeros((1, t), F32)
    tau = top
    for kk in range(PEER_TOPK):
        m = pieces[0]
        for pc in pieces[1:]:
            m = jnp.maximum(m, pc)
        m = jnp.max(m, axis=0, keepdims=True)
        z = z + jnp.exp(m - top)
        tau = m
        pieces = [jnp.where(pc == m, NEG_INF, pc) for pc in pieces]
    cnt = jnp.zeros((PEER_TOPK, t), F32)
    for l in range(PEER_TOPK):
        cnt = cnt + jnp.where(a16 + b16[l:l + 1, :] >= tau, 1.0, 0.0)
    n1 = jnp.zeros(s1.shape, F32)
    for r in range(PEER_TOPK):
        n1 = jnp.where(s1 == a16[r:r + 1, :], cnt[r:r + 1, :], n1)
    return n1, jnp.exp(s1 - a16[0:1, :]) * (1.0 / z), rank2, jnp.exp(s2 - b16[0:1, :])


def _peerq_kernel(h2t_ref, wq_ref, sk_ref, n_ref, a_ref, cj_ref, bj_ref, qt_scr):
    tq = h2t_ref.shape[1]
    hrows = 2 * PEER_KEYS

    def project(h):
        rows = pl.ds(pl.multiple_of(h * hrows, hrows), hrows)
        qt_scr[rows, :] = jnp.dot(wq_ref[rows, :], h2t_ref[...], preferred_element_type=F32).astype(BF16)

    project(0)

    def head(h, carry):
        r1 = pl.ds(pl.multiple_of(2 * h * PEER_KEYS, PEER_KEYS), PEER_KEYS)
        r2 = pl.ds(pl.multiple_of((2 * h + 1) * PEER_KEYS, PEER_KEYS), PEER_KEYS)
        s1 = jnp.dot(sk_ref[2 * h], qt_scr[r1, :], preferred_element_type=F32)
        s2 = jnp.dot(sk_ref[2 * h + 1], qt_scr[r2, :], preferred_element_type=F32)
        project(jnp.minimum(h + 1, PEER_HEADS - 1))
        for st in range(tq // LANES):
            cols = slice(st * LANES, (st + 1) * LANES)
            n1, a, rank2, b = _product_key_select(s1[:, cols], s2[:, cols])
            n_ref[h, :, cols] = n1
            a_ref[h, :, cols] = a
            cj_ref[h, :, cols] = rank2.astype(BF16)
            bj_ref[h, :, cols] = b.astype(BF16)
        return carry

    lax.fori_loop(0, PEER_HEADS, head, 0)


def _peerq_call(h2t, wq_t_bf, sk_bf):
    d, t = h2t.shape
    tq = 512
    big = lambda: pl.BlockSpec((PEER_HEADS, PEER_KEYS, tq), lambda i: (0, 0, i))
    shp = lambda dt: jax.ShapeDtypeStruct((PEER_HEADS, PEER_KEYS, t), dt)
    return pl.pallas_call(
        _peerq_kernel,
        grid=(t // tq,),
        in_specs=[pl.BlockSpec((d, tq), lambda i: (0, i)),
                  _const_spec(wq_t_bf.shape), _const_spec(sk_bf.shape)],
        out_specs=[big(), big(), big(), big()],
        out_shape=[shp(F32), shp(F32), shp(BF16), shp(BF16)],
        scratch_shapes=[pltpu.VMEM((wq_t_bf.shape[0], tq), BF16)],
        compiler_params=_cparams("arbitrary"),
    )(h2t, wq_t_bf, sk_bf)


PEER_STRIP = 256


PEER_VALUE_K = 256


def _peer_kernel(h2t_ref, u_ref, vt_ref, n_ref, a_ref, cj_ref, bj_ref, acc_ref, s_even, s_odd, c_scr):
    c = pl.program_id(1)
    last = pl.num_programs(1) - 1
    ce, tm = s_even.shape

    nsub = ce // PEER_KEYS
    chunks_per_block = n_ref.shape[1] // nsub

    def step(s_write, s_read, parity, do_score=True, do_finish=True):
        per_slice = PEER_VALUE_K // PEER_KEYS
        half = tm // 2
        row_off = ((parity + 1) % chunks_per_block) * nsub

        def weigh(ks):
            for il in range(ks * per_slice, (ks + 1) * per_slice):
                rows = slice(il * PEER_KEYS, (il + 1) * PEER_KEYS)
                krow = slice(row_off + il, row_off + il + 1)
                for si in range(tm // PEER_STRIP):
                    cols = slice(si * PEER_STRIP, (si + 1) * PEER_STRIP)
                    w = jnp.zeros((PEER_KEYS, PEER_STRIP), BF16)
                    for h in range(PEER_HEADS):
                        nrow = n_ref[h, krow, cols].astype(BF16)
                        arow = a_ref[h, krow, cols].astype(BF16)
                        sel = cj_ref[h, :, cols] < nrow
                        w = w + jnp.where(sel, bj_ref[h, :, cols], jnp.zeros((), BF16)) * arow
                    act = _gelu(s_read[rows, cols]).astype(BF16)
                    c_scr[rows, cols] = w * act

        def value(ks):
            kk = slice(ks * PEER_VALUE_K, (ks + 1) * PEER_VALUE_K)
            acc_ref[...] += jnp.dot(vt_ref[:, kk], c_scr[kk, :], preferred_element_type=F32)

        def score(piece):
            cols = slice(piece * half, (piece + 1) * half)
            s_write[:, cols] = jnp.dot(u_ref[...], h2t_ref[:, cols], preferred_element_type=F32)

        if ce // PEER_VALUE_K == 4:
            order = (("weigh", 0), ("score", 0), ("weigh", 1), ("value", 0), ("weigh", 2), ("value", 1),
                     ("score", 1), ("weigh", 3), ("value", 2), ("value", 3))
        else:
            assert ce // PEER_VALUE_K == 2
            order = (("weigh", 0), ("score", 0), ("weigh", 1), ("value", 0), ("score", 1), ("value", 1))
        for kind, idx in order:
            if kind == "score":
                if do_score:
                    score(idx)
            elif do_finish:
                (weigh if kind == "weigh" else value)(idx)

    @pl.when(c == 0)
    def _():
        acc_ref[...] = jnp.zeros_like(acc_ref)
        step(s_even, s_odd, 0, do_finish=False)

    @pl.when((c > 0) & (c < last) & (c % 2 == 0))
    def _():
        step(s_even, s_odd, 0)

    @pl.when((c > 0) & (c < last) & (c % 2 == 1))
    def _():
        step(s_odd, s_even, 1)

    @pl.when((c == last) & (c % 2 == 0))
    def _():
        step(s_even, s_odd, 0, do_score=False)

    @pl.when((c == last) & (c % 2 == 1))
    def _():
        step(s_odd, s_even, 1, do_score=False)


def _peer_call(h2t, u_bf, vt_bf, nt, at, cjt, bjt):
    d, t = h2t.shape
    ne = u_bf.shape[0]
    tm = 512
    ce = 1024
    nc = ne // ce
    chunks_per_block = SUBLANES * PEER_KEYS // ce
    assert chunks_per_block in (1, 2) and t % tm == 0
    prev = lambda c: jnp.maximum(c - 1, 0)
    sub = lambda: pl.BlockSpec((PEER_HEADS, SUBLANES, tm), lambda i, c: (0, prev(c) // chunks_per_block, i))
    allk = lambda: pl.BlockSpec((PEER_HEADS, PEER_KEYS, tm), lambda i, c: (0, 0, i))
    return pl.pallas_call(
        _peer_kernel,
        grid=(t // tm, nc + 1),
        in_specs=[pl.BlockSpec((d, tm), lambda i, c: (0, i)),
                  pl.BlockSpec((ce, d), lambda i, c: (jnp.minimum(c, nc - 1), 0)),
                  pl.BlockSpec((d, ce), lambda i, c: (0, prev(c))),
                  sub(), sub(), allk(), allk()],
        out_specs=pl.BlockSpec((d, tm), lambda i, c: (0, i)),
        out_shape=jax.ShapeDtypeStruct((d, t), F32),
        scratch_shapes=[pltpu.VMEM((ce, tm), F32), pltpu.VMEM((ce, tm), F32), pltpu.VMEM((ce, tm), BF16)],
        compiler_params=_cparams("arbitrary", "arbitrary"),
    )(h2t, u_bf, vt_bf, nt, at, cjt, bjt)


def _final_kernel(pt_ref, x1_ref, mod_ref, gf_ref, ya_ref, yb_ref, *, n_first):
    g2 = mod_ref[...][5:6]
    x2 = x1_ref[...] + g2 * pt_ref[...].T
    y = _rms(x2, gf_ref[...])

    @pl.when(pl.program_id(0) < n_first)
    def _():
        ya_ref[...] = y

    @pl.when(pl.program_id(0) >= n_first)
    def _():
        yb_ref[...] = y


def _final_call(peer_t, x1, mod3, gf, n_first):
    b, s, d = x1.shape
    tm = 512
    nt = s // tm
    tok = lambda i, j: (i, j, 0)
    return pl.pallas_call(
        functools.partial(_final_kernel, n_first=n_first),
        grid=(b, nt),
        in_specs=[pl.BlockSpec((d, tm), lambda i, j: (0, i * nt + j)),
                  pl.BlockSpec((None, tm, d), tok),
                  pl.BlockSpec((None, N_MOD, d), lambda i, j: (i, 0, 0)),
                  _const_spec((1, d))],
        out_specs=list(_two_group_specs((None, tm, d), n_first, nt)),
        out_shape=[jax.ShapeDtypeStruct((n_first, s, d), F32), jax.ShapeDtypeStruct((b - n_first, s, d), F32)],
        compiler_params=_cparams("arbitrary", "arbitrary"),
    )(peer_t, x1, mod3, gf)


def _layer(xa, xb2, c, w_mod, b_mod, norm1_g, w_in, conv_w, conv_b, lru_w_r, lru_b_r, lru_w_i, lru_b_i,
           lru_lambda, attn_out_g, lru_out_g, w_out, norm2_g, peer_w_q, peer_sub_keys, peer_u, peer_v,
           norm_final_g):
    n_first, s, d = xa.shape
    b = n_first + xb2.shape[0]
    c8 = jnp.pad(c, ((0, (-b) % SUBLANES), (0, 0)))
    mod = _mod_call(c8, w_mod, b_mod[None, :])
    mod3 = mod.reshape(c8.shape[0], N_MOD, d)[:b]
    cosf, sinf = _rope_call(s)
    *qkv, xb, gate = _inproj_call(xa, xb2, mod3, norm1_g[None, :], w_in.astype(BF16), cosf, sinf)
    att = _attention(*qkv)
    hf, hr = _lru_call(xb, conv_w, conv_b[None, :], lru_w_r.astype(BF16), lru_w_i.astype(BF16),
                       lru_b_r, lru_b_i, lru_lambda)
    x1, h2t = _outproj_call(att, hf, hr, gate, xa, xb2, mod3, attn_out_g[None, :], lru_out_g[None, :],
                            w_out.astype(BF16), norm2_g[None, :])
    sk = peer_sub_keys.reshape(PEER_HEADS * 2, PEER_KEYS, -1).astype(BF16)
    nt, at, cjt, bjt = _peerq_call(h2t, peer_w_q.T.astype(BF16), sk)
    peer_t = _peer_call(h2t, peer_u.astype(BF16), peer_v.T.astype(BF16), nt, at, cjt, bjt)
    return _final_call(peer_t, x1, mod3, norm_final_g[None, :], n_first)


def kernel(x_prompt, x_sample, c_prompt, c_sample, w_mod, b_mod, norm1_g, w_in, conv_w, conv_b, lru_w_r,
           lru_b_r, lru_w_i, lru_b_i, lru_lambda, attn_out_g, lru_out_g, w_out, norm2_g, peer_w_q,
           peer_sub_keys, peer_u, peer_v, norm_final_g):
    assert w_mod.shape[0] == 1, "single-layer stack expected"
    assert x_prompt.shape[1:] == x_sample.shape[1:]
    c = jnp.concatenate([c_prompt, c_sample], axis=0)
    y_prompt, y_sample = _layer(
        x_prompt, x_sample, c, w_mod[0], b_mod[0], norm1_g[0], w_in[0], conv_w[0], conv_b[0], lru_w_r[0],
        lru_b_r[0], lru_w_i[0], lru_b_i[0], lru_lambda[0], attn_out_g[0], lru_out_g[0], w_out[0], norm2_g[0],
        peer_w_q[0], peer_sub_keys[0], peer_u[0], peer_v[0], norm_final_g)
    return (y_prompt, y_sample)
```

```python
import functools
import math

import jax
import jax.numpy as jnp
from jax import lax
from jax.experimental import pallas as pl
from jax.experimental.pallas import tpu as pltpu

F32 = jnp.float32
BF16 = jnp.bfloat16

HEAD_DIM = 128
N_HEADS = 8
ATTN_WIDTH = N_HEADS * HEAD_DIM
LRU_BLOCK = 128
DILATIONS = (16, 4, 1)
BAND_RADIUS = 64
ROPE_THETA = 10000.0
LRU_C = 8.0
PEER_HEADS = 8
PEER_KEYS = 128
PEER_TOPK = 16
N_MOD = 6
EPS = 1e-6
MASK_VALUE = -1e30
NEG_INF = float("-inf")

V7X_VMEM_BYTES = 64 * 1024 * 1024
VMEM_LIMIT = V7X_VMEM_BYTES - 8 * 1024 * 1024
LANES = 128
SUBLANES = 8


def _cparams(*sem):
    return pltpu.CompilerParams(dimension_semantics=sem, vmem_limit_bytes=VMEM_LIMIT)


def _const_spec(shape):
    nd = len(shape)
    return pl.BlockSpec(shape, lambda *_: (0,) * nd, pipeline_mode=pl.Buffered(1))


def _two_group_specs(block, n_first, n_tiles):
    first = pl.BlockSpec(block, lambda i, j: (jnp.minimum(i, n_first - 1), jnp.where(i < n_first, j, n_tiles - 1), 0))
    second = pl.BlockSpec(block, lambda i, j: (jnp.maximum(i - n_first, 0), jnp.where(i < n_first, 0, j), 0))
    return first, second


def _rms(x, g):
    return x * lax.rsqrt(jnp.mean(x * x, axis=-1, keepdims=True) + EPS) * g


def _gelu(x):
    alpha = -2.0 * math.sqrt(2.0 / math.pi) * math.log2(math.e)
    e = jnp.exp2(x * (alpha + (alpha * 0.044715) * (x * x)))
    return x * (1.0 / (1.0 + e))


def _mod_kernel(c_ref, w_ref, b_ref, o_ref):
    c = c_ref[...]
    sc = c * jax.nn.sigmoid(c)
    o_ref[...] = jnp.dot(sc, w_ref[...], precision=lax.Precision.HIGHEST,
                         preferred_element_type=F32) + b_ref[...]


def _mod_call(c8, w_mod, b_mod):
    rows, d = c8.shape
    n = w_mod.shape[1]
    tn = 1024
    return pl.pallas_call(
        _mod_kernel,
        grid=(n // tn,),
        in_specs=[pl.BlockSpec((rows, d), lambda j: (0, 0)),
                  pl.BlockSpec((d, tn), lambda j: (0, j)),
                  pl.BlockSpec((1, tn), lambda j: (0, j))],
        out_specs=pl.BlockSpec((rows, tn), lambda j: (0, j)),
        out_shape=jax.ShapeDtypeStruct((rows, n), F32),
        compiler_params=_cparams("arbitrary"),
    )(c8, w_mod, b_mod)


def _rope_kernel(inv_ref, cos_ref, sin_ref):
    ts = cos_ref.shape[0]
    pos = (pl.program_id(0) * ts + lax.broadcasted_iota(jnp.int32, (ts, HEAD_DIM), 0)).astype(F32)
    lane = lax.broadcasted_iota(jnp.int32, (ts, HEAD_DIM), 1)
    ang = pos * inv_ref[...]
    cos_ref[...] = jnp.cos(ang)
    sn = jnp.sin(ang)
    sin_ref[...] = jnp.where(lane < HEAD_DIM // 2, -sn, sn)


def _rope_call(seq):
    half = HEAD_DIM // 2
    inv = ROPE_THETA ** (-jnp.arange(half, dtype=F32) / half)
    inv2 = jnp.concatenate([inv, inv])[None, :]
    ts = min(seq, 1024)
    return pl.pallas_call(
        _rope_kernel,
        grid=(seq // ts,),
        in_specs=[pl.BlockSpec((1, HEAD_DIM), lambda i: (0, 0))],
        out_specs=[pl.BlockSpec((ts, HEAD_DIM), lambda i: (i, 0))] * 2,
        out_shape=[jax.ShapeDtypeStruct((seq, HEAD_DIM), F32)] * 2,
        compiler_params=_cparams("arbitrary"),
    )(inv2)


def _inproj_kernel(xa_ref, xb2_ref, mod_ref, g_ref, w_ref, cos_ref, sin_ref, p4_ref, p16_ref,
                   q1_ref, k1_ref, v1_ref, q4_ref, k4_ref, v4_ref, q16_ref, k16_ref, v16_ref,
                   xb_ref, gate_ref, *, n_first):
    x = jnp.where(pl.program_id(0) < n_first, xa_ref[...], xb2_ref[...])
    modv = mod_ref[...]
    sh1, sc1 = modv[0:1], modv[1:2]
    h = _rms(x, g_ref[...]) * (1.0 + sc1) + sh1
    hb = h.astype(BF16)
    cosv = cos_ref[...]
    sinv = sin_ref[...]
    aw = ATTN_WIDTH

    def rope(z, scale):
        parts = []
        for hd in range(N_HEADS):
            zs = z[:, hd * HEAD_DIM:(hd + 1) * HEAD_DIM]
            rot = pltpu.roll(zs, HEAD_DIM // 2, axis=1)
            r = zs * cosv + rot * sinv
            if scale != 1.0:
                r = r * scale
            parts.append(r.astype(BF16))
        return jnp.concatenate(parts, axis=1)

    def store_layouts(zb, o1_ref, o4_ref, o16_ref):
        o1_ref[...] = zb
        for p_ref, o_ref in ((p4_ref, o4_ref), (p16_ref, o16_ref)):
            perm = jnp.dot(p_ref[...], zb, preferred_element_type=F32).astype(BF16)
            o_ref[...] = perm.reshape(o_ref.shape)

    store_layouts(rope(jnp.dot(hb, w_ref[:, 0:aw], preferred_element_type=F32), HEAD_DIM ** -0.5 * math.log2(math.e)),
                  q1_ref, q4_ref, q16_ref)
    store_layouts(rope(jnp.dot(hb, w_ref[:, aw:2 * aw], preferred_element_type=F32), 1.0),
                  k1_ref, k4_ref, k16_ref)
    store_layouts(jnp.dot(hb, w_ref[:, 2 * aw:3 * aw], preferred_element_type=F32).astype(BF16),
                  v1_ref, v4_ref, v16_ref)
    lw = xb_ref.shape[-1]
    xb_ref[...] = jnp.dot(hb, w_ref[:, 3 * aw:3 * aw + lw], preferred_element_type=F32)
    gate_ref[...] = jnp.dot(hb, w_ref[:, 3 * aw + lw:3 * aw + 2 * lw],
                            preferred_element_type=F32).astype(gate_ref.dtype)


def _residue_perm(tm, dil):
    src = jnp.arange(tm)
    dst = (src % dil) * (tm // dil) + src // dil
    return (jnp.arange(tm)[:, None] == dst[None, :]).astype(BF16)


def _inproj_call(xa, xb2, mod3, g1, w_in_bf, cosf, sinf):
    n_first, s, d = xa.shape
    b = n_first + xb2.shape[0]
    aw = ATTN_WIDTH
    lw = (w_in_bf.shape[1] - 3 * aw) // 2
    tm = 256
    tok = lambda i, j: (i, j, 0)
    grouped = lambda i, j: (i, 0, j, 0)
    lay1 = pl.BlockSpec((None, tm, aw), tok)
    lay4 = pl.BlockSpec((None, 4, tm // 4, aw), grouped)
    lay16 = pl.BlockSpec((None, 16, tm // 16, aw), grouped)
    shp1 = jax.ShapeDtypeStruct((b, s, aw), BF16)
    shp4 = jax.ShapeDtypeStruct((b, 4, s // 4, aw), BF16)
    shp16 = jax.ShapeDtypeStruct((b, 16, s // 16, aw), BF16)
    return pl.pallas_call(
        functools.partial(_inproj_kernel, n_first=n_first),
        grid=(b, s // tm),
        in_specs=[*_two_group_specs((None, tm, d), n_first, s // tm),
                  pl.BlockSpec((None, N_MOD, d), lambda i, j: (i, 0, 0)),
                  _const_spec((1, d)),
                  _const_spec(w_in_bf.shape),
                  pl.BlockSpec((tm, HEAD_DIM), lambda i, j: (j, 0)),
                  pl.BlockSpec((tm, HEAD_DIM), lambda i, j: (j, 0)),
                  _const_spec((tm, tm)), _const_spec((tm, tm))],
        out_specs=[lay1] * 3 + [lay4] * 3 + [lay16] * 3 + [pl.BlockSpec((None, tm, lw), tok)] * 2,
        out_shape=[shp1] * 3 + [shp4] * 3 + [shp16] * 3
        + [jax.ShapeDtypeStruct((b, s, lw), F32), jax.ShapeDtypeStruct((b, s, lw), BF16)],
        compiler_params=_cparams("arbitrary", "arbitrary"),
    )(xa, xb2, mod3, g1, w_in_bf, cosf, sinf, _residue_perm(tm, 4), _residue_perm(tm, 16))


ATT_QB = 128
ATT_KW = ATT_QB + 2 * BAND_RADIUS


ATT_NB = 2048


def _attn_kernel(q1_ref, k1_ref, v1_ref, q4_ref, k4_ref, v4_ref, q16_ref, k16_ref, v16_ref, y_ref,
                 o_scr, l_scr, *, seq):
    blk = pl.program_id(2)
    nb = y_ref.shape[0]
    rel0 = (lax.broadcasted_iota(jnp.int32, (ATT_QB, ATT_KW), 1)
            - lax.broadcasted_iota(jnp.int32, (ATT_QB, ATT_KW), 0))
    branches = ((q16_ref, k16_ref, v16_ref, 16), (q4_ref, k4_ref, v4_ref, 4), (q1_ref, k1_ref, v1_ref, 1))
    for bi, (q_ref, k_ref, v_ref, dil) in enumerate(branches):
        per = nb // dil
        nj = per // ATT_QB
        sl = seq // dil

        def sub(idx, carry, bi=bi, q_ref=q_ref, k_ref=k_ref, v_ref=v_ref, dil=dil, per=per, nj=nj, sl=sl):
            r = idx // nj
            j = idx % nj
            l0 = blk * per + j * ATT_QB
            ks = pl.multiple_of(jnp.clip(l0 - BAND_RADIUS, 0, sl - ATT_KW), BAND_RADIUS)
            qrows = pl.ds(pl.multiple_of(j * ATT_QB, ATT_QB), ATT_QB)
            if dil == 1:
                q, kw, vw = q_ref[qrows, :], k_ref[pl.ds(ks, ATT_KW), :], v_ref[pl.ds(ks, ATT_KW), :]
            else:
                q, kw, vw = q_ref[r, qrows, :], k_ref[r, pl.ds(ks, ATT_KW), :], v_ref[r, pl.ds(ks, ATT_KW), :]
            s = lax.dot_general(q, kw, (((1,), (1,)), ((), ())), preferred_element_type=F32)
            shifted = (rel0 + (ks - l0 + BAND_RADIUS)).astype(jnp.uint32)
            s = jnp.where(shifted <= 2 * BAND_RADIUS, s, MASK_VALUE)
            m = jnp.max(s, axis=-1, keepdims=True)
            p = jnp.exp2(s - m)
            den = jnp.sum(p, axis=-1, keepdims=True)
            o = jnp.dot(p.astype(BF16), vw, preferred_element_type=F32) * (1.0 / den)
            lse = jnp.broadcast_to(m + jnp.log2(den), (ATT_QB, HEAD_DIM))
            if dil != 1:
                dst = pl.ds(j * (ATT_QB * dil) + r, ATT_QB, stride=dil)
                o_scr[bi, dst, :] = o
                l_scr[bi, dst, :] = lse
            else:
                l0, l1 = l_scr[0, qrows, :], l_scr[1, qrows, :]
                mx = jnp.maximum(jnp.maximum(l0, l1), lse)
                w0, w1, w2 = jnp.exp2(l0 - mx), jnp.exp2(l1 - mx), jnp.exp2(lse - mx)
                y = (o_scr[0, qrows, :] * w0 + o_scr[1, qrows, :] * w1 + o * w2) * (1.0 / (w0 + w1 + w2))
                y_ref[qrows, :] = y.astype(y_ref.dtype)
            return carry

        lax.fori_loop(0, dil * nj, sub, 0, unroll=True)


def _attention(q1, k1, v1, q4, k4, v4, q16, k16, v16):
    b, s, aw = q1.shape
    nb = min(s, ATT_NB)
    assert s % nb == 0 and s // 16 >= ATT_KW and nb % (16 * ATT_QB) == 0
    nat_q = pl.BlockSpec((None, nb, HEAD_DIM), lambda i, h, j: (i, j, h))
    nat_kv = pl.BlockSpec((None, s, HEAD_DIM), lambda i, h, j: (i, 0, h))
    grp_q = lambda dil: pl.BlockSpec((None, dil, nb // dil, HEAD_DIM), lambda i, h, j: (i, 0, j, h))
    grp_kv = lambda dil: pl.BlockSpec((None, dil, s // dil, HEAD_DIM), lambda i, h, j: (i, 0, 0, h))
    return pl.pallas_call(
        functools.partial(_attn_kernel, seq=s),
        grid=(b, N_HEADS, s // nb),
        in_specs=[nat_q, nat_kv, nat_kv, grp_q(4), grp_kv(4), grp_kv(4), grp_q(16), grp_kv(16), grp_kv(16)],
        out_specs=pl.BlockSpec((None, nb, HEAD_DIM), lambda i, h, j: (i, j, h)),
        out_shape=jax.ShapeDtypeStruct((b, s, aw), BF16),
        scratch_shapes=[pltpu.VMEM((len(DILATIONS) - 1, nb, HEAD_DIM), F32),
                        pltpu.VMEM((len(DILATIONS) - 1, nb, HEAD_DIM), F32)],
        compiler_params=_cparams("arbitrary", "arbitrary", "arbitrary"),
    )(q1, k1, v1, q4, k4, v4, q16, k16, v16)


def _lru_kernel(xf_ref, xfb_ref, xfa_ref, xr_ref, xrb_ref, xra_ref, cw_ref, cb_ref, wr_ref, wi_ref,
                br_ref, bi_ref, lam_ref, hf_ref, hr_ref, a_scr, b_scr, carry_scr):
    si = pl.program_id(1)
    ns = pl.num_programs(1)
    ts, w = xf_ref.shape
    ngroups = ts // SUBLANES

    @pl.when(si == 0)
    def _():
        carry_scr[...] = jnp.zeros_like(carry_scr)

    sub = lax.broadcasted_iota(jnp.int32, (ngroups, SUBLANES, LRU_BLOCK), 1)
    cw = cw_ref[...]
    cb = cb_ref[...]

    def prepare(dirn, main_ref, before_ref, after_ref, tile):
        main = main_ref[...]
        before = jnp.where(tile == 0, 0.0, before_ref[...])
        after = jnp.where(tile == ns - 1, 0.0, after_ref[...])
        row8 = lax.broadcasted_iota(jnp.int32, (SUBLANES, w), 0)
        xm1 = pltpu.roll(main, 1, axis=0)
        xm1 = jnp.concatenate([jnp.where(row8 == 0, before[7:8], xm1[0:SUBLANES]), xm1[SUBLANES:]], axis=0)
        xm2 = pltpu.roll(main, 2, axis=0)
        xm2 = jnp.concatenate([jnp.where(row8 == 0, before[6:7],
                                         jnp.where(row8 == 1, before[7:8], xm2[0:SUBLANES])),
                               xm2[SUBLANES:]], axis=0)
        xp1 = pltpu.roll(main, ts - 1, axis=0)
        xp1 = jnp.concatenate([xp1[:ts - SUBLANES],
                               jnp.where(row8 == SUBLANES - 1, after[0:1], xp1[ts - SUBLANES:])], axis=0)
        xc = cb + xm2 * cw[0:1] + xm1 * cw[1:2] + main * cw[2:3] + xp1 * cw[3:4]
        xcb = xc.astype(BF16)
        lam = lam_ref[dirn:dirn + 1, :]
        sp = jnp.maximum(-lam, 0.0) + jnp.log1p(jnp.exp(-jnp.abs(lam)))
        decay = (-LRU_C * math.log2(math.e)) * sp
        for n in range(w // LRU_BLOCK):
            cols = slice(n * LRU_BLOCK, (n + 1) * LRU_BLOCK)
            blk = xcb[:, cols]
            r = jax.nn.sigmoid(jnp.dot(blk, wr_ref[dirn, n], preferred_element_type=F32)
                               + br_ref[dirn:dirn + 1, cols])
            gi = jax.nn.sigmoid(jnp.dot(blk, wi_ref[dirn, n], preferred_element_type=F32)
                                + bi_ref[dirn:dirn + 1, cols])
            a = jnp.exp2(r * decay[:, cols])
            bt = jnp.sqrt(jnp.maximum(1.0 - a * a, 0.0)) * (gi * xc[:, cols])
            a = a.reshape(ngroups, SUBLANES, LRU_BLOCK)
            bt = bt.reshape(ngroups, SUBLANES, LRU_BLOCK)
            for shift in (1, 2, 4):
                if dirn == 0:
                    a_sh = pltpu.roll(a, shift, axis=1)
                    b_sh = pltpu.roll(bt, shift, axis=1)
                    ok = sub >= shift
                else:
                    a_sh = pltpu.roll(a, SUBLANES - shift, axis=1)
                    b_sh = pltpu.roll(bt, SUBLANES - shift, axis=1)
                    ok = sub < SUBLANES - shift
                bt = jnp.where(ok, a * b_sh + bt, bt)
                a = jnp.where(ok, a * a_sh, a)
            a_scr[dirn, :, cols] = a.reshape(ts, LRU_BLOCK)
            b_scr[dirn, :, cols] = bt.reshape(ts, LRU_BLOCK)

    prepare(0, xf_ref, xfb_ref, xfa_ref, si)
    prepare(1, xr_ref, xrb_ref, xra_ref, ns - 1 - si)

    def body(g, carry):
        cf, cr = carry
        rf = pl.ds(pl.multiple_of(g * SUBLANES, SUBLANES), SUBLANES)
        hf = b_scr[0, rf, :] + a_scr[0, rf, :] * cf
        hf_ref[rf, :] = hf
        cf = jnp.broadcast_to(hf[SUBLANES - 1:SUBLANES, :], (SUBLANES, w))
        rr = pl.ds(pl.multiple_of((ngroups - 1 - g) * SUBLANES, SUBLANES), SUBLANES)
        hr = b_scr[1, rr, :] + a_scr[1, rr, :] * cr
        hr_ref[rr, :] = hr
        cr = jnp.broadcast_to(hr[0:1, :], (SUBLANES, w))
        return cf, cr

    cf, cr = lax.fori_loop(0, ngroups, body, (carry_scr[0], carry_scr[1]))
    carry_scr[0] = cf
    carry_scr[1] = cr


def _lru_call(xb, conv_w, conv_b, wr_bf, wi_bf, b_r, b_i, lam):
    b, s, w = xb.shape
    ts = 512
    ns = s // ts
    hb = ts // SUBLANES
    nh = s // SUBLANES
    fwd = lambda i, j: (i, j, 0)
    rev = lambda i, j: (i, ns - 1 - j, 0)
    fwd_before = lambda i, j: (i, jnp.maximum(j * hb - 1, 0), 0)
    fwd_after = lambda i, j: (i, jnp.minimum((j + 1) * hb, nh - 1), 0)
    rev_before = lambda i, j: (i, jnp.maximum((ns - 1 - j) * hb - 1, 0), 0)
    rev_after = lambda i, j: (i, jnp.minimum((ns - j) * hb, nh - 1), 0)
    main = lambda im: pl.BlockSpec((None, ts, w), im)
    halo = lambda im: pl.BlockSpec((None, SUBLANES, w), im)
    return pl.pallas_call(
        _lru_kernel,
        grid=(b, ns),
        in_specs=[main(fwd), halo(fwd_before), halo(fwd_after),
                  main(rev), halo(rev_before), halo(rev_after),
                  _const_spec(conv_w.shape), _const_spec(conv_b.shape),
                  _const_spec(wr_bf.shape), _const_spec(wi_bf.shape),
                  _const_spec(b_r.shape), _const_spec(b_i.shape), _const_spec(lam.shape)],
        out_specs=[main(fwd), main(rev)],
        out_shape=[jax.ShapeDtypeStruct((b, s, w), F32)] * 2,
        scratch_shapes=[pltpu.VMEM((2, ts, w), F32), pltpu.VMEM((2, ts, w), F32),
                        pltpu.VMEM((2, SUBLANES, w), F32)],
        compiler_params=_cparams("arbitrary", "arbitrary"),
    )(xb, xb, xb, xb, xb, xb, conv_w, conv_b, wr_bf, wi_bf, b_r, b_i, lam)


def _outproj_kernel(att_ref, hf_ref, hr_ref, gate_ref, xa_ref, xb2_ref, mod_ref, ga_ref, gl_ref, w_ref, g2_ref,
                    x1_ref, h2t_ref, *, n_first):
    modv = mod_ref[...]
    g1, sh2, sc2 = modv[2:3], modv[3:4], modv[4:5]
    an = _rms(att_ref[...].astype(F32), ga_ref[...])
    lru = (hf_ref[...] + hr_ref[...]) * _gelu(gate_ref[...].astype(F32))
    ln = _rms(lru, gl_ref[...])
    aw = an.shape[-1]
    mix = (jnp.dot(an.astype(BF16), w_ref[0:aw, :], preferred_element_type=F32)
           + jnp.dot(ln.astype(BF16), w_ref[aw:, :], preferred_element_type=F32))
    x1 = jnp.where(pl.program_id(0) < n_first, xa_ref[...], xb2_ref[...]) + g1 * mix
    x1_ref[...] = x1
    h2 = _rms(x1, g2_ref[...]) * (1.0 + sc2) + sh2
    h2t_ref[...] = h2.T.astype(BF16)


def _outproj_call(att, hf, hr, gate, xa, xb2, mod3, ga, gl, w_out_bf, g2):
    n_first, s, d = xa.shape
    b = n_first + xb2.shape[0]
    aw, lw = att.shape[-1], hf.shape[-1]
    tm = 512
    nt = s // tm
    tok = lambda i, j: (i, j, 0)
    return pl.pallas_call(
        functools.partial(_outproj_kernel, n_first=n_first),
        grid=(b, nt),
        in_specs=[pl.BlockSpec((None, tm, aw), tok), pl.BlockSpec((None, tm, lw), tok),
                  pl.BlockSpec((None, tm, lw), tok), pl.BlockSpec((None, tm, lw), tok),
                  *_two_group_specs((None, tm, d), n_first, nt),
                  pl.BlockSpec((None, N_MOD, d), lambda i, j: (i, 0, 0)),
                  _const_spec((1, aw)), _const_spec((1, lw)), _const_spec(w_out_bf.shape),
                  _const_spec((1, d))],
        out_specs=[pl.BlockSpec((None, tm, d), tok),
                   pl.BlockSpec((d, tm), lambda i, j: (0, i * nt + j))],
        out_shape=[jax.ShapeDtypeStruct((b, s, d), F32), jax.ShapeDtypeStruct((d, b * s), BF16)],
        compiler_params=_cparams("arbitrary", "arbitrary"),
    )(att, hf, hr, gate, xa, xb2, mod3, ga, gl, w_out_bf, g2)


def _bitonic_merge(x):
    j = len(x) // 2
    while j >= 1:
        for i in range(len(x)):
            l = i ^ j
            if l > i:
                x[i], x[l] = jnp.maximum(x[i], x[l]), jnp.minimum(x[i], x[l])
        j //= 2
    return x


def _top16_network(s):
    n, t = s.shape
    x = [s[SUBLANES * v:SUBLANES * (v + 1), :] for v in range(n // SUBLANES)]
    assert len(x) == PEER_TOPK
    k = 2
    while k <= PEER_TOPK:
        j = k // 2
        while j >= 1:
            for i in range(PEER_TOPK):
                l = i ^ j
                if l > i:
                    hi, lo = jnp.maximum(x[i], x[l]), jnp.minimum(x[i], x[l])
                    x[i], x[l] = (hi, lo) if (i & k) == 0 else (lo, hi)
            j //= 2
        k *= 2
    for shift in (4, 2, 1):
        other = [pltpu.roll(v, shift, axis=0) for v in x]
        x = _bitonic_merge([jnp.maximum(x[k], other[PEER_TOPK - 1 - k]) for k in range(PEER_TOPK)])
    row8 = lax.broadcasted_iota(jnp.int32, (SUBLANES, t), 0)
    halves = []
    for base in (0, SUBLANES):
        acc = x[base]
        for k in range(1, SUBLANES):
            acc = jnp.where(row8 == k, x[base + k], acc)
        halves.append(acc)
    return jnp.concatenate(halves, axis=0)


def _product_key_select(s1, s2):
    t = s1.shape[1]
    a16 = _top16_network(s1)
    b16 = _top16_network(s2)
    rank2 = jnp.full(s2.shape, float(PEER_TOPK), F32)
    for kk in reversed(range(PEER_TOPK)):
        rank2 = jnp.where(s2 >= b16[kk:kk + 1, :], float(kk), rank2)
    row8 = lax.broadcasted_iota(jnp.int32, (SUBLANES, t), 0)
    a_lo, a_hi, b_lo, b_hi = a16[0:8], a16[8:16], b16[0:8], b16[8:16]
    arow = lambda i: a16[i:i + 1, :]
    brow = lambda j: b16[j:j + 1, :]
    pieces = [
        arow(0) + b_lo,
        arow(0) + b_hi,
        arow(1) + b_lo,
        a_hi + brow(0),
        jnp.where(row8 < 4, arow(3) + b_lo, a_lo + brow(0)),
        jnp.where(row8 < 4, arow(2) + b_lo, a_lo + brow(1)),
        jnp.where(row8 == 4, arow(2) + b_lo,
                  jnp.where(row8 == 2, arow(4) + b_lo, NEG_INF)),
    ]
    top = a16[0:1, :] + b16[0:1, :]
    z = jnp.zeros((1, t), F32)
    tau = top
    for kk in range(PEER_TOPK):
        m = pieces[0]
        for pc in pieces[1:]:
            m = jnp.maximum(m, pc)
        m = jnp.max(m, axis=0, keepdims=True)
        z = z + jnp.exp(m - top)
        tau = m
        pieces = [jnp.where(pc == m, NEG_INF, pc) for pc in pieces]
    cnt = jnp.zeros((PEER_TOPK, t), F32)
    for l in range(PEER_TOPK):
        cnt = cnt + jnp.where(a16 + b16[l:l + 1, :] >= tau, 1.0, 0.0)
    n1 = jnp.zeros(s1.shape, F32)
    for r in range(PEER_TOPK):
        n1 = jnp.where(s1 == a16[r:r + 1, :], cnt[r:r + 1, :], n1)
    return n1, jnp.exp(s1 - a16[0:1, :]) * (1.0 / z), rank2, jnp.exp(s2 - b16[0:1, :])


def _peerq_kernel(h2t_ref, wq_ref, sk_ref, n_ref, a_ref, cj_ref, bj_ref, qt_scr):
    tq = h2t_ref.shape[1]
    hrows = 2 * PEER_KEYS

    def project(h):
        rows = pl.ds(pl.multiple_of(h * hrows, hrows), hrows)
        qt_scr[rows, :] = jnp.dot(wq_ref[rows, :], h2t_ref[...], preferred_element_type=F32).astype(BF16)

    project(0)

    def head(h, carry):
        r1 = pl.ds(pl.multiple_of(2 * h * PEER_KEYS, PEER_KEYS), PEER_KEYS)
        r2 = pl.ds(pl.multiple_of((2 * h + 1) * PEER_KEYS, PEER_KEYS), PEER_KEYS)
        s1 = jnp.dot(sk_ref[2 * h], qt_scr[r1, :], preferred_element_type=F32)
        s2 = jnp.dot(sk_ref[2 * h + 1], qt_scr[r2, :], preferred_element_type=F32)
        project(jnp.minimum(h + 1, PEER_HEADS - 1))
        for st in range(tq // LANES):
            cols = slice(st * LANES, (st + 1) * LANES)
            n1, a, rank2, b = _product_key_select(s1[:, cols], s2[:, cols])
            n_ref[h, :, cols] = n1
            a_ref[h, :, cols] = a
            cj_ref[h, :, cols] = rank2.astype(BF16)
            bj_ref[h, :, cols] = b.astype(BF16)
        return carry

    lax.fori_loop(0, PEER_HEADS, head, 0)


def _peerq_call(h2t, wq_t_bf, sk_bf):
    d, t = h2t.shape
    tq = 512
    big = lambda: pl.BlockSpec((PEER_HEADS, PEER_KEYS, tq), lambda i: (0, 0, i))
    shp = lambda dt: jax.ShapeDtypeStruct((PEER_HEADS, PEER_KEYS, t), dt)
    return pl.pallas_call(
        _peerq_kernel,
        grid=(t // tq,),
        in_specs=[pl.BlockSpec((d, tq), lambda i: (0, i)),
                  _const_spec(wq_t_bf.shape), _const_spec(sk_bf.shape)],
        out_specs=[big(), big(), big(), big()],
        out_shape=[shp(F32), shp(F32), shp(BF16), shp(BF16)],
        scratch_shapes=[pltpu.VMEM((wq_t_bf.shape[0], tq), BF16)],
        compiler_params=_cparams("arbitrary"),
    )(h2t, wq_t_bf, sk_bf)


PEER_STRIP = 256


PEER_VALUE_K = 256


def _peer_kernel(h2t_ref, u_ref, vt_ref, n_ref, a_ref, cj_ref, bj_ref, acc_ref, s_even, s_odd, c_scr):
    c = pl.program_id(1)
    last = pl.num_programs(1) - 1
    ce, tm = s_even.shape

    nsub = ce // PEER_KEYS
    chunks_per_block = n_ref.shape[1] // nsub

    def step(s_write, s_read, parity, do_score=True, do_finish=True):
        per_slice = PEER_VALUE_K // PEER_KEYS
        half = tm // 2
        row_off = ((parity + 1) % chunks_per_block) * nsub

        def weigh(ks):
            for il in range(ks * per_slice, (ks + 1) * per_slice):
                rows = slice(il * PEER_KEYS, (il + 1) * PEER_KEYS)
                krow = slice(row_off + il, row_off + il + 1)
                for si in range(tm // PEER_STRIP):
                    cols = slice(si * PEER_STRIP, (si + 1) * PEER_STRIP)
                    w = jnp.zeros((PEER_KEYS, PEER_STRIP), BF16)
                    for h in range(PEER_HEADS):
                        nrow = n_ref[h, krow, cols].astype(BF16)
                        arow = a_ref[h, krow, cols].astype(BF16)
                        sel = cj_ref[h, :, cols] < nrow
                        w = w + jnp.where(sel, bj_ref[h, :, cols], jnp.zeros((), BF16)) * arow
                    act = _gelu(s_read[rows, cols]).astype(BF16)
                    c_scr[rows, cols] = w * act

        def value(ks):
            kk = slice(ks * PEER_VALUE_K, (ks + 1) * PEER_VALUE_K)
            acc_ref[...] += jnp.dot(vt_ref[:, kk], c_scr[kk, :], preferred_element_type=F32)

        def score(piece):
            cols = slice(piece * half, (piece + 1) * half)
            s_write[:, cols] = jnp.dot(u_ref[...], h2t_ref[:, cols], preferred_element_type=F32)

        if ce // PEER_VALUE_K == 4:
            order = (("weigh", 0), ("score", 0), ("weigh", 1), ("value", 0), ("weigh", 2), ("value", 1),
                     ("score", 1), ("weigh", 3), ("value", 2), ("value", 3))
        else:
            assert ce // PEER_VALUE_K == 2
            order = (("weigh", 0), ("score", 0), ("weigh", 1), ("value", 0), ("score", 1), ("value", 1))
        for kind, idx in order:
            if kind == "score":
                if do_score:
                    score(idx)
            elif do_finish:
                (weigh if kind == "weigh" else value)(idx)

    @pl.when(c == 0)
    def _():
        acc_ref[...] = jnp.zeros_like(acc_ref)
        step(s_even, s_odd, 0, do_finish=False)

    @pl.when((c > 0) & (c < last) & (c % 2 == 0))
    def _():
        step(s_even, s_odd, 0)

    @pl.when((c > 0) & (c < last) & (c % 2 == 1))
    def _():
        step(s_odd, s_even, 1)

    @pl.when((c == last) & (c % 2 == 0))
    def _():
        step(s_even, s_odd, 0, do_score=False)

    @pl.when((c == last) & (c % 2 == 1))
    def _():
        step(s_odd, s_even, 1, do_score=False)


def _peer_call(h2t, u_bf, vt_bf, nt, at, cjt, bjt):
    d, t = h2t.shape
    ne = u_bf.shape[0]
    tm = 512
    ce = 1024
    nc = ne // ce
    chunks_per_block = SUBLANES * PEER_KEYS // ce
    assert chunks_per_block in (1, 2) and t % tm == 0
    prev = lambda c: jnp.maximum(c - 1, 0)
    sub = lambda: pl.BlockSpec((PEER_HEADS, SUBLANES, tm), lambda i, c: (0, prev(c) // chunks_per_block, i))
    allk = lambda: pl.BlockSpec((PEER_HEADS, PEER_KEYS, tm), lambda i, c: (0, 0, i))
    return pl.pallas_call(
        _peer_kernel,
        grid=(t // tm, nc + 1),
        in_specs=[pl.BlockSpec((d, tm), lambda i, c: (0, i)),
                  pl.BlockSpec((ce, d), lambda i, c: (jnp.minimum(c, nc - 1), 0)),
                  pl.BlockSpec((d, ce), lambda i, c: (0, prev(c))),
                  sub(), sub(), allk(), allk()],
        out_specs=pl.BlockSpec((d, tm), lambda i, c: (0, i)),
        out_shape=jax.ShapeDtypeStruct((d, t), F32),
        scratch_shapes=[pltpu.VMEM((ce, tm), F32), pltpu.VMEM((ce, tm), F32), pltpu.VMEM((ce, tm), BF16)],
        compiler_params=_cparams("arbitrary", "arbitrary"),
    )(h2t, u_bf, vt_bf, nt, at, cjt, bjt)


def _final_kernel(pt_ref, x1_ref, mod_ref, gf_ref, ya_ref, yb_ref, *, n_first):
    g2 = mod_ref[...][5:6]
    x2 = x1_ref[...] + g2 * pt_ref[...].T
    y = _rms(x2, gf_ref[...])

    @pl.when(pl.program_id(0) < n_first)
    def _():
        ya_ref[...] = y

    @pl.when(pl.program_id(0) >= n_first)
    def _():
        yb_ref[...] = y


def _final_call(peer_t, x1, mod3, gf, n_first):
    b, s, d = x1.shape
    tm = 512
    nt = s // tm
    tok = lambda i, j: (i, j, 0)
    return pl.pallas_call(
        functools.partial(_final_kernel, n_first=n_first),
        grid=(b, nt),
        in_specs=[pl.BlockSpec((d, tm), lambda i, j: (0, i * nt + j)),
                  pl.BlockSpec((None, tm, d), tok),
                  pl.BlockSpec((None, N_MOD, d), lambda i, j: (i, 0, 0)),
                  _const_spec((1, d))],
        out_specs=list(_two_group_specs((None, tm, d), n_first, nt)),
        out_shape=[jax.ShapeDtypeStruct((n_first, s, d), F32), jax.ShapeDtypeStruct((b - n_first, s, d), F32)],
        compiler_params=_cparams("arbitrary", "arbitrary"),
    )(peer_t, x1, mod3, gf)


def _layer(xa, xb2, c, w_mod, b_mod, norm1_g, w_in, conv_w, conv_b, lru_w_r, lru_b_r, lru_w_i, lru_b_i,
           lru_lambda, attn_out_g, lru_out_g, w_out, norm2_g, peer_w_q, peer_sub_keys, peer_u, peer_v,
           norm_final_g):
    n_first, s, d = xa.shape
    b = n_first + xb2.shape[0]
    c8 = jnp.pad(c, ((0, (-b) % SUBLANES), (0, 0)))
    mod = _mod_call(c8, w_mod, b_mod[None, :])
    mod3 = mod.reshape(c8.shape[0], N_MOD, d)[:b]
    cosf, sinf = _rope_call(s)
    *qkv, xb, gate = _inproj_call(xa, xb2, mod3, norm1_g[None, :], w_in.astype(BF16), cosf, sinf)
    att = _attention(*qkv)
    hf, hr = _lru_call(xb, conv_w, conv_b[None, :], lru_w_r.astype(BF16), lru_w_i.astype(BF16),
                       lru_b_r, lru_b_i, lru_lambda)
    x1, h2t = _outproj_call(att, hf, hr, gate, xa, xb2, mod3, attn_out_g[None, :], lru_out_g[None, :],
                            w_out.astype(BF16), norm2_g[None, :])
    sk = peer_sub_keys.reshape(PEER_HEADS * 2, PEER_KEYS, -1).astype(BF16)
    nt, at, cjt, bjt = _peerq_call(h2t, peer_w_q.T.astype(BF16), sk)
    peer_t = _peer_call(h2t, peer_u.astype(BF16), peer_v.T.astype(BF16), nt, at, cjt, bjt)
    return _final_call(peer_t, x1, mod3, norm_final_g[None, :], n_first)


def kernel(x_prompt, x_sample, c_prompt, c_sample, w_mod, b_mod, norm1_g, w_in, conv_w, conv_b, lru_w_r,
           lru_b_r, lru_w_i, lru_b_i, lru_lambda, attn_out_g, lru_out_g, w_out, norm2_g, peer_w_q,
           peer_sub_keys, peer_u, peer_v, norm_final_g):
    assert w_mod.shape[0] == 1, "single-layer stack expected"
    assert x_prompt.shape[1:] == x_sample.shape[1:]
    c = jnp.concatenate([c_prompt, c_sample], axis=0)
    y_prompt, y_sample = _layer(
        x_prompt, x_sample, c, w_mod[0], b_mod[0], norm1_g[0], w_in[0], conv_w[0], conv_b[0], lru_w_r[0],
        lru_b_r[0], lru_w_i[0], lru_b_i[0], lru_lambda[0], attn_out_g[0], lru_out_g[0], w_out[0], norm2_g[0],
        peer_w_q[0], peer_sub_keys[0], peer_u[0], peer_v[0], norm_final_g)
    return (y_prompt, y_sample)
```

```python
import functools
import math

import jax
import jax.numpy as jnp
from jax import lax
from jax.experimental import pallas as pl
from jax.experimental.pallas import tpu as pltpu

F32 = jnp.float32
BF16 = jnp.bfloat16

HEAD_DIM = 128
N_HEADS = 8
ATTN_WIDTH = N_HEADS * HEAD_DIM
LRU_BLOCK = 128
DILATIONS = (16, 4, 1)
BAND_RADIUS = 64
ROPE_THETA = 10000.0
LRU_C = 8.0
PEER_HEADS = 8
PEER_KEYS = 128
PEER_TOPK = 16
N_MOD = 6
EPS = 1e-6
MASK_VALUE = -1e30
NEG_INF = float("-inf")

V7X_VMEM_BYTES = 64 * 1024 * 1024
VMEM_LIMIT = V7X_VMEM_BYTES - 8 * 1024 * 1024
LANES = 128
SUBLANES = 8


def _cparams(*sem):
    return pltpu.CompilerParams(dimension_semantics=sem, vmem_limit_bytes=VMEM_LIMIT)


def _const_spec(shape):
    nd = len(shape)
    return pl.BlockSpec(shape, lambda *_: (0,) * nd, pipeline_mode=pl.Buffered(1))


def _two_group_specs(block, n_first, n_tiles):
    first = pl.BlockSpec(block, lambda i, j: (jnp.minimum(i, n_first - 1), jnp.where(i < n_first, j, n_tiles - 1), 0))
    second = pl.BlockSpec(block, lambda i, j: (jnp.maximum(i - n_first, 0), jnp.where(i < n_first, 0, j), 0))
    return first, second


def _rms(x, g):
    return x * lax.rsqrt(jnp.mean(x * x, axis=-1, keepdims=True) + EPS) * g


def _gelu(x):
    alpha = -2.0 * math.sqrt(2.0 / math.pi) * math.log2(math.e)
    e = jnp.exp2(x * (alpha + (alpha * 0.044715) * (x * x)))
    return x * (1.0 / (1.0 + e))


def _mod_kernel(c_ref, w_ref, b_ref, o_ref):
    c = c_ref[...]
    sc = c * jax.nn.sigmoid(c)
    o_ref[...] = jnp.dot(sc, w_ref[...], precision=lax.Precision.HIGHEST,
                         preferred_element_type=F32) + b_ref[...]


def _mod_call(c8, w_mod, b_mod):
    rows, d = c8.shape
    n = w_mod.shape[1]
    tn = 2048
    return pl.pallas_call(
        _mod_kernel,
        grid=(n // tn,),
        in_specs=[pl.BlockSpec((rows, d), lambda j: (0, 0)),
                  pl.BlockSpec((d, tn), lambda j: (0, j)),
                  pl.BlockSpec((1, tn), lambda j: (0, j))],
        out_specs=pl.BlockSpec((rows, tn), lambda j: (0, j)),
        out_shape=jax.ShapeDtypeStruct((rows, n), F32),
        compiler_params=_cparams("arbitrary"),
    )(c8, w_mod, b_mod)


def _rope_kernel(inv_ref, cos_ref, sin_ref):
    ts = cos_ref.shape[0]
    pos = (pl.program_id(0) * ts + lax.broadcasted_iota(jnp.int32, (ts, HEAD_DIM), 0)).astype(F32)
    lane = lax.broadcasted_iota(jnp.int32, (ts, HEAD_DIM), 1)
    ang = pos * inv_ref[...]
    cos_ref[...] = jnp.cos(ang)
    sn = jnp.sin(ang)
    sin_ref[...] = jnp.where(lane < HEAD_DIM // 2, -sn, sn)


def _rope_call(seq):
    half = HEAD_DIM // 2
    inv = ROPE_THETA ** (-jnp.arange(half, dtype=F32) / half)
    inv2 = jnp.concatenate([inv, inv])[None, :]
    ts = min(seq, 1024)
    return pl.pallas_call(
        _rope_kernel,
        grid=(seq // ts,),
        in_specs=[pl.BlockSpec((1, HEAD_DIM), lambda i: (0, 0))],
        out_specs=[pl.BlockSpec((ts, HEAD_DIM), lambda i: (i, 0))] * 2,
        out_shape=[jax.ShapeDtypeStruct((seq, HEAD_DIM), F32)] * 2,
        compiler_params=_cparams("arbitrary"),
    )(inv2)


def _inproj_kernel(xa_ref, xb2_ref, mod_ref, g_ref, w_ref, cos_ref, sin_ref, p4_ref, p16_ref,
                   q1_ref, k1_ref, v1_ref, q4_ref, k4_ref, v4_ref, q16_ref, k16_ref, v16_ref,
                   xb_ref, gate_ref, *, n_first):
    x = jnp.where(pl.program_id(0) < n_first, xa_ref[...], xb2_ref[...])
    modv = mod_ref[...]
    sh1, sc1 = modv[0:1], modv[1:2]
    h = _rms(x, g_ref[...]) * (1.0 + sc1) + sh1
    hb = h.astype(BF16)
    cosv = cos_ref[...]
    sinv = sin_ref[...]
    aw = ATTN_WIDTH

    def rope(z, scale):
        parts = []
        for hd in range(N_HEADS):
            zs = z[:, hd * HEAD_DIM:(hd + 1) * HEAD_DIM]
            rot = pltpu.roll(zs, HEAD_DIM // 2, axis=1)
            r = zs * cosv + rot * sinv
            if scale != 1.0:
                r = r * scale
            parts.append(r.astype(BF16))
        return jnp.concatenate(parts, axis=1)

    def store_layouts(zb, o1_ref, o4_ref, o16_ref):
        o1_ref[...] = zb
        for p_ref, o_ref in ((p4_ref, o4_ref), (p16_ref, o16_ref)):
            perm = jnp.dot(p_ref[...], zb, preferred_element_type=F32).astype(BF16)
            o_ref[...] = perm.reshape(o_ref.shape)

    store_layouts(rope(jnp.dot(hb, w_ref[:, 0:aw], preferred_element_type=F32), HEAD_DIM ** -0.5 * math.log2(math.e)),
                  q1_ref, q4_ref, q16_ref)
    store_layouts(rope(jnp.dot(hb, w_ref[:, aw:2 * aw], preferred_element_type=F32), 1.0),
                  k1_ref, k4_ref, k16_ref)
    store_layouts(jnp.dot(hb, w_ref[:, 2 * aw:3 * aw], preferred_element_type=F32).astype(BF16),
                  v1_ref, v4_ref, v16_ref)
    lw = xb_ref.shape[-1]
    xb_ref[...] = jnp.dot(hb, w_ref[:, 3 * aw:3 * aw + lw], preferred_element_type=F32)
    gate_ref[...] = jnp.dot(hb, w_ref[:, 3 * aw + lw:3 * aw + 2 * lw],
                            preferred_element_type=F32).astype(gate_ref.dtype)


def _residue_perm(tm, dil):
    src = jnp.arange(tm)
    dst = (src % dil) * (tm // dil) + src // dil
    return (jnp.arange(tm)[:, None] == dst[None, :]).astype(BF16)


def _inproj_call(xa, xb2, mod3, g1, w_in_bf, cosf, sinf):
    n_first, s, d = xa.shape
    b = n_first + xb2.shape[0]
    aw = ATTN_WIDTH
    lw = (w_in_bf.shape[1] - 3 * aw) // 2
    tm = 256
    tok = lambda i, j: (i, j, 0)
    grouped = lambda i, j: (i, 0, j, 0)
    lay1 = pl.BlockSpec((None, tm, aw), tok)
    lay4 = pl.BlockSpec((None, 4, tm // 4, aw), grouped)
    lay16 = pl.BlockSpec((None, 16, tm // 16, aw), grouped)
    shp1 = jax.ShapeDtypeStruct((b, s, aw), BF16)
    shp4 = jax.ShapeDtypeStruct((b, 4, s // 4, aw), BF16)
    shp16 = jax.ShapeDtypeStruct((b, 16, s // 16, aw), BF16)
    return pl.pallas_call(
        functools.partial(_inproj_kernel, n_first=n_first),
        grid=(b, s // tm),
        in_specs=[*_two_group_specs((None, tm, d), n_first, s // tm),
                  pl.BlockSpec((None, N_MOD, d), lambda i, j: (i, 0, 0)),
                  _const_spec((1, d)),
                  _const_spec(w_in_bf.shape),
                  pl.BlockSpec((tm, HEAD_DIM), lambda i, j: (j, 0)),
                  pl.BlockSpec((tm, HEAD_DIM), lambda i, j: (j, 0)),
                  _const_spec((tm, tm)), _const_spec((tm, tm))],
        out_specs=[lay1] * 3 + [lay4] * 3 + [lay16] * 3 + [pl.BlockSpec((None, tm, lw), tok)] * 2,
        out_shape=[shp1] * 3 + [shp4] * 3 + [shp16] * 3
        + [jax.ShapeDtypeStruct((b, s, lw), F32), jax.ShapeDtypeStruct((b, s, lw), BF16)],
        compiler_params=_cparams("arbitrary", "arbitrary"),
    )(xa, xb2, mod3, g1, w_in_bf, cosf, sinf, _residue_perm(tm, 4), _residue_perm(tm, 16))


ATT_QB = 128
ATT_KW = ATT_QB + 2 * BAND_RADIUS


ATT_NB = 2048


def _attn_kernel(q1_ref, k1_ref, v1_ref, q4_ref, k4_ref, v4_ref, q16_ref, k16_ref, v16_ref, y_ref,
                 o_scr, l_scr, *, seq):
    blk = pl.program_id(2)
    nb = y_ref.shape[0]
    rel0 = (lax.broadcasted_iota(jnp.int32, (ATT_QB, ATT_KW), 1)
            - lax.broadcasted_iota(jnp.int32, (ATT_QB, ATT_KW), 0))
    branches = ((q16_ref, k16_ref, v16_ref, 16), (q4_ref, k4_ref, v4_ref, 4), (q1_ref, k1_ref, v1_ref, 1))
    for bi, (q_ref, k_ref, v_ref, dil) in enumerate(branches):
        per = nb // dil
        nj = per // ATT_QB
        sl = seq // dil

        def sub(idx, carry, bi=bi, q_ref=q_ref, k_ref=k_ref, v_ref=v_ref, dil=dil, per=per, nj=nj, sl=sl):
            r = idx // nj
            j = idx % nj
            l0 = blk * per + j * ATT_QB
            ks = pl.multiple_of(jnp.clip(l0 - BAND_RADIUS, 0, sl - ATT_KW), BAND_RADIUS)
            qrows = pl.ds(pl.multiple_of(j * ATT_QB, ATT_QB), ATT_QB)
            if dil == 1:
                q, kw, vw = q_ref[qrows, :], k_ref[pl.ds(ks, ATT_KW), :], v_ref[pl.ds(ks, ATT_KW), :]
            else:
                q, kw, vw = q_ref[r, qrows, :], k_ref[r, pl.ds(ks, ATT_KW), :], v_ref[r, pl.ds(ks, ATT_KW), :]
            s = lax.dot_general(q, kw, (((1,), (1,)), ((), ())), preferred_element_type=F32)
            shifted = (rel0 + (ks - l0 + BAND_RADIUS)).astype(jnp.uint32)
            s = jnp.where(shifted <= 2 * BAND_RADIUS, s, MASK_VALUE)
            m = jnp.max(s, axis=-1, keepdims=True)
            p = jnp.exp2(s - m)
            den = jnp.sum(p, axis=-1, keepdims=True)
            o = jnp.dot(p.astype(BF16), vw, preferred_element_type=F32) * (1.0 / den)
            lse = jnp.broadcast_to(m + jnp.log2(den), (ATT_QB, HEAD_DIM))
            if dil != 1:
                dst = pl.ds(j * (ATT_QB * dil) + r, ATT_QB, stride=dil)
                o_scr[bi, dst, :] = o
                l_scr[bi, dst, :] = lse
            else:
                l0, l1 = l_scr[0, qrows, :], l_scr[1, qrows, :]
                mx = jnp.maximum(jnp.maximum(l0, l1), lse)
                w0, w1, w2 = jnp.exp2(l0 - mx), jnp.exp2(l1 - mx), jnp.exp2(lse - mx)
                y = (o_scr[0, qrows, :] * w0 + o_scr[1, qrows, :] * w1 + o * w2) * (1.0 / (w0 + w1 + w2))
                y_ref[qrows, :] = y.astype(y_ref.dtype)
            return carry

        lax.fori_loop(0, dil * nj, sub, 0, unroll=True)


def _attention(q1, k1, v1, q4, k4, v4, q16, k16, v16):
    b, s, aw = q1.shape
    nb = min(s, ATT_NB)
    assert s % nb == 0 and s // 16 >= ATT_KW and nb % (16 * ATT_QB) == 0
    nat_q = pl.BlockSpec((None, nb, HEAD_DIM), lambda i, h, j: (i, j, h))
    nat_kv = pl.BlockSpec((None, s, HEAD_DIM), lambda i, h, j: (i, 0, h))
    grp_q = lambda dil: pl.BlockSpec((None, dil, nb // dil, HEAD_DIM), lambda i, h, j: (i, 0, j, h))
    grp_kv = lambda dil: pl.BlockSpec((None, dil, s // dil, HEAD_DIM), lambda i, h, j: (i, 0, 0, h))
    return pl.pallas_call(
        functools.partial(_attn_kernel, seq=s),
        grid=(b, N_HEADS, s // nb),
        in_specs=[nat_q, nat_kv, nat_kv, grp_q(4), grp_kv(4), grp_kv(4), grp_q(16), grp_kv(16), grp_kv(16)],
        out_specs=pl.BlockSpec((None, nb, HEAD_DIM), lambda i, h, j: (i, j, h)),
        out_shape=jax.ShapeDtypeStruct((b, s, aw), BF16),
        scratch_shapes=[pltpu.VMEM((len(DILATIONS) - 1, nb, HEAD_DIM), F32),
                        pltpu.VMEM((len(DILATIONS) - 1, nb, HEAD_DIM), F32)],
        compiler_params=_cparams("arbitrary", "arbitrary", "arbitrary"),
    )(q1, k1, v1, q4, k4, v4, q16, k16, v16)


def _lru_kernel(xf_ref, xfb_ref, xfa_ref, xr_ref, xrb_ref, xra_ref, cw_ref, cb_ref, wr_ref, wi_ref,
                br_ref, bi_ref, lam_ref, hf_ref, hr_ref, a_scr, b_scr, carry_scr):
    si = pl.program_id(1)
    ns = pl.num_programs(1)
    ts, w = xf_ref.shape
    ngroups = ts // SUBLANES

    @pl.when(si == 0)
    def _():
        carry_scr[...] = jnp.zeros_like(carry_scr)

    sub = lax.broadcasted_iota(jnp.int32, (ngroups, SUBLANES, LRU_BLOCK), 1)
    cw = cw_ref[...]
    cb = cb_ref[...]

    def prepare(dirn, main_ref, before_ref, after_ref, tile):
        main = main_ref[...]
        before = jnp.where(tile == 0, 0.0, before_ref[...])
        after = jnp.where(tile == ns - 1, 0.0, after_ref[...])
        row8 = lax.broadcasted_iota(jnp.int32, (SUBLANES, w), 0)
        xm1 = pltpu.roll(main, 1, axis=0)
        xm1 = jnp.concatenate([jnp.where(row8 == 0, before[7:8], xm1[0:SUBLANES]), xm1[SUBLANES:]], axis=0)
        xm2 = pltpu.roll(main, 2, axis=0)
        xm2 = jnp.concatenate([jnp.where(row8 == 0, before[6:7],
                                         jnp.where(row8 == 1, before[7:8], xm2[0:SUBLANES])),
                               xm2[SUBLANES:]], axis=0)
        xp1 = pltpu.roll(main, ts - 1, axis=0)
        xp1 = jnp.concatenate([xp1[:ts - SUBLANES],
                               jnp.where(row8 == SUBLANES - 1, after[0:1], xp1[ts - SUBLANES:])], axis=0)
        xc = cb + xm2 * cw[0:1] + xm1 * cw[1:2] + main * cw[2:3] + xp1 * cw[3:4]
        xcb = xc.astype(BF16)
        lam = lam_ref[dirn:dirn + 1, :]
        sp = jnp.maximum(-lam, 0.0) + jnp.log1p(jnp.exp(-jnp.abs(lam)))
        decay = (-LRU_C * math.log2(math.e)) * sp
        for n in range(w // LRU_BLOCK):
            cols = slice(n * LRU_BLOCK, (n + 1) * LRU_BLOCK)
            blk = xcb[:, cols]
            r = jax.nn.sigmoid(jnp.dot(blk, wr_ref[dirn, n], preferred_element_type=F32)
                               + br_ref[dirn:dirn + 1, cols])
            gi = jax.nn.sigmoid(jnp.dot(blk, wi_ref[dirn, n], preferred_element_type=F32)
                                + bi_ref[dirn:dirn + 1, cols])
            a = jnp.exp2(r * decay[:, cols])
            bt = jnp.sqrt(jnp.maximum(1.0 - a * a, 0.0)) * (gi * xc[:, cols])
            a = a.reshape(ngroups, SUBLANES, LRU_BLOCK)
            bt = bt.reshape(ngroups, SUBLANES, LRU_BLOCK)
            for shift in (1, 2, 4):
                if dirn == 0:
                    a_sh = pltpu.roll(a, shift, axis=1)
                    b_sh = pltpu.roll(bt, shift, axis=1)
                    ok = sub >= shift
                else:
                    a_sh = pltpu.roll(a, SUBLANES - shift, axis=1)
                    b_sh = pltpu.roll(bt, SUBLANES - shift, axis=1)
                    ok = sub < SUBLANES - shift
                bt = jnp.where(ok, a * b_sh + bt, bt)
                a = jnp.where(ok, a * a_sh, a)
            a_scr[dirn, :, cols] = a.reshape(ts, LRU_BLOCK)
            b_scr[dirn, :, cols] = bt.reshape(ts, LRU_BLOCK)

    prepare(0, xf_ref, xfb_ref, xfa_ref, si)
    prepare(1, xr_ref, xrb_ref, xra_ref, ns - 1 - si)

    def body(g, carry):
        cf, cr = carry
        rf = pl.ds(pl.multiple_of(g * SUBLANES, SUBLANES), SUBLANES)
        hf = b_scr[0, rf, :] + a_scr[0, rf, :] * cf
        hf_ref[rf, :] = hf
        cf = jnp.broadcast_to(hf[SUBLANES - 1:SUBLANES, :], (SUBLANES, w))
        rr = pl.ds(pl.multiple_of((ngroups - 1 - g) * SUBLANES, SUBLANES), SUBLANES)
        hr = b_scr[1, rr, :] + a_scr[1, rr, :] * cr
        hr_ref[rr, :] = hr
        cr = jnp.broadcast_to(hr[0:1, :], (SUBLANES, w))
        return cf, cr

    cf, cr = lax.fori_loop(0, ngroups, body, (carry_scr[0], carry_scr[1]))
    carry_scr[0] = cf
    carry_scr[1] = cr


def _lru_call(xb, conv_w, conv_b, wr_bf, wi_bf, b_r, b_i, lam):
    b, s, w = xb.shape
    ts = 512
    ns = s // ts
    hb = ts // SUBLANES
    nh = s // SUBLANES
    fwd = lambda i, j: (i, j, 0)
    rev = lambda i, j: (i, ns - 1 - j, 0)
    fwd_before = lambda i, j: (i, jnp.maximum(j * hb - 1, 0), 0)
    fwd_after = lambda i, j: (i, jnp.minimum((j + 1) * hb, nh - 1), 0)
    rev_before = lambda i, j: (i, jnp.maximum((ns - 1 - j) * hb - 1, 0), 0)
    rev_after = lambda i, j: (i, jnp.minimum((ns - j) * hb, nh - 1), 0)
    main = lambda im: pl.BlockSpec((None, ts, w), im)
    halo = lambda im: pl.BlockSpec((None, SUBLANES, w), im)
    return pl.pallas_call(
        _lru_kernel,
        grid=(b, ns),
        in_specs=[main(fwd), halo(fwd_before), halo(fwd_after),
                  main(rev), halo(rev_before), halo(rev_after),
                  _const_spec(conv_w.shape), _const_spec(conv_b.shape),
                  _const_spec(wr_bf.shape), _const_spec(wi_bf.shape),
                  _const_spec(b_r.shape), _const_spec(b_i.shape), _const_spec(lam.shape)],
        out_specs=[main(fwd), main(rev)],
        out_shape=[jax.ShapeDtypeStruct((b, s, w), F32)] * 2,
        scratch_shapes=[pltpu.VMEM((2, ts, w), F32), pltpu.VMEM((2, ts, w), F32),
                        pltpu.VMEM((2, SUBLANES, w), F32)],
        compiler_params=_cparams("arbitrary", "arbitrary"),
    )(xb, xb, xb, xb, xb, xb, conv_w, conv_b, wr_bf, wi_bf, b_r, b_i, lam)


def _outproj_kernel(att_ref, hf_ref, hr_ref, gate_ref, xa_ref, xb2_ref, mod_ref, ga_ref, gl_ref, w_ref, g2_ref,
                    x1_ref, h2t_ref, *, n_first):
    modv = mod_ref[...]
    g1, sh2, sc2 = modv[2:3], modv[3:4], modv[4:5]
    an = _rms(att_ref[...].astype(F32), ga_ref[...])
    lru = (hf_ref[...] + hr_ref[...]) * _gelu(gate_ref[...].astype(F32))
    ln = _rms(lru, gl_ref[...])
    aw = an.shape[-1]
    mix = (jnp.dot(an.astype(BF16), w_ref[0:aw, :], preferred_element_type=F32)
           + jnp.dot(ln.astype(BF16), w_ref[aw:, :], preferred_element_type=F32))
    x1 = jnp.where(pl.program_id(0) < n_first, xa_ref[...], xb2_ref[...]) + g1 * mix
    x1_ref[...] = x1
    h2 = _rms(x1, g2_ref[...]) * (1.0 + sc2) + sh2
    h2t_ref[...] = h2.T.astype(BF16)


def _outproj_call(att, hf, hr, gate, xa, xb2, mod3, ga, gl, w_out_bf, g2):
    n_first, s, d = xa.shape
    b = n_first + xb2.shape[0]
    aw, lw = att.shape[-1], hf.shape[-1]
    tm = 512
    nt = s // tm
    tok = lambda i, j: (i, j, 0)
    return pl.pallas_call(
        functools.partial(_outproj_kernel, n_first=n_first),
        grid=(b, nt),
        in_specs=[pl.BlockSpec((None, tm, aw), tok), pl.BlockSpec((None, tm, lw), tok),
                  pl.BlockSpec((None, tm, lw), tok), pl.BlockSpec((None, tm, lw), tok),
                  *_two_group_specs((None, tm, d), n_first, nt),
                  pl.BlockSpec((None, N_MOD, d), lambda i, j: (i, 0, 0)),
                  _const_spec((1, aw)), _const_spec((1, lw)), _const_spec(w_out_bf.shape),
                  _const_spec((1, d))],
        out_specs=[pl.BlockSpec((None, tm, d), tok),
                   pl.BlockSpec((d, tm), lambda i, j: (0, i * nt + j))],
        out_shape=[jax.ShapeDtypeStruct((b, s, d), F32), jax.ShapeDtypeStruct((d, b * s), BF16)],
        compiler_params=_cparams("arbitrary", "arbitrary"),
    )(att, hf, hr, gate, xa, xb2, mod3, ga, gl, w_out_bf, g2)


def _bitonic_merge(x):
    j = len(x) // 2
    while j >= 1:
        for i in range(len(x)):
            l = i ^ j
            if l > i:
                x[i], x[l] = jnp.maximum(x[i], x[l]), jnp.minimum(x[i], x[l])
        j //= 2
    return x


def _top16_network(s):
    n, t = s.shape
    x = [s[SUBLANES * v:SUBLANES * (v + 1), :] for v in range(n // SUBLANES)]
    assert len(x) == PEER_TOPK
    k = 2
    while k <= PEER_TOPK:
        j = k // 2
        while j >= 1:
            for i in range(PEER_TOPK):
                l = i ^ j
                if l > i:
                    hi, lo = jnp.maximum(x[i], x[l]), jnp.minimum(x[i], x[l])
                    x[i], x[l] = (hi, lo) if (i & k) == 0 else (lo, hi)
            j //= 2
        k *= 2
    for shift in (4, 2, 1):
        other = [pltpu.roll(v, shift, axis=0) for v in x]
        x = _bitonic_merge([jnp.maximum(x[k], other[PEER_TOPK - 1 - k]) for k in range(PEER_TOPK)])
    row8 = lax.broadcasted_iota(jnp.int32, (SUBLANES, t), 0)
    halves = []
    for base in (0, SUBLANES):
        acc = x[base]
        for k in range(1, SUBLANES):
            acc = jnp.where(row8 == k, x[base + k], acc)
        halves.append(acc)
    return jnp.concatenate(halves, axis=0)


def _product_key_select(s1, s2):
    t = s1.shape[1]
    a16 = _top16_network(s1)
    b16 = _top16_network(s2)
    rank2 = jnp.full(s2.shape, float(PEER_TOPK), F32)
    for kk in reversed(range(PEER_TOPK)):
        rank2 = jnp.where(s2 >= b16[kk:kk + 1, :], float(kk), rank2)
    row8 = lax.broadcasted_iota(jnp.int32, (SUBLANES, t), 0)
    a_lo, a_hi, b_lo, b_hi = a16[0:8], a16[8:16], b16[0:8], b16[8:16]
    arow = lambda i: a16[i:i + 1, :]
    brow = lambda j: b16[j:j + 1, :]
    pieces = [
        arow(0) + b_lo,
        arow(0) + b_hi,
        arow(1) + b_lo,
        a_hi + brow(0),
        jnp.where(row8 < 4, arow(3) + b_lo, a_lo + brow(0)),
        jnp.where(row8 < 4, arow(2) + b_lo, a_lo + brow(1)),
        jnp.where(row8 == 4, arow(2) + b_lo,
                  jnp.where(row8 == 2, arow(4) + b_lo, NEG_INF)),
    ]
    top = a16[0:1, :] + b16[0:1, :]
    z = jnp.zeros((1, t), F32)
    tau = top
    for kk in range(PEER_TOPK):
        m = pieces[0]
        for pc in pieces[1:]:
            m = jnp.maximum(m, pc)
        m = jnp.max(m, axis=0, keepdims=True)
        z = z + jnp.exp(m - top)
        tau = m
        pieces = [jnp.where(pc == m, NEG_INF, pc) for pc in pieces]
    cnt = jnp.zeros((PEER_TOPK, t), F32)
    for l in range(PEER_TOPK):
        cnt = cnt + jnp.where(a16 + b16[l:l + 1, :] >= tau, 1.0, 0.0)
    n1 = jnp.zeros(s1.shape, F32)
    for r in range(PEER_TOPK):
        n1 = jnp.where(s1 == a16[r:r + 1, :], cnt[r:r + 1, :], n1)
    return n1, jnp.exp(s1 - a16[0:1, :]) * (1.0 / z), rank2, jnp.exp(s2 - b16[0:1, :])


def _peerq_kernel(h2t_ref, wq_ref, sk_ref, n_ref, a_ref, cj_ref, bj_ref, qt_scr):
    tq = h2t_ref.shape[1]
    hrows = 2 * PEER_KEYS

    def project(h):
        rows = pl.ds(pl.multiple_of(h * hrows, hrows), hrows)
        qt_scr[rows, :] = jnp.dot(wq_ref[rows, :], h2t_ref[...], preferred_element_type=F32).astype(BF16)

    project(0)

    def head(h, carry):
        r1 = pl.ds(pl.multiple_of(2 * h * PEER_KEYS, PEER_KEYS), PEER_KEYS)
        r2 = pl.ds(pl.multiple_of((2 * h + 1) * PEER_KEYS, PEER_KEYS), PEER_KEYS)
        s1 = jnp.dot(sk_ref[2 * h], qt_scr[r1, :], preferred_element_type=F32)
        s2 = jnp.dot(sk_ref[2 * h + 1], qt_scr[r2, :], preferred_element_type=F32)
        project(jnp.minimum(h + 1, PEER_HEADS - 1))
        for st in range(tq // LANES):
            cols = slice(st * LANES, (st + 1) * LANES)
            n1, a, rank2, b = _product_key_select(s1[:, cols], s2[:, cols])
            n_ref[h, :, cols] = n1
            a_ref[h, :, cols] = a
            cj_ref[h, :, cols] = rank2.astype(BF16)
            bj_ref[h, :, cols] = b.astype(BF16)
        return carry

    lax.fori_loop(0, PEER_HEADS, head, 0)


def _peerq_call(h2t, wq_t_bf, sk_bf):
    d, t = h2t.shape
    tq = 512
    big = lambda: pl.BlockSpec((PEER_HEADS, PEER_KEYS, tq), lambda i: (0, 0, i))
    shp = lambda dt: jax.ShapeDtypeStruct((PEER_HEADS, PEER_KEYS, t), dt)
    return pl.pallas_call(
        _peerq_kernel,
        grid=(t // tq,),
        in_specs=[pl.BlockSpec((d, tq), lambda i: (0, i)),
                  _const_spec(wq_t_bf.shape), _const_spec(sk_bf.shape)],
        out_specs=[big(), big(), big(), big()],
        out_shape=[shp(F32), shp(F32), shp(BF16), shp(BF16)],
        scratch_shapes=[pltpu.VMEM((wq_t_bf.shape[0], tq), BF16)],
        compiler_params=_cparams("arbitrary"),
    )(h2t, wq_t_bf, sk_bf)


PEER_STRIP = 256


PEER_VALUE_K = 256


def _peer_kernel(h2t_ref, u_ref, vt_ref, n_ref, a_ref, cj_ref, bj_ref, acc_ref, s_even, s_odd, c_scr):
    c = pl.program_id(1)
    last = pl.num_programs(1) - 1
    ce, tm = s_even.shape

    nsub = ce // PEER_KEYS
    chunks_per_block = n_ref.shape[1] // nsub

    def step(s_write, s_read, parity, do_score=True, do_finish=True):
        per_slice = PEER_VALUE_K // PEER_KEYS
        half = tm // 2
        row_off = ((parity + 1) % chunks_per_block) * nsub

        def weigh(ks):
            for il in range(ks * per_slice, (ks + 1) * per_slice):
                rows = slice(il * PEER_KEYS, (il + 1) * PEER_KEYS)
                krow = slice(row_off + il, row_off + il + 1)
                for si in range(tm // PEER_STRIP):
                    cols = slice(si * PEER_STRIP, (si + 1) * PEER_STRIP)
                    w = jnp.zeros((PEER_KEYS, PEER_STRIP), BF16)
                    for h in range(PEER_HEADS):
                        nrow = n_ref[h, krow, cols].astype(BF16)
                        arow = a_ref[h, krow, cols].astype(BF16)
                        sel = cj_ref[h, :, cols] < nrow
                        w = w + jnp.where(sel, bj_ref[h, :, cols], jnp.zeros((), BF16)) * arow
                    act = _gelu(s_read[rows, cols]).astype(BF16)
                    c_scr[rows, cols] = w * act

        def value(ks):
            kk = slice(ks * PEER_VALUE_K, (ks + 1) * PEER_VALUE_K)
            acc_ref[...] += jnp.dot(vt_ref[:, kk], c_scr[kk, :], preferred_element_type=F32)

        def score(piece):
            cols = slice(piece * half, (piece + 1) * half)
            s_write[:, cols] = jnp.dot(u_ref[...], h2t_ref[:, cols], preferred_element_type=F32)

        if ce // PEER_VALUE_K == 4:
            order = (("weigh", 0), ("score", 0), ("weigh", 1), ("value", 0), ("weigh", 2), ("value", 1),
                     ("score", 1), ("weigh", 3), ("value", 2), ("value", 3))
        else:
            assert ce // PEER_VALUE_K == 2
            order = (("weigh", 0), ("score", 0), ("weigh", 1), ("value", 0), ("score", 1), ("value", 1))
        for kind, idx in order:
            if kind == "score":
                if do_score:
                    score(idx)
            elif do_finish:
                (weigh if kind == "weigh" else value)(idx)

    @pl.when(c == 0)
    def _():
        acc_ref[...] = jnp.zeros_like(acc_ref)
        step(s_even, s_odd, 0, do_finish=False)

    @pl.when((c > 0) & (c < last) & (c % 2 == 0))
    def _():
        step(s_even, s_odd, 0)

    @pl.when((c > 0) & (c < last) & (c % 2 == 1))
    def _():
        step(s_odd, s_even, 1)

    @pl.when((c == last) & (c % 2 == 0))
    def _():
        step(s_even, s_odd, 0, do_score=False)

    @pl.when((c == last) & (c % 2 == 1))
    def _():
        step(s_odd, s_even, 1, do_score=False)


def _peer_call(h2t, u_bf, vt_bf, nt, at, cjt, bjt):
    d, t = h2t.shape
    ne = u_bf.shape[0]
    tm = 512
    ce = 1024
    nc = ne // ce
    chunks_per_block = SUBLANES * PEER_KEYS // ce
    assert chunks_per_block in (1, 2) and t % tm == 0
    prev = lambda c: jnp.maximum(c - 1, 0)
    sub = lambda: pl.BlockSpec((PEER_HEADS, SUBLANES, tm), lambda i, c: (0, prev(c) // chunks_per_block, i))
    allk = lambda: pl.BlockSpec((PEER_HEADS, PEER_KEYS, tm), lambda i, c: (0, 0, i))
    return pl.pallas_call(
        _peer_kernel,
        grid=(t // tm, nc + 1),
        in_specs=[pl.BlockSpec((d, tm), lambda i, c: (0, i)),
                  pl.BlockSpec((ce, d), lambda i, c: (jnp.minimum(c, nc - 1), 0)),
                  pl.BlockSpec((d, ce), lambda i, c: (0, prev(c))),
                  sub(), sub(), allk(), allk()],
        out_specs=pl.BlockSpec((d, tm), lambda i, c: (0, i)),
        out_shape=jax.ShapeDtypeStruct((d, t), F32),
        scratch_shapes=[pltpu.VMEM((ce, tm), F32), pltpu.VMEM((ce, tm), F32), pltpu.VMEM((ce, tm), BF16)],
        compiler_params=_cparams("arbitrary", "arbitrary"),
    )(h2t, u_bf, vt_bf, nt, at, cjt, bjt)


def _final_kernel(pt_ref, x1_ref, mod_ref, gf_ref, ya_ref, yb_ref, *, n_first):
    g2 = mod_ref[...][5:6]
    x2 = x1_ref[...] + g2 * pt_ref[...].T
    y = _rms(x2, gf_ref[...])

    @pl.when(pl.program_id(0) < n_first)
    def _():
        ya_ref[...] = y

    @pl.when(pl.program_id(0) >= n_first)
    def _():
        yb_ref[...] = y


def _final_call(peer_t, x1, mod3, gf, n_first):
    b, s, d = x1.shape
    tm = 512
    nt = s // tm
    tok = lambda i, j: (i, j, 0)
    return pl.pallas_call(
        functools.partial(_final_kernel, n_first=n_first),
        grid=(b, nt),
        in_specs=[pl.BlockSpec((d, tm), lambda i, j: (0, i * nt + j)),
                  pl.BlockSpec((None, tm, d), tok),
                  pl.BlockSpec((None, N_MOD, d), lambda i, j: (i, 0, 0)),
                  _const_spec((1, d))],
        out_specs=list(_two_group_specs((None, tm, d), n_first, nt)),
        out_shape=[jax.ShapeDtypeStruct((n_first, s, d), F32), jax.ShapeDtypeStruct((b - n_first, s, d), F32)],
        compiler_params=_cparams("arbitrary", "arbitrary"),
    )(peer_t, x1, mod3, gf)


def _layer(xa, xb2, c, w_mod, b_mod, norm1_g, w_in, conv_w, conv_b, lru_w_r, lru_b_r, lru_w_i, lru_b_i,
           lru_lambda, attn_out_g, lru_out_g, w_out, norm2_g, peer_w_q, peer_sub_keys, peer_u, peer_v,
           norm_final_g):
    n_first, s, d = xa.shape
    b = n_first + xb2.shape[0]
    c8 = jnp.pad(c, ((0, (-b) % SUBLANES), (0, 0)))
    mod = _mod_call(c8, w_mod, b_mod[None, :])
    mod3 = mod.reshape(c8.shape[0], N_MOD, d)[:b]
    cosf, sinf = _rope_call(s)
    *qkv, xb, gate = _inproj_call(xa, xb2, mod3, norm1_g[None, :], w_in.astype(BF16), cosf, sinf)
    att = _attention(*qkv)
    hf, hr = _lru_call(xb, conv_w, conv_b[None, :], lru_w_r.astype(BF16), lru_w_i.astype(BF16),
                       lru_b_r, lru_b_i, lru_lambda)
    x1, h2t = _outproj_call(att, hf, hr, gate, xa, xb2, mod3, attn_out_g[None, :], lru_out_g[None, :],
                            w_out.astype(BF16), norm2_g[None, :])
    sk = peer_sub_keys.reshape(PEER_HEADS * 2, PEER_KEYS, -1).astype(BF16)
    nt, at, cjt, bjt = _peerq_call(h2t, peer_w_q.T.astype(BF16), sk)
    peer_t = _peer_call(h2t, peer_u.astype(BF16), peer_v.astype(BF16).T, nt, at, cjt, bjt)
    return _final_call(peer_t, x1, mod3, norm_final_g[None, :], n_first)


def kernel(x_prompt, x_sample, c_prompt, c_sample, w_mod, b_mod, norm1_g, w_in, conv_w, conv_b, lru_w_r,
           lru_b_r, lru_w_i, lru_b_i, lru_lambda, attn_out_g, lru_out_g, w_out, norm2_g, peer_w_q,
           peer_sub_keys, peer_u, peer_v, norm_final_g):
    assert w_mod.shape[0] == 1, "single-layer stack expected"
    assert x_prompt.shape[1:] == x_sample.shape[1:]
    c = jnp.concatenate([c_prompt, c_sample], axis=0)
    y_prompt, y_sample = _layer(
        x_prompt, x_sample, c, w_mod[0], b_mod[0], norm1_g[0], w_in[0], conv_w[0], conv_b[0], lru_w_r[0],
        lru_b_r[0], lru_w_i[0], lru_b_i[0], lru_lambda[0], attn_out_g[0], lru_out_g[0], w_out[0], norm2_g[0],
        peer_w_q[0], peer_sub_keys[0], peer_u[0], peer_v[0], norm_final_g)
    return (y_prompt, y_sample)
```
